```python
import math
import jax
import jax.numpy as jnp
from jax import lax
import numpy as np

D_MODEL = 1024
BATCH = 2
SEQ = 8192
DEPTH = 4

GRID_W = 64
CTX_LEN = 256
N_MIXERS = 4
NORM_EPS = 1e-6

MB_D_INNER = 2 * D_MODEL
MB_HEAD_DIM = 64
MB_HEADS = MB_D_INNER // MB_HEAD_DIM
MB_GROUPS = 4
MB_HPG = MB_HEADS // MB_GROUPS
MB_STATE = 128
MB_CONV = 5
MB_CHUNK = 128
MB_XBC = MB_D_INNER + 2 * MB_GROUPS * MB_STATE
MB_IN = MB_D_INNER + MB_XBC + 2 * MB_HEADS

RW_HEAD = 64
RW_HEADS = D_MODEL // RW_HEAD
RW_DECAY_LORA = 64
RW_ICLR_LORA = 64
RW_GATE_LORA = 128
RW_DECAY_SCALE = 0.606531
RW_LN_EPS = 64e-5

POOL_WINDOWS = (2, 4, 8, 16)
POOL_GROUP = D_MODEL // len(POOL_WINDOWS)

AT_HEAD = 64
AT_HEADS = D_MODEL // AT_HEAD
AT_KV_HEADS = 4
AT_GROUP = AT_HEADS // AT_KV_HEADS
AT_BLOCK = 128
ROPE_THETA = 10000.0

FF_DENSE = 2816
N_EXPERTS = 8
TOP_K = 2
FF_EXPERT = 3584

kernel_name = 'hybrid_ssd_rwkv7_pool_gqa_moe_dit'


def _rmsnorm(x, g, eps=NORM_EPS):
    xf = x.astype(jnp.float32)
    y = xf * lax.rsqrt(jnp.mean(xf * xf, axis=-1, keepdims=True) + eps)
    return (y * g.astype(jnp.float32)).astype(x.dtype)


def _flip(t):
    return jnp.flip(t, axis=1)


def _dwconv_centred(u, w, b):
    ch = u.shape[-1]
    pad = (w.shape[0] - 1) // 2
    y = lax.conv_general_dilated(u, w[:, None, :].astype(u.dtype), window_strides=(1,),
                                 padding=[(pad, pad)], dimension_numbers=('NWC', 'WIO', 'NWC'),
                                 feature_group_count=ch)
    return y + b


def _ssd_scan(x, dt, a_neg, bm, cm, s0, need_y):
    f32 = jnp.float32
    b, L, G, E, P = x.shape
    N = bm.shape[-1]
    nc = L // MB_CHUNK
    x = x.astype(f32).reshape(b, nc, MB_CHUNK, G, E, P)
    dt = dt.astype(f32).reshape(b, nc, MB_CHUNK, G, E)
    bm = bm.astype(f32).reshape(b, nc, MB_CHUNK, G, N)
    cm = cm.astype(f32).reshape(b, nc, MB_CHUNK, G, N)
    cum = jnp.cumsum(dt * a_neg, axis=2)
    last = cum[:, :, -1]
    xdt = x * (jnp.exp(last[:, :, None] - cum) * dt)[..., None]
    chunk_states = jnp.einsum('bcqgn,bcqgep->bcgepn', bm, xdt)

    def step(s, inp):
        st, dec = inp
        return s * dec[..., None, None] + st, s

    s_fin, s_prev = lax.scan(step, s0.astype(f32),
                             (jnp.moveaxis(chunk_states, 1, 0), jnp.moveaxis(jnp.exp(last), 1, 0)))
    if not need_y:
        return None, s_fin
    s_prev = jnp.moveaxis(s_prev, 0, 1)
    pos = jnp.arange(MB_CHUNK)
    lower = (pos[:, None] >= pos[None, :])[:, :, None, None]
    seg = cum[:, :, :, None] - cum[:, :, None, :]
    decay = jnp.exp(jnp.where(lower, seg, -jnp.inf))
    cb = jnp.einsum('bcign,bcjgn->bcijg', cm, bm)
    w = cb[..., None] * decay * dt[:, :, None]
    y = jnp.einsum('bcijge,bcjgep->bcigep', w, x)
    y = y + jnp.einsum('bcign,bcgepn->bcigep', cm, s_prev) * jnp.exp(cum)[..., None]
    return y.reshape(b, L, G, E, P), s_fin


def _mamba_project(u, in_w, conv_w, conv_b, dt_bias):
    b, L, _ = u.shape
    zxd = u @ in_w
    z = zxd[..., :MB_D_INNER]
    xbc = jax.nn.silu(_dwconv_centred(zxd[..., MB_D_INNER:MB_D_INNER + MB_XBC], conv_w, conv_b))
    dt_raw = zxd[..., MB_D_INNER + MB_XBC:].reshape(b, L, 2, MB_HEADS)
    gn = MB_GROUPS * MB_STATE
    xs = xbc[..., :MB_D_INNER].reshape(b, L, MB_GROUPS, MB_HPG, MB_HEAD_DIM)
    bm = xbc[..., MB_D_INNER:MB_D_INNER + gn].reshape(b, L, MB_GROUPS, MB_STATE)
    cm = xbc[..., MB_D_INNER + gn:].reshape(b, L, MB_GROUPS, MB_STATE)
    dt = jax.nn.softplus(dt_raw.astype(jnp.float32) + dt_bias.astype(jnp.float32))
    return z, xs, bm, cm, dt.reshape(b, L, 2, MB_GROUPS, MB_HPG)


def _mamba_mixer(h, hc, want_ctx, in_w, conv_w, conv_b, dt_bias, a_log, d_skip, norm_g, out_w):
    f32 = jnp.float32
    a_neg = -jnp.exp(a_log.astype(f32)).reshape(2, MB_GROUPS, MB_HPG)
    s0 = jnp.zeros((h.shape[0], MB_GROUPS, MB_HPG, MB_HEAD_DIM, MB_STATE), f32)

    def bidir(u, init_f, init_b, need_y):
        z, xs, bm, cm, dt = _mamba_project(u, in_w, conv_w, conv_b, dt_bias)
        y_f, s_f = _ssd_scan(xs, dt[:, :, 0], a_neg[0], bm, cm, init_f, need_y)
        y_b, s_b = _ssd_scan(_flip(xs), _flip(dt[:, :, 1]), a_neg[1], _flip(bm), _flip(cm), init_b, need_y)
        if not need_y:
            return None, s_f, s_b
        y = y_f + _flip(y_b) + xs.astype(f32) * d_skip.astype(f32).reshape(MB_GROUPS, MB_HPG, 1)
        y = y.reshape(u.shape[0], u.shape[1], MB_D_INNER).astype(u.dtype)
        return _rmsnorm(y * jax.nn.silu(z), norm_g) @ out_w, s_f, s_b

    yc, sc_f, sc_b = bidir(hc, s0, s0, want_ctx)
    y, _, _ = bidir(h, sc_f, sc_b, True)
    return y, yc


def _centred_shift(u):
    prev = jnp.pad(u, ((0, 0), (1, 0), (0, 0)))[:, :-1]
    nxt = jnp.pad(u, ((0, 0), (0, 1), (0, 0)))[:, 1:]
    return 0.5 * (prev + nxt) - u


def _heads(t):
    return t.reshape(t.shape[0], t.shape[1], RW_HEADS, RW_HEAD)


def _rwkv_features(u, mix, rkv_w, w0, w1, w2, a0, a1, a2, g1, g2, k_k, k_a):
    f32 = jnp.float32
    xx = _centred_shift(u)
    xr, xw, xk, xv, xa, xg = [u + xx * mix[j] for j in range(6)]
    r = xr @ rkv_w[0]
    k = xk @ rkv_w[1]
    v = xv @ rkv_w[2]
    g = jax.nn.sigmoid(xg @ g1) @ g2
    kk = _heads((k * k_k).astype(f32))
    kk = kk / jnp.maximum(jnp.sqrt(jnp.sum(kk * kk, axis=-1, keepdims=True)), 1e-12)
    logw, iclr, kd = [], [], []
    for d in range(2):
        logw.append(_heads(-RW_DECAY_SCALE * jax.nn.sigmoid((w0[d] + jnp.tanh(xw @ w1[d]) @ w2[d]).astype(f32))))
        a = jax.nn.sigmoid((a0[d] + (xa @ a1[d]) @ a2[d]).astype(f32))
        iclr.append(_heads(a))
        kd.append(_heads(k.astype(f32) * (1.0 + (a - 1.0) * k_a.astype(f32))))
    return _heads(r.astype(f32)), _heads(v.astype(f32)), g, kk, logw, iclr, kd


def _rwkv_scan(r, logw, k, v, kk, a, s0, need_y):
    f32 = jnp.float32
    tm = lambda t: jnp.moveaxis(t.astype(f32), 1, 0)
    seq = (tm(r), tm(jnp.exp(logw)), tm(k), tm(v), tm(kk), tm(kk * a))

    def step(S, inp):
        r_t, w_t, k_t, v_t, kk_t, b_t = inp
        sa = jnp.einsum('bhvk,bhk->bhv', S, kk_t)
        S = S * w_t[:, :, None, :] - sa[..., None] * b_t[:, :, None, :] + v_t[..., None] * k_t[:, :, None, :]
        return S, (jnp.einsum('bhvk,bhk->bhv', S, r_t) if need_y else None)

    s_fin, ys = lax.scan(step, s0, seq)
    return (jnp.moveaxis(ys, 0, 1) if need_y else None), s_fin


def _rwkv_mixer(h, hc, want_ctx, mix, rkv_w, w0, w1, w2, a0, a1, a2, g1, g2, k_k, k_a,
                r_k, ln_g, ln_b, out_w):
    f32 = jnp.float32
    s0 = jnp.zeros((h.shape[0], RW_HEADS, RW_HEAD, RW_HEAD), f32)

    def bidir(u, inits, need_y):
        r, v, g, kk, logw, iclr, kd = _rwkv_features(u, mix, rkv_w, w0, w1, w2, a0, a1, a2, g1, g2, k_k, k_a)
        ys, states = [], []
        for d in range(2):
            f = (lambda t: t) if d == 0 else _flip
            y_d, s_d = _rwkv_scan(f(r), f(logw[d]), f(kd[d]), f(v), f(kk), f(iclr[d]), inits[d], need_y)
            ys.append(f(y_d) if need_y else None)
            states.append(s_d)
        if not need_y:
            return None, states
        y = ys[0] + ys[1]
        mu = jnp.mean(y, axis=-1, keepdims=True)
        var = jnp.mean(jnp.square(y - mu), axis=-1, keepdims=True)
        yn = (y - mu) * lax.rsqrt(var + RW_LN_EPS) * ln_g.astype(f32).reshape(RW_HEADS, RW_HEAD) \
            + ln_b.astype(f32).reshape(RW_HEADS, RW_HEAD)
        bonus = jnp.sum(r * r_k.astype(f32) * (0.5 * (kd[0] + kd[1])), axis=-1, keepdims=True) * v
        o = (yn + bonus).reshape(u.shape).astype(u.dtype) * g
        return o @ out_w, states

    yc, sc = bidir(hc, [s0, s0], want_ctx)
    y, _ = bidir(h, sc, True)
    return y, yc


def _pool_mixer(u, w, scale):
    b, L, _ = u.shape
    cs = jnp.cumsum(u.astype(jnp.float32), axis=1)
    cs = jnp.concatenate([jnp.zeros_like(cs[:, :1]), cs], axis=1)
    t = jnp.arange(L)
    means = []
    for gi, win in enumerate(POOL_WINDOWS):
        lo = jnp.clip(t - win // 2, 0, L)
        hi = jnp.clip(t + win // 2, 0, L)
        csg = cs[:, :, gi * POOL_GROUP:(gi + 1) * POOL_GROUP]
        cnt = (hi - lo).astype(jnp.float32)[None, :, None]
        means.append((csg[:, hi] - csg[:, lo]) / cnt)
    pooled = jnp.concatenate(means, axis=-1).astype(u.dtype) - u
    y = jnp.einsum('blgc,gcd->blgd', pooled.reshape(b, L, len(POOL_WINDOWS), POOL_GROUP), w)
    return y.reshape(b, L, -1) * scale


def _axial_rope(x, rows, cols):
    f32 = jnp.float32
    half = x.shape[-1] // 2
    quarter = half // 2
    inv = ROPE_THETA ** (-jnp.arange(quarter, dtype=f32) / quarter)

    def rot(xp, pos):
        ang = pos.astype(f32)[:, None] * inv
        cos = jnp.cos(ang)[None, :, None, :].astype(x.dtype)
        sin = jnp.sin(ang)[None, :, None, :].astype(x.dtype)
        x1, x2 = xp[..., :quarter], xp[..., quarter:]
        return jnp.concatenate([x1 * cos - x2 * sin, x2 * cos + x1 * sin], axis=-1)

    return jnp.concatenate([rot(x[..., :half], rows), rot(x[..., half:], cols)], axis=-1)


def _gqa_attend(q, k, v):
    s = jnp.einsum('bqkgd,bskd->bkgqs', q, k).astype(jnp.float32)
    p = jax.nn.softmax(s, axis=-1).astype(v.dtype)
    return jnp.einsum('bkgqs,bskd->bqkgd', p, v)


def _attn_mixer(h, hc, want_ctx, rows, cols, qkv_w, q_g, k_g, out_w):
    nq = AT_HEADS * AT_HEAD
    nk = AT_KV_HEADS * AT_HEAD
    scale = AT_HEAD ** -0.5

    def proj(u):
        b, L, _ = u.shape
        qkv = u @ qkv_w
        q = _rmsnorm(qkv[..., :nq].reshape(b, L, AT_HEADS, AT_HEAD), q_g)
        k = _rmsnorm(qkv[..., nq:nq + nk].reshape(b, L, AT_KV_HEADS, AT_HEAD), k_g)
        v = qkv[..., nq + nk:].reshape(b, L, AT_KV_HEADS, AT_HEAD)
        return q, k, v

    qc, kc, vc = proj(hc)
    q, k, v = proj(h)
    q = _axial_rope(q, rows, cols) * scale
    k = _axial_rope(k, rows, cols)
    k_all = jnp.concatenate([kc, k], axis=1)
    v_all = jnp.concatenate([vc, v], axis=1)
    b, L = h.shape[:2]
    nblk = L // AT_BLOCK
    qb = jnp.moveaxis(q.reshape(b, nblk, AT_BLOCK, AT_KV_HEADS, AT_GROUP, AT_HEAD), 1, 0)
    o = lax.map(lambda blk: _gqa_attend(blk, k_all, v_all), qb)
    y = jnp.moveaxis(o, 0, 1).reshape(b, L, nq) @ out_w
    yc = None
    if want_ctx:
        bc, lc = hc.shape[:2]
        oc = _gqa_attend(qc.reshape(bc, lc, AT_KV_HEADS, AT_GROUP, AT_HEAD) * scale, kc, vc)
        yc = oc.reshape(bc, lc, nq) @ out_w
    return y, yc


def _swiglu(u, w_in, w_out):
    gu = u @ w_in
    f = w_out.shape[0]
    return (jax.nn.silu(gu[..., :f]) * gu[..., f:]) @ w_out


def _moe(u, router_w, w_in, w_out):
    logits = (u @ router_w).astype(jnp.float32)
    top_v, top_i = lax.top_k(logits, TOP_K)
    wts = jax.nn.softmax(top_v, axis=-1)
    gates = jnp.einsum('btk,btke->bte', wts, jax.nn.one_hot(top_i, N_EXPERTS, dtype=jnp.float32)).astype(u.dtype)
    out = jnp.zeros_like(u)
    for e in range(N_EXPERTS):
        out = out + gates[..., e:e + 1] * _swiglu(u, w_in[e], w_out[e])
    return out


def setup_inputs(seed: int = 0) -> dict:
    key = jax.random.key(seed)
    keys = iter(jax.random.split(key, 64))
    f32 = jnp.float32
    D = D_MODEL

    def nrm(shape, s):
        return jax.random.normal(next(keys), shape, f32) * s

    def gain(shape):
        return 1.0 + nrm(shape, 0.05)

    na, nb, npool, nattn = [len(range(k, DEPTH, N_MIXERS)) for k in range(N_MIXERS)]
    n_dense = len(range(0, DEPTH, 2))
    n_moe = len(range(1, DEPTH, 2))
    dt0 = jnp.exp(jax.random.uniform(next(keys), (na, 2, MB_HEADS), f32, math.log(1e-3), math.log(1e-1)))
    return {
        'x': nrm((BATCH, SEQ, D), 1.0),
        'c': nrm((BATCH, D), 1.0),
        'ctx': nrm((BATCH, CTX_LEN, D), 1.0),
        'c_ctx': nrm((D,), 1.0),
        'mod_w': nrm((DEPTH, D, 6 * D), 0.5 * D ** -0.5),
        'mod_b': nrm((DEPTH, 6 * D), 0.02),
        'norm1_g': gain((DEPTH, D)),
        'norm2_g': gain((DEPTH, D)),
        'final_g': gain((D,)),
        'mb_in_w': nrm((na, D, MB_IN), D ** -0.5),
        'mb_conv_w': nrm((na, MB_CONV, MB_XBC), MB_CONV ** -0.5),
        'mb_conv_b': nrm((na, MB_XBC), 0.02),
        'mb_dt_bias': dt0 + jnp.log(-jnp.expm1(-dt0)),
        'mb_a_log': jnp.log(jax.random.uniform(next(keys), (na, 2, MB_HEADS), f32, 1.0, 16.0)),
        'mb_d': gain((na, MB_HEADS)),
        'mb_norm_g': gain((na, MB_D_INNER)),
        'mb_out_w': nrm((na, MB_D_INNER, D), MB_D_INNER ** -0.5),
        'rw_mix': jax.random.uniform(next(keys), (nb, 6, D), f32),
        'rw_rkv_w': nrm((nb, 3, D, D), D ** -0.5),
        'rw_w0': nrm((nb, 2, D), 1.0),
        'rw_w1': nrm((nb, 2, D, RW_DECAY_LORA), D ** -0.5),
        'rw_w2': nrm((nb, 2, RW_DECAY_LORA, D), 0.5 * RW_DECAY_LORA ** -0.5),
        'rw_a0': nrm((nb, 2, D), 0.5),
        'rw_a1': nrm((nb, 2, D, RW_ICLR_LORA), D ** -0.5),
        'rw_a2': nrm((nb, 2, RW_ICLR_LORA, D), 0.5 * RW_ICLR_LORA ** -0.5),
        'rw_g1': nrm((nb, D, RW_GATE_LORA), D ** -0.5),
        'rw_g2': nrm((nb, RW_GATE_LORA, D), RW_GATE_LORA ** -0.5),
        'rw_k_k': 0.85 + nrm((nb, D), 0.05),
        'rw_k_a': gain((nb, D)),
        'rw_r_k': nrm((nb, RW_HEADS, RW_HEAD), 0.1),
        'rw_ln_g': gain((nb, D)),
        'rw_ln_b': nrm((nb, D), 0.02),
        'rw_out_w': nrm((nb, D, D), D ** -0.5),
        'pl_w': nrm((npool, len(POOL_WINDOWS), POOL_GROUP, POOL_GROUP), POOL_GROUP ** -0.5),
        'pl_scale': gain((npool, D)),
        'at_qkv_w': nrm((nattn, D, (AT_HEADS + 2 * AT_KV_HEADS) * AT_HEAD), D ** -0.5),
        'at_q_g': gain((nattn, AT_HEAD)),
        'at_k_g': gain((nattn, AT_HEAD)),
        'at_out_w': nrm((nattn, AT_HEADS * AT_HEAD, D), (AT_HEADS * AT_HEAD) ** -0.5),
        'ff_in_w': nrm((n_dense, D, 2 * FF_DENSE), D ** -0.5),
        'ff_out_w': nrm((n_dense, FF_DENSE, D), FF_DENSE ** -0.5),
        'moe_router_w': nrm((n_moe, D, N_EXPERTS), D ** -0.5),
        'moe_in_w': nrm((n_moe, N_EXPERTS, D, 2 * FF_EXPERT), D ** -0.5),
        'moe_out_w': nrm((n_moe, N_EXPERTS, FF_EXPERT, D), FF_EXPERT ** -0.5),
    }


def reference(x, c, ctx, c_ctx, mod_w, mod_b, norm1_g, norm2_g, final_g,
              mb_in_w, mb_conv_w, mb_conv_b, mb_dt_bias, mb_a_log, mb_d, mb_norm_g, mb_out_w,
              rw_mix, rw_rkv_w, rw_w0, rw_w1, rw_w2, rw_a0, rw_a1, rw_a2, rw_g1, rw_g2,
              rw_k_k, rw_k_a, rw_r_k, rw_ln_g, rw_ln_b, rw_out_w,
              pl_w, pl_scale, at_qkv_w, at_q_g, at_k_g, at_out_w,
              ff_in_w, ff_out_w, moe_router_w, moe_in_w, moe_out_w):
    b, L, _ = x.shape
    n_rows = L // GRID_W
    rows = jnp.repeat(jnp.arange(n_rows, dtype=jnp.int32), GRID_W)
    cols = jnp.arange(L, dtype=jnp.int32) % GRID_W
    n_ctx = ctx.shape[1]
    s_lat = jax.nn.silu(c)[:, None, :]
    s_ctx = jax.nn.silu(c_ctx)
    xc = ctx
    for i in range(DEPTH):
        last = i == DEPTH - 1
        kind = i % N_MIXERS
        j = i // N_MIXERS
        m = jnp.split(s_lat @ mod_w[i] + mod_b[i], 6, axis=-1)
        h = _rmsnorm(x, norm1_g[i]) * (1 + m[1]) + m[0]
        need_hc = (not last) or kind != 2
        mc, hc = None, None
        if need_hc:
            mc = jnp.split(s_ctx @ mod_w[i] + mod_b[i], 6, axis=-1)
            hc = _rmsnorm(xc, norm1_g[i]) * (1 + mc[1]) + mc[0]
        if kind == 0:
            y, yc = _mamba_mixer(h, hc, not last, mb_in_w[j], mb_conv_w[j], mb_conv_b[j], mb_dt_bias[j],
                                 mb_a_log[j], mb_d[j], mb_norm_g[j], mb_out_w[j])
        elif kind == 1:
            y, yc = _rwkv_mixer(h, hc, not last, rw_mix[j], rw_rkv_w[j], rw_w0[j], rw_w1[j], rw_w2[j],
                                rw_a0[j], rw_a1[j], rw_a2[j], rw_g1[j], rw_g2[j], rw_k_k[j], rw_k_a[j],
                                rw_r_k[j], rw_ln_g[j], rw_ln_b[j], rw_out_w[j])
        elif kind == 2:
            y = _pool_mixer(h, pl_w[j], pl_scale[j])
            yc = None if last else _pool_mixer(hc, pl_w[j], pl_scale[j])
        else:
            y, yc = _attn_mixer(h, hc, not last, rows, cols, at_qkv_w[j], at_q_g[j], at_k_g[j], at_out_w[j])
        x = x + m[2] * y
        h2 = _rmsnorm(x, norm2_g[i]) * (1 + m[4]) + m[3]
        if not last:
            xc = xc + mc[2] * yc
            h2c = _rmsnorm(xc, norm2_g[i]) * (1 + mc[4]) + mc[3]
            hcat = jnp.concatenate([h2c, h2], axis=1)
        else:
            hcat = h2
        if i % 2 == 0:
            f = _swiglu(hcat, ff_in_w[i // 2], ff_out_w[i // 2])
        else:
            f = _moe(hcat, moe_router_w[i // 2], moe_in_w[i // 2], moe_out_w[i // 2])
        if last:
            x = x + m[5] * f
        else:
            xc = xc + mc[5] * f[:, :n_ctx]
            x = x + m[5] * f[:, n_ctx:]
    return _rmsnorm(x, final_g)
```

```python
import functools
import math

import jax
import jax.numpy as jnp
from jax import lax
from jax.experimental import pallas as pl
from jax.experimental.pallas import tpu as pltpu

F32 = jnp.float32
BF16 = jnp.bfloat16

NORM_EPS = 1e-6
TM = 256
LANES = 128
HEAD = 64
VMEM_LIMIT = 48 * 1024 * 1024

MB_HEADS = 32
MB_GROUPS = 4
MB_STATE = 128
MB_CHUNK = 128
MB_CONV = 5
RW_CHUNK = 64
RW_DECAY_SCALE = 0.606531
RW_LN_EPS = 64e-5
POOL_WINDOWS = (2, 4, 8, 16)
POOL_HALO = 16
AT_HEADS = 16
AT_KV_HEADS = 4
ROPE_THETA = 10000.0
GRID_W = 64
N_EXPERTS = 8


def _params(sem, vmem=VMEM_LIMIT):
    return pltpu.CompilerParams(dimension_semantics=sem, vmem_limit_bytes=vmem)


def _bdot(a, b):
    return jnp.dot(a.astype(BF16), b.astype(BF16), preferred_element_type=F32)


def _bdot_nt(a, b):
    return lax.dot_general(a.astype(BF16), b.astype(BF16), (((1,), (1,)), ((), ())),
                           preferred_element_type=F32)


def _bdot_tn(a, b):
    return lax.dot_general(a.astype(BF16), b.astype(BF16), (((0,), (0,)), ((), ())),
                           preferred_element_type=F32)


def _split3(x):
    p0 = x.astype(BF16)
    r1 = x - p0.astype(F32)
    p1 = r1.astype(BF16)
    p2 = (r1 - p1.astype(F32)).astype(BF16)
    return p0, p1, p2


def _as01(mask):
    return jnp.where(mask, 1.0, 0.0).astype(BF16)


def _dot01_left(sel, x):
    sel = _as01(sel)
    p0, p1, p2 = _split3(x)
    return (jnp.dot(sel, p0, preferred_element_type=F32)
            + jnp.dot(sel, p1, preferred_element_type=F32)
            + jnp.dot(sel, p2, preferred_element_type=F32))


def _dot01_right(x, sel):
    sel = _as01(sel)
    p0, p1, p2 = _split3(x)
    return (jnp.dot(p0, sel, preferred_element_type=F32)
            + jnp.dot(p1, sel, preferred_element_type=F32)
            + jnp.dot(p2, sel, preferred_element_type=F32))


def _head_sum(x):
    r = lax.broadcasted_iota(jnp.int32, (LANES, LANES), 0) // HEAD
    c = lax.broadcasted_iota(jnp.int32, (LANES, LANES), 1) // HEAD
    ones_bd = r == c
    slabs = [_dot01_right(x[:, s:s + LANES], ones_bd) for s in range(0, x.shape[1], LANES)]
    return slabs[0] if len(slabs) == 1 else jnp.concatenate(slabs, axis=1)


def _sigmoid(x):
    return 1.0 / (1.0 + jnp.exp(-x))


def _silu(x):
    return x * _sigmoid(x)


def _rms(x, g):
    ms = jnp.mean(x * x, axis=-1, keepdims=True)
    return x * lax.rsqrt(ms + NORM_EPS) * g


def _mods_body(s_ref, w_ref, b_ref, o_ref):
    o_ref[...] = _bdot(_silu(s_ref[...]), w_ref[...]) + b_ref[...]


def _mods(svec, mod_w, mod_b):
    depth, d, n6 = mod_w.shape
    tn = 1536
    return pl.pallas_call(
        _mods_body,
        out_shape=jax.ShapeDtypeStruct((depth, 8, n6), F32),
        grid=(depth, n6 // tn),
        in_specs=[pl.BlockSpec((8, d), lambda l, j: (0, 0)),
                  pl.BlockSpec((None, d, tn), lambda l, j: (l, 0, j)),
                  pl.BlockSpec((None, 1, tn), lambda l, j: (l, 0, j))],
        out_specs=pl.BlockSpec((None, 8, tn), lambda l, j: (l, 0, j)),
        compiler_params=_params(("parallel", "parallel")),
        name="mods",
    )(svec, mod_w, mod_b.reshape(depth, 1, n6))


def _stream(width, off=0, rows=TM):
    return pl.BlockSpec((None, rows, width), lambda b, i: (b, i + off, 0))


def _stream2(width, d, off=0):
    return pl.BlockSpec((None, None, TM, width), lambda b, i: (d, b, i + off, 0))


def _vec(width, rows=1):
    return pl.BlockSpec((rows, width), lambda b, i: (0, 0))


def _mv(layer, nct, nb, d):
    return pl.BlockSpec((None, None, 8, d), lambda b, i: (layer, jnp.where(i < nct, nb, b), 0, 0))


def _halo(width, nrows, t_total, side):
    per = TM // nrows
    last = t_total // nrows - 1
    if side < 0:
        return pl.BlockSpec((None, nrows, width), lambda b, i: (b, jnp.maximum(i * per - 1, 0), 0))
    return pl.BlockSpec((None, nrows, width), lambda b, i: (b, jnp.minimum((i + 1) * per, last), 0))


def _seq_edges(i, nt, nct):
    first = jnp.logical_or(i == 0, i == nct)
    last = jnp.logical_or(i == nct - 1, i == nt - 1)
    return first, last


def _shift_rows(x, prev8, next8, o):
    rows = x.shape[0]
    r8 = lax.broadcasted_iota(jnp.int32, (8, x.shape[1]), 0)
    if o < 0:
        k = -o
        s = pltpu.roll(x, k, axis=0)
        top = jnp.where(r8 < k, pltpu.roll(prev8, k, axis=0), s[0:8])
        return jnp.concatenate([top, s[8:]], axis=0)
    s = pltpu.roll(x, rows - o, axis=0)
    bot = jnp.where(r8 >= 8 - o, pltpu.roll(next8, 8 - o, axis=0), s[rows - 8:])
    return jnp.concatenate([s[:rows - 8], bot], axis=0)


def _norm_mod_body(x_ref, g_ref, mv_ref, h_ref, *, js, jc):
    h = _rms(x_ref[...], g_ref[...]) * (1.0 + mv_ref[jc:jc + 1, :]) + mv_ref[js:js + 1, :]
    h_ref[...] = h.astype(h_ref.dtype)


def _norm_mod(x, g, mv, layer, nct, out_dtype):
    b, t, d = x.shape
    nb = b
    return pl.pallas_call(
        functools.partial(_norm_mod_body, js=0, jc=1),
        out_shape=jax.ShapeDtypeStruct((b, t, d), out_dtype),
        grid=(b, t // TM),
        in_specs=[_stream(d), _vec(d), _mv(layer, nct, nb, d)],
        out_specs=_stream(d),
        compiler_params=_params(("parallel", "parallel")),
        name="norm_mod",
    )(x, g.reshape(1, d), mv)


def _resid_norm_body(x_ref, y_ref, mvg_ref, g_ref, mvm_ref, xo_ref, h_ref, *, jg, js, jc):
    xn = x_ref[...] + mvg_ref[jg:jg + 1, :] * y_ref[...].astype(F32)
    xo_ref[...] = xn
    h = _rms(xn, g_ref[...]) * (1.0 + mvm_ref[jc:jc + 1, :]) + mvm_ref[js:js + 1, :]
    h_ref[...] = h.astype(h_ref.dtype)


def _resid_norm(x, y, mv, g, *, gate_layer, jg, mod_layer, js, jc, nct, x_off, out_dtype):
    b, t, d = y.shape
    return pl.pallas_call(
        functools.partial(_resid_norm_body, jg=jg, js=js, jc=jc),
        out_shape=(jax.ShapeDtypeStruct((b, t, d), F32), jax.ShapeDtypeStruct((b, t, d), out_dtype)),
        grid=(b, t // TM),
        in_specs=[_stream(d, x_off), _stream(d), _mv(gate_layer, nct, b, d), _vec(d),
                  _mv(mod_layer, nct, b, d)],
        out_specs=(_stream(d), _stream(d)),
        compiler_params=_params(("parallel", "parallel")),
        name="resid_norm",
    )(x, y, mv, g.reshape(1, d), mv)


def _top2_gates(logits):
    lane = lax.broadcasted_iota(jnp.int32, logits.shape, 1)
    neg = jnp.float32(-jnp.inf)
    lg = jnp.where(lane < N_EXPERTS, logits, neg)
    v1 = jnp.max(lg, axis=-1, keepdims=True)
    i1 = jnp.min(jnp.where(lg == v1, lane, LANES), axis=-1, keepdims=True)
    lg2 = jnp.where(lane == i1, neg, lg)
    v2 = jnp.max(lg2, axis=-1, keepdims=True)
    i2 = jnp.min(jnp.where(lg2 == v2, lane, LANES), axis=-1, keepdims=True)
    e = jnp.exp(v2 - v1)
    w1 = 1.0 / (1.0 + e)
    w2 = e / (1.0 + e)
    return jnp.where(lane == i1, w1, 0.0) + jnp.where(lane == i2, w2, 0.0)


def _resid_norm_router_body(x_ref, y_ref, mvg_ref, g_ref, mvm_ref, rw_ref, xo_ref, h_ref, gates_ref,
                            *, jg, js, jc):
    xn = x_ref[...] + mvg_ref[jg:jg + 1, :] * y_ref[...].astype(F32)
    xo_ref[...] = xn
    h = _rms(xn, g_ref[...]) * (1.0 + mvm_ref[jc:jc + 1, :]) + mvm_ref[js:js + 1, :]
    h_ref[...] = h.astype(h_ref.dtype)
    h0, h1, h2 = _split3(h)
    w0, w1, w2 = _split3(rw_ref[...])
    dot = lambda a, c: jnp.dot(a, c, preferred_element_type=F32)
    logits = (dot(h0, w0) + (dot(h0, w1) + dot(h1, w0))
              + (dot(h0, w2) + dot(h1, w1) + dot(h2, w0)))
    gates_ref[...] = _top2_gates(logits)


def _resid_norm_router(x, y, mv, g, router_w, *, gate_layer, jg, mod_layer, js, jc, nct, x_off):
    b, t, d = y.shape
    rw = jnp.pad(router_w, ((0, 0), (0, LANES - router_w.shape[1])))
    return pl.pallas_call(
        functools.partial(_resid_norm_router_body, jg=jg, js=js, jc=jc),
        out_shape=(jax.ShapeDtypeStruct((b, t, d), F32), jax.ShapeDtypeStruct((b, t, d), BF16),
                   jax.ShapeDtypeStruct((b, t, LANES), F32)),
        grid=(b, t // TM),
        in_specs=[_stream(d, x_off), _stream(d), _mv(gate_layer, nct, b, d), _vec(d),
                  _mv(mod_layer, nct, b, d), _vec(LANES, d)],
        out_specs=(_stream(d), _stream(d), _stream(LANES)),
        compiler_params=_params(("parallel", "parallel")),
        name="resid_norm_router",
    )(x, y, mv, g.reshape(1, d), mv, rw)


def _resid_final_body(x_ref, y_ref, mvg_ref, g_ref, o_ref, *, jg):
    xn = x_ref[...] + mvg_ref[jg:jg + 1, :] * y_ref[...].astype(F32)
    o_ref[...] = _rms(xn, g_ref[...])


def _resid_final(x, y, mv, g, *, gate_layer, jg):
    b, t, d = y.shape
    return pl.pallas_call(
        functools.partial(_resid_final_body, jg=jg),
        out_shape=jax.ShapeDtypeStruct((b, t, d), F32),
        grid=(b, t // TM),
        in_specs=[_stream(d), _stream(d), _mv(gate_layer, 0, b, d), _vec(d)],
        out_specs=_stream(d),
        compiler_params=_params(("parallel", "parallel")),
        name="resid_final",
    )(x, y, mv, g.reshape(1, d))


def _mm_body(x_ref, w_ref, o_ref):
    o_ref[...] = _bdot(x_ref[...], w_ref[...]).astype(o_ref.dtype)


def _pick(n, prefs):
    for p in prefs:
        if n % p == 0:
            return p
    raise ValueError(f"no tile for {n}")


def _matmul(x, w, *, n_cols=None, out_dtype=F32, tm=None, tn=None):
    m, k = x.shape
    n = n_cols or w.shape[1]
    tm = tm or _pick(m, (1024, 768, 512, 256))
    tn = tn or _pick(n, (1024, 768, 512, 256, 128))
    return pl.pallas_call(
        _mm_body,
        out_shape=jax.ShapeDtypeStruct((m, n), out_dtype),
        grid=(n // tn, m // tm),
        in_specs=[pl.BlockSpec((tm, k), lambda j, i: (i, 0)),
                  pl.BlockSpec((k, tn), lambda j, i: (0, j))],
        out_specs=pl.BlockSpec((tm, tn), lambda j, i: (i, j)),
        compiler_params=_params(("parallel", "parallel")),
        name="matmul",
    )(x, w)


def _swiglu_body(x_ref, wg_ref, wu_ref, o_ref):
    x = x_ref[...]
    o_ref[...] = (_silu(_bdot(x, wg_ref[...])) * _bdot(x, wu_ref[...])).astype(o_ref.dtype)


def _swiglu_in(x, w_in, *, tm=None, tn=None):
    m, k = x.shape
    f = w_in.shape[1] // 2
    tm = tm or _pick(m, (768, 512, 256))
    tn = tn or _pick(f, (1408, 896, 512, 256, 128))
    nf = f // tn
    return pl.pallas_call(
        _swiglu_body,
        out_shape=jax.ShapeDtypeStruct((m, f), BF16),
        grid=(nf, m // tm),
        in_specs=[pl.BlockSpec((tm, k), lambda j, i: (i, 0)),
                  pl.BlockSpec((k, tn), lambda j, i: (0, j)),
                  pl.BlockSpec((k, tn), lambda j, i: (0, j + nf))],
        out_specs=pl.BlockSpec((tm, tn), lambda j, i: (i, j)),
        compiler_params=_params(("parallel", "parallel")),
        name="swiglu_in",
    )(x, w_in, w_in)


def _lora_body(x_ref, a_ref, b_ref, bias_ref, o_ref, *, act):
    t = _bdot(x_ref[...], a_ref[...])
    if act == "tanh":
        t = jnp.tanh(t)
    elif act == "sigmoid":
        t = _sigmoid(t)
    o_ref[...] = _bdot(t, b_ref[...]) + bias_ref[...]


def _lora(x, a, bm, bias, act):
    m, k = x.shape
    r = a.shape[1]
    n = bm.shape[1]
    tm = _pick(m, (512, 256))
    return pl.pallas_call(
        functools.partial(_lora_body, act=act),
        out_shape=jax.ShapeDtypeStruct((m, n), F32),
        grid=(m // tm,),
        in_specs=[pl.BlockSpec((tm, k), lambda i: (i, 0)),
                  pl.BlockSpec((k, r), lambda i: (0, 0)),
                  pl.BlockSpec((r, n), lambda i: (0, 0)),
                  pl.BlockSpec((1, n), lambda i: (0, 0))],
        out_specs=pl.BlockSpec((tm, n), lambda i: (i, 0)),
        compiler_params=_params(("parallel",)),
        name="lora",
    )(x, a, bm, bias)


def _mb_conv_body(x_ref, xp_ref, xn_ref, w_ref, b_ref, o_ref, *, nt, nct):
    i = pl.program_id(1)
    first, last = _seq_edges(i, nt, nct)
    x = x_ref[...]
    prev8 = jnp.where(first, 0.0, xp_ref[...])
    next8 = jnp.where(last, 0.0, xn_ref[...])
    pad = (MB_CONV - 1) // 2
    acc = x * w_ref[pad:pad + 1, :] + b_ref[...]
    for o in range(-pad, pad + 1):
        if o != 0:
            acc = acc + _shift_rows(x, prev8, next8, o) * w_ref[pad + o:pad + o + 1, :]
    o_ref[...] = _silu(acc)


def _mb_conv(zx, conv_w, conv_b, *, col0, nct):
    b, t, _ = zx.shape
    c = conv_w.shape[1]
    tc = 512
    cb = col0 // tc
    nt = t // TM
    per = TM // 8
    lastb = t // 8 - 1
    w8 = jnp.pad(conv_w, ((0, 8 - conv_w.shape[0]), (0, 0)))
    return pl.pallas_call(
        functools.partial(_mb_conv_body, nt=nt, nct=nct),
        out_shape=jax.ShapeDtypeStruct((b, t, c), F32),
        grid=(b, nt, c // tc),
        in_specs=[pl.BlockSpec((None, TM, tc), lambda b_, i, j: (b_, i, cb + j)),
                  pl.BlockSpec((None, 8, tc), lambda b_, i, j: (b_, jnp.maximum(i * per - 1, 0), cb + j)),
                  pl.BlockSpec((None, 8, tc),
                               lambda b_, i, j: (b_, jnp.minimum((i + 1) * per, lastb), cb + j)),
                  pl.BlockSpec((8, tc), lambda b_, i, j: (0, j)),
                  pl.BlockSpec((1, tc), lambda b_, i, j: (0, j))],
        out_specs=pl.BlockSpec((None, TM, tc), lambda b_, i, j: (b_, i, j)),
        compiler_params=_params(("parallel", "parallel", "parallel")),
        name="mb_conv",
    )(zx, zx, zx, w8, conv_b.reshape(1, c))


def _softplus(x):
    return jnp.maximum(x, 0.0) + jnp.log(1.0 + jnp.exp(-jnp.abs(x)))


def _lane_bcast(col, width):
    return jnp.broadcast_to(col, (col.shape[0], width))


def _ssd_body(x_ref, b_ref, c_ref, dtc_ref, dtr_ref, bc_ref, br_ref, ac_ref, ar_ref, y_ref, st_ref):
    d = pl.program_id(1)
    q = MB_CHUNK
    hpg = MB_HEADS // MB_GROUPS

    @pl.when(pl.program_id(2) == 0)
    def _():
        st_ref[...] = jnp.zeros_like(st_ref)

    sgn = 1 - 2 * d
    dt_c = _softplus(dtc_ref[...] + bc_ref[...])
    dt_r = _softplus(dtr_ref[...] + br_ref[...])
    dta_c = dt_c * (-jnp.exp(ac_ref[...]))
    dta_r = dt_r * (-jnp.exp(ar_ref[...]))
    ii = lax.broadcasted_iota(jnp.int32, (q, q), 0)
    jj = lax.broadcasted_iota(jnp.int32, (q, q), 1)
    mask = (ii - jj) * sgn >= 0
    cum_c = _dot01_left(mask, dta_c)
    cum_r = _dot01_right(dta_r, (jj - ii) * sgn >= 0)
    tot_c = jnp.sum(dta_c, axis=0, keepdims=True)
    ecum_c = jnp.exp(cum_c)
    f_c = jnp.exp(tot_c - cum_c) * dt_c
    etot_c = jnp.exp(tot_c)
    lane = lax.broadcasted_iota(jnp.int32, (q, LANES), 1)
    lo = lane < HEAD
    lane1 = lax.broadcasted_iota(jnp.int32, (1, LANES), 1)

    for g in range(MB_GROUPS):
        bg = b_ref[:, g * MB_STATE:(g + 1) * MB_STATE].astype(BF16)
        cg = c_ref[:, g * MB_STATE:(g + 1) * MB_STATE].astype(BF16)
        gmat = _bdot_nt(cg, bg)
        for pq in range(hpg // 2):
            p = g * (hpg // 2) + pq
            h0 = 2 * p
            xp = x_ref[:, p * LANES:(p + 1) * LANES]
            xpb = xp.astype(BF16)
            ys = []
            for h in (h0, h0 + 1):
                seg = _lane_bcast(cum_c[:, h:h + 1], q) - cum_r[h:h + 1, :]
                wmat = gmat * (jnp.exp(jnp.where(mask, seg, -jnp.inf)) * dt_r[h:h + 1, :])
                ys.append(_bdot(wmat, xpb))
            y_intra = jnp.where(lo, ys[0], ys[1])
            pair = lambda a: jnp.where(lo, _lane_bcast(a[:, h0:h0 + 1], LANES),
                                       _lane_bcast(a[:, h0 + 1:h0 + 2], LANES))
            st = st_ref[p]
            y_ref[:, p * LANES:(p + 1) * LANES] = y_intra + _bdot(cg, st) * pair(ecum_c)
            upd = _bdot_tn(bg, xp * pair(f_c))
            et = jnp.where(lane1 < HEAD, _lane_bcast(etot_c[:, h0:h0 + 1], LANES),
                           _lane_bcast(etot_c[:, h0 + 1:h0 + 2], LANES))
            st_ref[p] = st * et + upd


def _scan_chunk(d, c, n_ctx, n_all):
    rev = jnp.where(c < n_ctx, n_ctx - 1 - c, n_all - 1 - (c - n_ctx))
    return jnp.where(d == 0, c, rev)


def _ssd(xa, dtc, dtr, dt_bias, a_log, *, n_ctx_rows):
    b, t, _ = xa.shape
    q = MB_CHUNK
    nh = MB_HEADS
    di = nh * HEAD
    gn = MB_GROUPS * MB_STATE
    nc = t // q
    ncc = n_ctx_rows // q
    tc = functools.partial(_scan_chunk, n_ctx=ncc, n_all=nc)
    bias_c = dt_bias.reshape(2, 1, nh)
    bias_r = dt_bias.reshape(2, nh, 1)
    a_c = a_log.reshape(2, 1, nh)
    a_r = a_log.reshape(2, nh, 1)
    small_c = pl.BlockSpec((None, 1, nh), lambda b_, d, c: (d, 0, 0))
    small_r = pl.BlockSpec((None, nh, 1), lambda b_, d, c: (d, 0, 0))
    return pl.pallas_call(
        _ssd_body,
        out_shape=jax.ShapeDtypeStruct((2, b, t, di), F32),
        grid=(b, 2, nc),
        in_specs=[pl.BlockSpec((None, q, di), lambda b_, d, c: (b_, tc(d, c), 0)),
                  pl.BlockSpec((None, q, gn), lambda b_, d, c: (b_, tc(d, c), di // gn)),
                  pl.BlockSpec((None, q, gn), lambda b_, d, c: (b_, tc(d, c), di // gn + 1)),
                  pl.BlockSpec((None, None, q, nh), lambda b_, d, c: (d, b_, tc(d, c), 0)),
                  pl.BlockSpec((None, None, nh, q), lambda b_, d, c: (d, b_, 0, tc(d, c))),
                  small_c, small_r, small_c, small_r],
        out_specs=pl.BlockSpec((None, None, q, di), lambda b_, d, c: (d, b_, tc(d, c), 0)),
        scratch_shapes=[pltpu.VMEM((nh // 2, MB_STATE, LANES), F32)],
        compiler_params=_params(("parallel", "parallel", "arbitrary")),
        name="ssd_scan",
    )(xa, xa, xa, dtc, dtr, bias_c, bias_r, a_c, a_r)


def _mb_gate_body(y0_ref, y1_ref, xs_ref, z_ref, dv_ref, g_ref, o_ref):
    y = y0_ref[...] + y1_ref[...] + xs_ref[...] * dv_ref[...]
    o_ref[...] = _rms(y * _silu(z_ref[...]), g_ref[...]).astype(o_ref.dtype)


def _mb_gate(y, xa, zx, dvec, norm_g):
    _, b, t, di = y.shape
    return pl.pallas_call(
        _mb_gate_body,
        out_shape=jax.ShapeDtypeStruct((b, t, di), BF16),
        grid=(b, t // TM),
        in_specs=[_stream2(di, 0), _stream2(di, 1), _stream(di), _stream(di), _vec(di), _vec(di)],
        out_specs=_stream(di),
        compiler_params=_params(("parallel", "parallel")),
        name="mb_gate",
    )(y, y, xa, zx, dvec, norm_g.reshape(1, di))


def _mamba_layer(h, w, *, n_ctx_rows):
    b, t, d = h.shape
    m = b * t
    di = MB_HEADS * HEAD
    xbc = di + 2 * MB_GROUPS * MB_STATE
    h2 = h.reshape(m, d)
    zx = _matmul(h2, w["in_w"], n_cols=di + xbc).reshape(b, t, di + xbc)
    dt_raw = _matmul(h2, w["in_w_dt"], tn=LANES).reshape(b, t, LANES)[..., :2 * MB_HEADS]
    dtc = jnp.moveaxis(dt_raw.reshape(b, t, 2, MB_HEADS), 2, 0)
    dtr = jnp.swapaxes(dtc, 2, 3)
    xa = _mb_conv(zx, w["conv_w"], w["conv_b"], col0=di, nct=n_ctx_rows // TM)
    y = _ssd(xa, dtc, dtr, w["dt_bias"], w["a_log"], n_ctx_rows=n_ctx_rows)
    gated = _mb_gate(y, xa, zx, w["dvec"], w["norm_g"])
    return _matmul(gated.reshape(m, di), w["out_w"]).reshape(b, t, d)


def _rw_mix_body(h_ref, hp_ref, hn_ref, mix_ref, *o_refs, nt, nct):
    i = pl.program_id(1)
    first, last = _seq_edges(i, nt, nct)
    h = h_ref[...]
    prow = jnp.where(first, 0.0, hp_ref[7:8, :])
    nrow = jnp.where(last, 0.0, hn_ref[0:1, :])
    row = lax.broadcasted_iota(jnp.int32, h.shape, 0)
    prev = jnp.where(row == 0, prow, pltpu.roll(h, 1, axis=0))
    nxt = jnp.where(row == TM - 1, nrow, pltpu.roll(h, TM - 1, axis=0))
    xx = 0.5 * (prev + nxt) - h
    for j, o_ref in enumerate(o_refs):
        o_ref[...] = (h + xx * mix_ref[j:j + 1, :]).astype(o_ref.dtype)


def _rw_mix(h, mix, *, nct):
    b, t, d = h.shape
    nt = t // TM
    mix8 = jnp.pad(mix, ((0, 8 - mix.shape[0]), (0, 0)))
    return pl.pallas_call(
        functools.partial(_rw_mix_body, nt=nt, nct=nct),
        out_shape=tuple(jax.ShapeDtypeStruct((b, t, d), BF16) for _ in range(6)),
        grid=(b, nt),
        in_specs=[_stream(d), _halo(d, 8, t, -1), _halo(d, 8, t, +1), _vec(d, 8)],
        out_specs=tuple(_stream(d) for _ in range(6)),
        compiler_params=_params(("parallel", "parallel")),
        name="rw_mix",
    )(h, h, h, mix8)


def _rw_prep_body(k_ref, wl_ref, al_ref, kk_k_ref, k_a_ref, kk_ref, lw_ref, kd_ref, bb_ref):
    d = k_ref.shape[1]
    k = k_ref[...]
    kkr = k * kk_k_ref[...]
    kk = kkr / jnp.maximum(jnp.sqrt(_head_sum(kkr * kkr)), 1e-12)
    kk_ref[...] = kk
    for dr in range(2):
        lw_ref[dr] = -RW_DECAY_SCALE * _sigmoid(wl_ref[:, dr * d:(dr + 1) * d])
        a = _sigmoid(al_ref[:, dr * d:(dr + 1) * d])
        kd_ref[dr] = k * (1.0 + (a - 1.0) * k_a_ref[...])
        bb_ref[dr] = kk * a


def _rw_prep(k, wl, al, k_k, k_a):
    b, t, d = k.shape
    two = pl.BlockSpec((2, None, TM, d), lambda b_, i: (0, b_, i, 0))
    s2 = jax.ShapeDtypeStruct((2, b, t, d), F32)
    return pl.pallas_call(
        _rw_prep_body,
        out_shape=(jax.ShapeDtypeStruct((b, t, d), F32), s2, s2, s2),
        grid=(b, t // TM),
        in_specs=[_stream(d), _stream(2 * d), _stream(2 * d), _vec(d), _vec(d)],
        out_specs=(_stream(d), two, two, two),
        compiler_params=_params(("parallel", "parallel")),
        name="rw_prep",
    )(k, wl, al, k_k.reshape(1, d), k_a.reshape(1, d))


def _rw_scan_body(r_ref, v_ref, kk_ref, lw_ref, kd_ref, bb_ref, y_ref, s_ref):
    dr = pl.program_id(1)
    c = RW_CHUNK
    c2 = 2 * c

    @pl.when(pl.program_id(2) == 0)
    def _():
        s_ref[...] = jnp.zeros_like(s_ref)

    sgn = 1 - 2 * dr
    ti = lax.broadcasted_iota(jnp.int32, (c, c), 0)
    si = lax.broadcasted_iota(jnp.int32, (c, c), 1)
    lw = lw_ref[...]
    cum = _dot01_left((ti - si) * sgn >= 0, lw)
    eg = jnp.exp(cum)
    egp = jnp.exp(cum - lw)
    einv = jnp.exp(-cum)
    kk = kk_ref[...]
    at_all = -kk * egp
    rt_all = r_ref[...] * eg
    kt_all = kd_ref[...] * einv
    bt_all = bb_ref[...] * einv
    v_all = v_ref[...]
    g_end = jnp.where(dr == 0, eg[c - 1:c, :], eg[0:1, :])

    ri = lax.broadcasted_iota(jnp.int32, (c2, c2), 0)
    ci = lax.broadcasted_iota(jnp.int32, (c2, c2), 1)
    same = (ri >= c) == (ci >= c)
    dtm = ((ri & (c - 1)) - (ci & (c - 1))) * sgn
    m_strict = jnp.logical_and(same, dtm > 0)
    m_incl = jnp.logical_and(same, dtm >= 0)
    eye = jnp.where(ri == ci, 1.0, 0.0)
    lane = lax.broadcasted_iota(jnp.int32, (c, LANES), 1)
    lo = lane < HEAD
    rl = lax.broadcasted_iota(jnp.int32, (LANES, LANES), 0)
    cl = lax.broadcasted_iota(jnp.int32, (LANES, LANES), 1)
    bd = (rl >= HEAD) == (cl >= HEAD)

    def stack(x):
        return jnp.concatenate([jnp.where(lo, x, 0.0), jnp.where(lo, 0.0, x)], axis=0)

    def unstack(x2):
        return x2[0:c] + x2[c:c2]

    for p in range(s_ref.shape[0]):
        sl = slice(p * LANES, (p + 1) * LANES)
        at, rt, kt, bt, v = at_all[:, sl], rt_all[:, sl], kt_all[:, sl], bt_all[:, sl], v_all[:, sl]
        lhs = jnp.concatenate([stack(at), stack(rt)], axis=0).astype(BF16)
        rhs = jnp.concatenate([bt, bt, kt, kt], axis=0).astype(BF16)
        nn = _bdot_nt(lhs, rhs)
        n_ab = jnp.where(m_strict, nn[0:c2, 0:c2], 0.0)
        n_ak = jnp.where(m_strict, nn[0:c2, c2:2 * c2], 0.0)
        n_qb = jnp.where(m_incl, nn[c2:2 * c2, 0:c2], 0.0)
        n_qk = jnp.where(m_incl, nn[c2:2 * c2, c2:2 * c2], 0.0)
        s = s_ref[p]
        a_s = _bdot_nt(jnp.concatenate([at, rt], axis=0), s)
        v2 = stack(v)
        rhs2 = stack(a_s[0:c]) + _bdot(n_ak, v2)
        tinv = eye + n_ab
        pw = n_ab
        for _ in range(int(math.log2(c)) - 1):
            pw = _bdot(pw, pw)
            tinv = tinv + _bdot(pw, tinv)
        u2 = _bdot(tinv, rhs2)
        y2 = _bdot(jnp.concatenate([n_qk, n_qb], axis=1), jnp.concatenate([v2, u2], axis=0))
        y_ref[:, sl] = a_s[c:c2] + unstack(y2)
        u = unstack(u2)
        upd = _bdot_tn(jnp.concatenate([v, u], axis=0), jnp.concatenate([kt, bt], axis=0))
        s_ref[p] = (s + jnp.where(bd, upd, 0.0)) * g_end[:, sl]


def _rw_scan(r, v, kk, lw, kd, bb, *, n_ctx_rows):
    b, t, d = r.shape
    c = RW_CHUNK
    nc = t // c
    tc = functools.partial(_scan_chunk, n_ctx=n_ctx_rows // c, n_all=nc)
    one = pl.BlockSpec((None, c, d), lambda b_, dr, ch: (b_, tc(dr, ch), 0))
    two = pl.BlockSpec((None, None, c, d), lambda b_, dr, ch: (dr, b_, tc(dr, ch), 0))
    return pl.pallas_call(
        _rw_scan_body,
        out_shape=jax.ShapeDtypeStruct((2, b, t, d), F32),
        grid=(b, 2, nc),
        in_specs=[one, one, one, two, two, two],
        out_specs=two,
        scratch_shapes=[pltpu.VMEM((d // LANES, LANES, LANES), F32)],
        compiler_params=_params(("parallel", "parallel", "arbitrary")),
        name="rw_scan",
    )(r, v, kk, lw, kd, bb)


def _rw_out_body(y0_ref, y1_ref, r_ref, v_ref, kd0_ref, kd1_ref, g_ref, rk_ref, lg_ref, lb_ref, o_ref):
    y = y0_ref[...] + y1_ref[...]
    mu = _head_sum(y) * (1.0 / HEAD)
    yc = y - mu
    var = _head_sum(yc * yc) * (1.0 / HEAD)
    yn = yc * lax.rsqrt(var + RW_LN_EPS) * lg_ref[...] + lb_ref[...]
    bonus = _head_sum(r_ref[...] * rk_ref[...] * (0.5 * (kd0_ref[...] + kd1_ref[...]))) * v_ref[...]
    o_ref[...] = ((yn + bonus) * g_ref[...]).astype(o_ref.dtype)


def _rw_out(y, r, v, kd, g, r_k, ln_g, ln_b):
    b, t, d = r.shape
    return pl.pallas_call(
        _rw_out_body,
        out_shape=jax.ShapeDtypeStruct((b, t, d), BF16),
        grid=(b, t // TM),
        in_specs=[_stream2(d, 0), _stream2(d, 1), _stream(d), _stream(d), _stream2(d, 0), _stream2(d, 1),
                  _stream(d), _vec(d), _vec(d), _vec(d)],
        out_specs=_stream(d),
        compiler_params=_params(("parallel", "parallel")),
        name="rw_out",
    )(y, y, r, v, kd, kd, g, r_k.reshape(1, d), ln_g.reshape(1, d), ln_b.reshape(1, d))


def _rwkv_layer(h, w, *, n_ctx_rows):
    b, t, d = h.shape
    m = b * t
    xr, xw, xk, xv, xa, xg = [a.reshape(m, d) for a in _rw_mix(h, w["mix"], nct=n_ctx_rows // TM)]
    r = _matmul(xr, w["r_w"]).reshape(b, t, d)
    k = _matmul(xk, w["k_w"]).reshape(b, t, d)
    v = _matmul(xv, w["v_w"]).reshape(b, t, d)
    g = _lora(xg, w["g1"], w["g2"], jnp.zeros((1, d), F32), "sigmoid").reshape(b, t, d)
    wl = _lora(xw, w["w1"], w["w2"], w["w0"], "tanh").reshape(b, t, 2 * d)
    al = _lora(xa, w["a1"], w["a2"], w["a0"], "none").reshape(b, t, 2 * d)
    kk, lw, kd, bb = _rw_prep(k, wl, al, w["k_k"], w["k_a"])
    y = _rw_scan(r, v, kk, lw, kd, bb, n_ctx_rows=n_ctx_rows)
    o = _rw_out(y, r, v, kd, g, w["r_k"], w["ln_g"], w["ln_b"])
    return _matmul(o.reshape(m, d), w["out_w"]).reshape(b, t, d)


def _pool_body(h_ref, hp_ref, hn_ref, w_ref, sc_ref, o_ref, *, nt, nct):
    i = pl.program_id(1)
    first, last = _seq_edges(i, nt, nct)
    h = h_ref[...]
    halo = POOL_HALO
    ext = jnp.concatenate([jnp.where(first, 0.0, hp_ref[...]), h, jnp.where(last, 0.0, hn_ref[...])], axis=0)
    seq_start = jnp.where(i < nct, 0, nct)
    seq_len = jnp.where(i < nct, nct, nt - nct) * TM
    gw = h.shape[1] // len(POOL_WINDOWS)
    pos = (i - seq_start) * TM + lax.broadcasted_iota(jnp.int32, (TM, gw), 0)
    tr = lax.broadcasted_iota(jnp.int32, (TM, TM + 2 * halo), 0)
    er = lax.broadcasted_iota(jnp.int32, (TM, TM + 2 * halo), 1)
    for gi, win in enumerate(POOL_WINDOWS):
        half = win // 2
        band = jnp.logical_and(er >= tr + halo - half, er < tr + halo + half)
        cols = slice(gi * gw, (gi + 1) * gw)
        wsum = _dot01_left(band, ext[:, cols])
        cnt = (jnp.minimum(pos + half, seq_len) - jnp.maximum(pos - half, 0)).astype(F32)
        pooled = wsum / cnt - h[:, cols]
        o_ref[:, cols] = _bdot(pooled, w_ref[gi]) * sc_ref[:, cols]


def _pool_layer(h, pl_w, scale, *, nct):
    b, t, d = h.shape
    nt = t // TM
    ng, gw, _ = pl_w.shape
    return pl.pallas_call(
        functools.partial(_pool_body, nt=nt, nct=nct),
        out_shape=jax.ShapeDtypeStruct((b, t, d), F32),
        grid=(b, nt),
        in_specs=[_stream(d), _halo(d, POOL_HALO, t, -1), _halo(d, POOL_HALO, t, +1),
                  pl.BlockSpec((ng, gw, gw), lambda b_, i: (0, 0, 0)), _vec(d)],
        out_specs=_stream(d),
        compiler_params=_params(("parallel", "parallel")),
        name="pool_mixer",
    )(h, h, h, pl_w, scale.reshape(1, d))


def _at_prep_body(qkv_ref, qg_ref, kg_ref, cos_ref, sin_ref, q_ref, k_ref, v_ref, *, scale):
    nq = q_ref.shape[1]
    nk = AT_KV_HEADS * HEAD
    cos = cos_ref[...]
    sin = sin_ref[...]
    lane = lax.broadcasted_iota(jnp.int32, cos.shape, 1)
    up = (lane & 31) < 16
    lo = lane < HEAD

    def norm_rope(x, g):
        xn = x * lax.rsqrt(_head_sum(x * x) * (1.0 / HEAD) + NORM_EPS) * g
        swapped = jnp.where(up, pltpu.roll(xn, LANES - 16, axis=1), pltpu.roll(xn, 16, axis=1))
        return xn * cos + swapped * sin

    for s in range(nq // LANES):
        sl = slice(s * LANES, (s + 1) * LANES)
        q_ref[:, sl] = (norm_rope(qkv_ref[:, sl], qg_ref[...]) * scale).astype(q_ref.dtype)
    for s in range(nk // LANES):
        kslab = norm_rope(qkv_ref[:, nq + s * LANES:nq + (s + 1) * LANES], kg_ref[...])
        vslab = qkv_ref[:, nq + nk + s * LANES:nq + nk + (s + 1) * LANES]
        for slab, ref in ((kslab, k_ref), (vslab, v_ref)):
            rolled = pltpu.roll(slab, HEAD, axis=1)
            ref[2 * s] = jnp.where(lo, slab, rolled).astype(ref.dtype)
            ref[2 * s + 1] = jnp.where(lo, rolled, slab).astype(ref.dtype)


def _at_prep(qkv, q_g, k_g, cos, sin):
    b, t, _ = qkv.shape
    nq = AT_HEADS * HEAD
    width = qkv.shape[2]
    kvs = jax.ShapeDtypeStruct((b, AT_KV_HEADS, t, LANES), BF16)
    kv_spec = pl.BlockSpec((None, AT_KV_HEADS, TM, LANES), lambda b_, i: (b_, 0, i, 0))
    tab = pl.BlockSpec((TM, LANES), lambda b_, i: (i, 0))
    tile2 = lambda g: jnp.tile(g.reshape(1, HEAD), (1, LANES // HEAD))
    return pl.pallas_call(
        functools.partial(_at_prep_body, scale=HEAD ** -0.5),
        out_shape=(jax.ShapeDtypeStruct((b, t, nq), BF16), kvs, kvs),
        grid=(b, t // TM),
        in_specs=[_stream(width), _vec(LANES), _vec(LANES), tab, tab],
        out_specs=(_stream(nq), kv_spec, kv_spec),
        compiler_params=_params(("parallel", "parallel")),
        name="at_prep",
    )(qkv, tile2(q_g), tile2(k_g), cos, sin)


def _flash_body(q_ref, k_ref, v_ref, o_ref, *, tk):
    tq = q_ref.shape[0]
    lane = lax.broadcasted_iota(jnp.int32, (tq, LANES), 1)
    lo = lane < HEAD
    q = q_ref[...]
    zero = jnp.zeros_like(q)
    qs = jnp.concatenate([jnp.where(lo, q, zero), jnp.where(lo, zero, q)], axis=0)

    def step(kc, carry):
        m, l, acc = carry
        start = pl.multiple_of(kc * tk, tk)
        s = _bdot_nt(qs, k_ref[pl.ds(start, tk), :])
        m_new = jnp.maximum(m, jnp.max(s, axis=-1, keepdims=True))
        alpha = jnp.exp(m - m_new)
        p = jnp.exp(s - m_new)
        l = alpha * l + jnp.sum(p, axis=-1, keepdims=True)
        acc = alpha * acc + _bdot(p, v_ref[pl.ds(start, tk), :])
        return m_new, l, acc

    init = (jnp.full((2 * tq, 1), -jnp.inf, F32), jnp.zeros((2 * tq, 1), F32), jnp.zeros((2 * tq, LANES), F32))
    _, l, acc = lax.fori_loop(0, k_ref.shape[0] // tk, step, init)
    o2 = acc / l
    o_ref[...] = jnp.where(lo, o2[0:tq], o2[tq:2 * tq]).astype(o_ref.dtype)


def _flash(q, k2, v2, *, n_ctx_rows):
    b, t, nq = q.shape
    tq = TM
    tl = t - n_ctx_rows
    off = n_ctx_rows // tq
    tk = _pick(t, (768, 512, 256, 128))
    npair = nq // LANES
    hp = AT_HEADS // AT_KV_HEADS // 2
    kv = pl.BlockSpec((None, None, t, LANES), lambda b_, p, i: (b_, p // hp, 0, 0))
    return pl.pallas_call(
        functools.partial(_flash_body, tk=tk),
        out_shape=jax.ShapeDtypeStruct((b, tl, nq), BF16),
        grid=(b, npair, tl // tq),
        in_specs=[pl.BlockSpec((None, tq, LANES), lambda b_, p, i: (b_, i + off, p)), kv, kv],
        out_specs=pl.BlockSpec((None, tq, LANES), lambda b_, p, i: (b_, i, p)),
        compiler_params=_params(("parallel", "parallel", "parallel")),
        name="flash_gqa",
    )(q, k2, v2)


def _rope_tables(n_ctx_rows, seq):
    quarter = HEAD // 4
    inv = ROPE_THETA ** (-jnp.arange(quarter, dtype=F32) / quarter)
    rows = jnp.repeat(jnp.arange(seq // GRID_W, dtype=jnp.int32), GRID_W).astype(F32)
    cols = (jnp.arange(seq, dtype=jnp.int32) % GRID_W).astype(F32)
    ar = rows[:, None] * inv
    ac = cols[:, None] * inv
    cos = jnp.concatenate([jnp.cos(ar), jnp.cos(ar), jnp.cos(ac), jnp.cos(ac)], axis=1)
    sin = jnp.concatenate([-jnp.sin(ar), jnp.sin(ar), -jnp.sin(ac), jnp.sin(ac)], axis=1)
    cos = jnp.concatenate([jnp.ones((n_ctx_rows, HEAD), F32), cos], axis=0)
    sin = jnp.concatenate([jnp.zeros((n_ctx_rows, HEAD), F32), sin], axis=0)
    return jnp.tile(cos, (1, LANES // HEAD)), jnp.tile(sin, (1, LANES // HEAD))


def _attn_layer(h, w, *, n_ctx_rows):
    b, t, d = h.shape
    qkv = _matmul(h.reshape(b * t, d), w["qkv_w"]).reshape(b, t, -1)
    cos, sin = _rope_tables(n_ctx_rows, t - n_ctx_rows)
    q, k2, v2 = _at_prep(qkv, w["q_g"], w["k_g"], cos, sin)
    o = _flash(q, k2, v2, n_ctx_rows=n_ctx_rows)
    tl = t - n_ctx_rows
    return _matmul(o.reshape(b * tl, -1), w["out_w"]).reshape(b, tl, d)


def _ffn(h2, w_in, w_out):
    return _matmul(_swiglu_in(h2, w_in), w_out)


def _moe_body(h_ref, g_ref, wg_ref, wu_ref, wo_ref, o_ref):
    e = pl.program_id(1)

    @pl.when(jnp.logical_and(e == 0, pl.program_id(2) == 0))
    def _():
        o_ref[...] = jnp.zeros_like(o_ref)

    h = h_ref[...]
    act = _silu(_bdot(h, wg_ref[...])) * _bdot(h, wu_ref[...])
    gates = g_ref[...]
    lane = lax.broadcasted_iota(jnp.int32, gates.shape, 1)
    gate = jnp.sum(jnp.where(lane == e, gates, 0.0), axis=-1, keepdims=True)
    o_ref[...] += gate * _bdot(act, wo_ref[...])


def _moe(h2, gates, w_in, w_out):
    m, d = h2.shape
    ne, _, f2 = w_in.shape
    f = f2 // 2
    tm = _pick(m, (1024, 768, 512, 256))
    fk = _pick(f, (512, 256, 128))
    nf = f // fk
    return pl.pallas_call(
        _moe_body,
        out_shape=jax.ShapeDtypeStruct((m, d), F32),
        grid=(m // tm, ne, nf),
        in_specs=[pl.BlockSpec((tm, d), lambda i, e, j: (i, 0)),
                  pl.BlockSpec((tm, LANES), lambda i, e, j: (i, 0)),
                  pl.BlockSpec((None, d, fk), lambda i, e, j: (e, 0, j)),
                  pl.BlockSpec((None, d, fk), lambda i, e, j: (e, 0, j + nf)),
                  pl.BlockSpec((None, fk, d), lambda i, e, j: (e, j, 0))],
        out_specs=pl.BlockSpec((tm, d), lambda i, e, j: (i, 0)),
        compiler_params=_params(("parallel", "arbitrary", "arbitrary")),
        name="moe_dense",
    )(h2, gates, w_in, w_in, w_out)


def kernel(x, c, ctx, c_ctx, mod_w, mod_b, norm1_g, norm2_g, final_g, mb_in_w, mb_conv_w, mb_conv_b, mb_dt_bias, mb_a_log, mb_d, mb_norm_g, mb_out_w, rw_mix, rw_rkv_w, rw_w0, rw_w1, rw_w2, rw_a0, rw_a1, rw_a2, rw_g1, rw_g2, rw_k_k, rw_k_a, rw_r_k, rw_ln_g, rw_ln_b, rw_out_w, pl_w, pl_scale, at_qkv_w, at_q_g, at_k_g, at_out_w, ff_in_w, ff_out_w, moe_router_w, moe_in_w, moe_out_w):
    b, seq, d = x.shape
    n_ctx = ctx.shape[1]
    depth = mod_w.shape[0]
    t = n_ctx + seq
    nct = n_ctx // TM
    assert depth == 4 and n_ctx % TM == 0 and seq % TM == 0 and b + 1 <= 8
    bf = lambda a: a.astype(BF16)

    svec = jnp.concatenate([c, c_ctx[None, :], jnp.zeros((8 - b - 1, d), F32)], axis=0)
    mods = _mods(svec, bf(mod_w), mod_b)
    mv = jnp.pad(mods[:, :b + 1].reshape(depth, b + 1, 6, d), ((0, 0), (0, 0), (0, 2), (0, 0)))

    xs = jnp.concatenate([ctx, x], axis=1)
    m = b * t
    resid = functools.partial(_resid_norm, nct=nct, x_off=0)

    di = MB_HEADS * HEAD
    xbc = di + 2 * MB_GROUPS * MB_STATE
    mamba_w = dict(in_w=bf(mb_in_w[0]),
                   in_w_dt=bf(jnp.pad(mb_in_w[0][:, di + xbc:], ((0, 0), (0, LANES - 2 * MB_HEADS)))),
                   conv_w=mb_conv_w[0], conv_b=mb_conv_b[0], dt_bias=mb_dt_bias[0], a_log=mb_a_log[0],
                   dvec=jnp.repeat(mb_d[0], HEAD).reshape(1, di), norm_g=mb_norm_g[0], out_w=bf(mb_out_w[0]))
    h = _norm_mod(xs, norm1_g[0], mv, 0, nct, BF16)
    y = _mamba_layer(h, mamba_w, n_ctx_rows=n_ctx)
    xs, h2 = resid(xs, y, mv, norm2_g[0], gate_layer=0, jg=2, mod_layer=0, js=3, jc=4, out_dtype=BF16)
    f = _ffn(h2.reshape(m, d), bf(ff_in_w[0]), bf(ff_out_w[0])).reshape(b, t, d)
    xs, h = resid(xs, f, mv, norm1_g[1], gate_layer=0, jg=5, mod_layer=1, js=0, jc=1, out_dtype=F32)

    blockdiag = lambda u: jnp.concatenate(
        [jnp.concatenate([u[0], jnp.zeros_like(u[0])], axis=1),
         jnp.concatenate([jnp.zeros_like(u[1]), u[1]], axis=1)], axis=0)
    rwkv_w = dict(mix=rw_mix[0], r_w=bf(rw_rkv_w[0, 0]), k_w=bf(rw_rkv_w[0, 1]), v_w=bf(rw_rkv_w[0, 2]),
                  g1=bf(rw_g1[0]), g2=bf(rw_g2[0]),
                  w1=bf(jnp.concatenate([rw_w1[0, 0], rw_w1[0, 1]], axis=1)), w2=bf(blockdiag(rw_w2[0])),
                  w0=rw_w0[0].reshape(1, 2 * d),
                  a1=bf(jnp.concatenate([rw_a1[0, 0], rw_a1[0, 1]], axis=1)), a2=bf(blockdiag(rw_a2[0])),
                  a0=rw_a0[0].reshape(1, 2 * d),
                  k_k=rw_k_k[0], k_a=rw_k_a[0], r_k=rw_r_k[0].reshape(d), ln_g=rw_ln_g[0], ln_b=rw_ln_b[0],
                  out_w=bf(rw_out_w[0]))
    y = _rwkv_layer(h, rwkv_w, n_ctx_rows=n_ctx)
    xs, h2, gates = _resid_norm_router(xs, y, mv, norm2_g[1], moe_router_w[0], gate_layer=1, jg=2,
                                       mod_layer=1, js=3, jc=4, nct=nct, x_off=0)
    f = _moe(h2.reshape(m, d), gates.reshape(m, LANES), bf(moe_in_w[0]), bf(moe_out_w[0])).reshape(b, t, d)
    xs, h = resid(xs, f, mv, norm1_g[2], gate_layer=1, jg=5, mod_layer=2, js=0, jc=1, out_dtype=F32)

    y = _pool_layer(h, bf(pl_w[0]), pl_scale[0], nct=nct)
    xs, h2 = resid(xs, y, mv, norm2_g[2], gate_layer=2, jg=2, mod_layer=2, js=3, jc=4, out_dtype=BF16)
    f = _ffn(h2.reshape(m, d), bf(ff_in_w[1]), bf(ff_out_w[1])).reshape(b, t, d)
    xs, h = resid(xs, f, mv, norm1_g[3], gate_layer=2, jg=5, mod_layer=3, js=0, jc=1, out_dtype=BF16)

    attn_w = dict(qkv_w=bf(at_qkv_w[0]), q_g=at_q_g[0], k_g=at_k_g[0], out_w=bf(at_out_w[0]))
    y = _attn_layer(h, attn_w, n_ctx_rows=n_ctx)
    xl, h2, gates = _resid_norm_router(xs, y, mv, norm2_g[3], moe_router_w[1], gate_layer=3, jg=2,
                                       mod_layer=3, js=3, jc=4, nct=0, x_off=nct)
    ml = b * seq
    f = _moe(h2.reshape(ml, d), gates.reshape(ml, LANES), bf(moe_in_w[1]), bf(moe_out_w[1])).reshape(b, seq, d)
    return _resid_final(xl, f, mv, final_g, gate_layer=3, jg=5)
```

```python
import functools
import math

import jax
import jax.numpy as jnp
from jax import lax
from jax.experimental import pallas as pl
from jax.experimental.pallas import tpu as pltpu

F32 = jnp.float32
BF16 = jnp.bfloat16

NORM_EPS = 1e-6
TM = 256
LANES = 128
HEAD = 64
VMEM_LIMIT = 48 * 1024 * 1024

MB_HEADS = 32
MB_GROUPS = 4
MB_STATE = 128
MB_CHUNK = 128
MB_CONV = 5
RW_CHUNK = 64
RW_DECAY_SCALE = 0.606531
RW_LN_EPS = 64e-5
POOL_WINDOWS = (2, 4, 8, 16)
POOL_HALO = 16
AT_HEADS = 16
AT_KV_HEADS = 4
ROPE_THETA = 10000.0
GRID_W = 64
N_EXPERTS = 8


def _params(sem, vmem=VMEM_LIMIT):
    return pltpu.CompilerParams(dimension_semantics=sem, vmem_limit_bytes=vmem)


def _bdot(a, b):
    return jnp.dot(a.astype(BF16), b.astype(BF16), preferred_element_type=F32)


def _bdot_nt(a, b):
    return lax.dot_general(a.astype(BF16), b.astype(BF16), (((1,), (1,)), ((), ())),
                           preferred_element_type=F32)


def _bdot_tn(a, b):
    return lax.dot_general(a.astype(BF16), b.astype(BF16), (((0,), (0,)), ((), ())),
                           preferred_element_type=F32)


def _split3(x):
    p0 = x.astype(BF16)
    r1 = x - p0.astype(F32)
    p1 = r1.astype(BF16)
    p2 = (r1 - p1.astype(F32)).astype(BF16)
    return p0, p1, p2


def _as01(mask):
    return jnp.where(mask, 1.0, 0.0).astype(BF16)


def _dot01_left(sel, x):
    sel = _as01(sel)
    p0, p1, p2 = _split3(x)
    return (jnp.dot(sel, p0, preferred_element_type=F32)
            + jnp.dot(sel, p1, preferred_element_type=F32)
            + jnp.dot(sel, p2, preferred_element_type=F32))


def _dot01_right(x, sel):
    sel = _as01(sel)
    p0, p1, p2 = _split3(x)
    return (jnp.dot(p0, sel, preferred_element_type=F32)
            + jnp.dot(p1, sel, preferred_element_type=F32)
            + jnp.dot(p2, sel, preferred_element_type=F32))


def _head_sum(x):
    r = lax.broadcasted_iota(jnp.int32, (LANES, LANES), 0) // HEAD
    c = lax.broadcasted_iota(jnp.int32, (LANES, LANES), 1) // HEAD
    ones_bd = r == c
    slabs = [_dot01_right(x[:, s:s + LANES], ones_bd) for s in range(0, x.shape[1], LANES)]
    return slabs[0] if len(slabs) == 1 else jnp.concatenate(slabs, axis=1)


def _sigmoid(x):
    return 1.0 / (1.0 + jnp.exp(-x))


def _silu(x):
    return x * _sigmoid(x)


def _rms(x, g):
    ms = jnp.mean(x * x, axis=-1, keepdims=True)
    return x * lax.rsqrt(ms + NORM_EPS) * g


def _mods_body(s_ref, w_ref, b_ref, o_ref):
    o_ref[...] = _bdot(_silu(s_ref[...]), w_ref[...]) + b_ref[...]


def _mods(svec, mod_w, mod_b):
    depth, d, n6 = mod_w.shape
    tn = 1536
    return pl.pallas_call(
        _mods_body,
        out_shape=jax.ShapeDtypeStruct((depth, 8, n6), F32),
        grid=(depth, n6 // tn),
        in_specs=[pl.BlockSpec((8, d), lambda l, j: (0, 0)),
                  pl.BlockSpec((None, d, tn), lambda l, j: (l, 0, j)),
                  pl.BlockSpec((None, 1, tn), lambda l, j: (l, 0, j))],
        out_specs=pl.BlockSpec((None, 8, tn), lambda l, j: (l, 0, j)),
        compiler_params=_params(("parallel", "parallel")),
        name="mods",
    )(svec, mod_w, mod_b.reshape(depth, 1, n6))


def _stream(width, off=0, rows=TM):
    return pl.BlockSpec((None, rows, width), lambda b, i: (b, i + off, 0))


def _stream2(width, d, off=0):
    return pl.BlockSpec((None, None, TM, width), lambda b, i: (d, b, i + off, 0))


def _vec(width, rows=1):
    return pl.BlockSpec((rows, width), lambda b, i: (0, 0))


def _mv(layer, nct, nb, d):
    return pl.BlockSpec((None, None, 8, d), lambda b, i: (layer, jnp.where(i < nct, nb, b), 0, 0))


def _halo(width, nrows, t_total, side):
    per = TM // nrows
    last = t_total // nrows - 1
    if side < 0:
        return pl.BlockSpec((None, nrows, width), lambda b, i: (b, jnp.maximum(i * per - 1, 0), 0))
    return pl.BlockSpec((None, nrows, width), lambda b, i: (b, jnp.minimum((i + 1) * per, last), 0))


def _seq_edges(i, nt, nct):
    first = jnp.logical_or(i == 0, i == nct)
    last = jnp.logical_or(i == nct - 1, i == nt - 1)
    return first, last


def _shift_rows(x, prev8, next8, o):
    rows = x.shape[0]
    r8 = lax.broadcasted_iota(jnp.int32, (8, x.shape[1]), 0)
    if o < 0:
        k = -o
        s = pltpu.roll(x, k, axis=0)
        top = jnp.where(r8 < k, pltpu.roll(prev8, k, axis=0), s[0:8])
        return jnp.concatenate([top, s[8:]], axis=0)
    s = pltpu.roll(x, rows - o, axis=0)
    bot = jnp.where(r8 >= 8 - o, pltpu.roll(next8, 8 - o, axis=0), s[rows - 8:])
    return jnp.concatenate([s[:rows - 8], bot], axis=0)


def _norm_mod_body(x_ref, g_ref, mv_ref, h_ref, *, js, jc):
    h = _rms(x_ref[...], g_ref[...]) * (1.0 + mv_ref[jc:jc + 1, :]) + mv_ref[js:js + 1, :]
    h_ref[...] = h.astype(h_ref.dtype)


def _norm_mod(x, g, mv, layer, nct, out_dtype):
    b, t, d = x.shape
    nb = b
    return pl.pallas_call(
        functools.partial(_norm_mod_body, js=0, jc=1),
        out_shape=jax.ShapeDtypeStruct((b, t, d), out_dtype),
        grid=(b, t // TM),
        in_specs=[_stream(d), _vec(d), _mv(layer, nct, nb, d)],
        out_specs=_stream(d),
        compiler_params=_params(("parallel", "parallel")),
        name="norm_mod",
    )(x, g.reshape(1, d), mv)


def _resid_norm_body(x_ref, y_ref, mvg_ref, g_ref, mvm_ref, xo_ref, h_ref, *, jg, js, jc):
    xn = x_ref[...] + mvg_ref[jg:jg + 1, :] * y_ref[...].astype(F32)
    xo_ref[...] = xn
    h = _rms(xn, g_ref[...]) * (1.0 + mvm_ref[jc:jc + 1, :]) + mvm_ref[js:js + 1, :]
    h_ref[...] = h.astype(h_ref.dtype)


def _resid_norm(x, y, mv, g, *, gate_layer, jg, mod_layer, js, jc, nct, x_off, out_dtype):
    b, t, d = y.shape
    return pl.pallas_call(
        functools.partial(_resid_norm_body, jg=jg, js=js, jc=jc),
        out_shape=(jax.ShapeDtypeStruct((b, t, d), F32), jax.ShapeDtypeStruct((b, t, d), out_dtype)),
        grid=(b, t // TM),
        in_specs=[_stream(d, x_off), _stream(d), _mv(gate_layer, nct, b, d), _vec(d),
                  _mv(mod_layer, nct, b, d)],
        out_specs=(_stream(d), _stream(d)),
        compiler_params=_params(("parallel", "parallel")),
        name="resid_norm",
    )(x, y, mv, g.reshape(1, d), mv)


def _top2_gates(logits):
    lane = lax.broadcasted_iota(jnp.int32, logits.shape, 1)
    neg = jnp.float32(-jnp.inf)
    lg = jnp.where(lane < N_EXPERTS, logits, neg)
    v1 = jnp.max(lg, axis=-1, keepdims=True)
    i1 = jnp.min(jnp.where(lg == v1, lane, LANES), axis=-1, keepdims=True)
    lg2 = jnp.where(lane == i1, neg, lg)
    v2 = jnp.max(lg2, axis=-1, keepdims=True)
    i2 = jnp.min(jnp.where(lg2 == v2, lane, LANES), axis=-1, keepdims=True)
    e = jnp.exp(v2 - v1)
    w1 = 1.0 / (1.0 + e)
    w2 = e / (1.0 + e)
    return jnp.where(lane == i1, w1, 0.0) + jnp.where(lane == i2, w2, 0.0)


def _resid_norm_router_body(x_ref, y_ref, mvg_ref, g_ref, mvm_ref, rw_ref, xo_ref, h_ref, gates_ref,
                            *, jg, js, jc):
    xn = x_ref[...] + mvg_ref[jg:jg + 1, :] * y_ref[...].astype(F32)
    xo_ref[...] = xn
    h = _rms(xn, g_ref[...]) * (1.0 + mvm_ref[jc:jc + 1, :]) + mvm_ref[js:js + 1, :]
    h_ref[...] = h.astype(h_ref.dtype)
    h0, h1, h2 = _split3(h)
    w0, w1, w2 = _split3(rw_ref[...])
    dot = lambda a, c: jnp.dot(a, c, preferred_element_type=F32)
    logits = (dot(h0, w0) + (dot(h0, w1) + dot(h1, w0))
              + (dot(h0, w2) + dot(h1, w1) + dot(h2, w0)))
    gates_ref[...] = _top2_gates(logits)


def _resid_norm_router(x, y, mv, g, router_w, *, gate_layer, jg, mod_layer, js, jc, nct, x_off):
    b, t, d = y.shape
    rw = jnp.pad(router_w, ((0, 0), (0, LANES - router_w.shape[1])))
    return pl.pallas_call(
        functools.partial(_resid_norm_router_body, jg=jg, js=js, jc=jc),
        out_shape=(jax.ShapeDtypeStruct((b, t, d), F32), jax.ShapeDtypeStruct((b, t, d), BF16),
                   jax.ShapeDtypeStruct((b, t, LANES), F32)),
        grid=(b, t // TM),
        in_specs=[_stream(d, x_off), _stream(d), _mv(gate_layer, nct, b, d), _vec(d),
                  _mv(mod_layer, nct, b, d), _vec(LANES, d)],
        out_specs=(_stream(d), _stream(d), _stream(LANES)),
        compiler_params=_params(("parallel", "parallel")),
        name="resid_norm_router",
    )(x, y, mv, g.reshape(1, d), mv, rw)


def _resid_final_body(x_ref, y_ref, mvg_ref, g_ref, o_ref, *, jg):
    xn = x_ref[...] + mvg_ref[jg:jg + 1, :] * y_ref[...].astype(F32)
    o_ref[...] = _rms(xn, g_ref[...])


def _resid_final(x, y, mv, g, *, gate_layer, jg):
    b, t, d = y.shape
    return pl.pallas_call(
        functools.partial(_resid_final_body, jg=jg),
        out_shape=jax.ShapeDtypeStruct((b, t, d), F32),
        grid=(b, t // TM),
        in_specs=[_stream(d), _stream(d), _mv(gate_layer, 0, b, d), _vec(d)],
        out_specs=_stream(d),
        compiler_params=_params(("parallel", "parallel")),
        name="resid_final",
    )(x, y, mv, g.reshape(1, d))


def _mm_body(x_ref, w_ref, o_ref):
    o_ref[...] = _bdot(x_ref[...], w_ref[...]).astype(o_ref.dtype)


def _pick(n, prefs):
    for p in prefs:
        if n % p == 0:
            return p
    raise ValueError(f"no tile for {n}")


def _matmul(x, w, *, n_cols=None, out_dtype=F32, tm=None, tn=None):
    m, k = x.shape
    n = n_cols or w.shape[1]
    tm = tm or _pick(m, (1024, 768, 512, 256))
    tn = tn or _pick(n, (1024, 768, 512, 256, 128))
    return pl.pallas_call(
        _mm_body,
        out_shape=jax.ShapeDtypeStruct((m, n), out_dtype),
        grid=(n // tn, m // tm),
        in_specs=[pl.BlockSpec((tm, k), lambda j, i: (i, 0)),
                  pl.BlockSpec((k, tn), lambda j, i: (0, j))],
        out_specs=pl.BlockSpec((tm, tn), lambda j, i: (i, j)),
        compiler_params=_params(("parallel", "parallel")),
        name="matmul",
    )(x, w)


def _swiglu_body(x_ref, wg_ref, wu_ref, o_ref):
    x = x_ref[...]
    o_ref[...] = (_silu(_bdot(x, wg_ref[...])) * _bdot(x, wu_ref[...])).astype(o_ref.dtype)


def _swiglu_in(x, w_in, *, tm=None, tn=None):
    m, k = x.shape
    f = w_in.shape[1] // 2
    tm = tm or _pick(m, (768, 512, 256))
    tn = tn or _pick(f, (1408, 896, 512, 256, 128))
    nf = f // tn
    return pl.pallas_call(
        _swiglu_body,
        out_shape=jax.ShapeDtypeStruct((m, f), BF16),
        grid=(nf, m // tm),
        in_specs=[pl.BlockSpec((tm, k), lambda j, i: (i, 0)),
                  pl.BlockSpec((k, tn), lambda j, i: (0, j)),
                  pl.BlockSpec((k, tn), lambda j, i: (0, j + nf))],
        out_specs=pl.BlockSpec((tm, tn), lambda j, i: (i, j)),
        compiler_params=_params(("parallel", "parallel")),
        name="swiglu_in",
    )(x, w_in, w_in)


def _lora_body(x_ref, a_ref, b_ref, bias_ref, o_ref, *, act):
    t = _bdot(x_ref[...], a_ref[...])
    if act == "tanh":
        t = jnp.tanh(t)
    elif act == "sigmoid":
        t = _sigmoid(t)
    o_ref[...] = _bdot(t, b_ref[...]) + bias_ref[...]


def _lora(x, a, bm, bias, act):
    m, k = x.shape
    r = a.shape[1]
    n = bm.shape[1]
    tm = _pick(m, (512, 256))
    return pl.pallas_call(
        functools.partial(_lora_body, act=act),
        out_shape=jax.ShapeDtypeStruct((m, n), F32),
        grid=(m // tm,),
        in_specs=[pl.BlockSpec((tm, k), lambda i: (i, 0)),
                  pl.BlockSpec((k, r), lambda i: (0, 0)),
                  pl.BlockSpec((r, n), lambda i: (0, 0)),
                  pl.BlockSpec((1, n), lambda i: (0, 0))],
        out_specs=pl.BlockSpec((tm, n), lambda i: (i, 0)),
        compiler_params=_params(("parallel",)),
        name="lora",
    )(x, a, bm, bias)


def _mb_conv_body(x_ref, xp_ref, xn_ref, w_ref, b_ref, o_ref, *, nt, nct):
    i = pl.program_id(1)
    first, last = _seq_edges(i, nt, nct)
    x = x_ref[...]
    prev8 = jnp.where(first, 0.0, xp_ref[...])
    next8 = jnp.where(last, 0.0, xn_ref[...])
    pad = (MB_CONV - 1) // 2
    acc = x * w_ref[pad:pad + 1, :] + b_ref[...]
    for o in range(-pad, pad + 1):
        if o != 0:
            acc = acc + _shift_rows(x, prev8, next8, o) * w_ref[pad + o:pad + o + 1, :]
    o_ref[...] = _silu(acc)


def _mb_conv(zx, conv_w, conv_b, *, col0, nct):
    b, t, _ = zx.shape
    c = conv_w.shape[1]
    tc = 512
    cb = col0 // tc
    nt = t // TM
    per = TM // 8
    lastb = t // 8 - 1
    w8 = jnp.pad(conv_w, ((0, 8 - conv_w.shape[0]), (0, 0)))
    return pl.pallas_call(
        functools.partial(_mb_conv_body, nt=nt, nct=nct),
        out_shape=jax.ShapeDtypeStruct((b, t, c), F32),
        grid=(b, nt, c // tc),
        in_specs=[pl.BlockSpec((None, TM, tc), lambda b_, i, j: (b_, i, cb + j)),
                  pl.BlockSpec((None, 8, tc), lambda b_, i, j: (b_, jnp.maximum(i * per - 1, 0), cb + j)),
                  pl.BlockSpec((None, 8, tc),
                               lambda b_, i, j: (b_, jnp.minimum((i + 1) * per, lastb), cb + j)),
                  pl.BlockSpec((8, tc), lambda b_, i, j: (0, j)),
                  pl.BlockSpec((1, tc), lambda b_, i, j: (0, j))],
        out_specs=pl.BlockSpec((None, TM, tc), lambda b_, i, j: (b_, i, j)),
        compiler_params=_params(("parallel", "parallel", "parallel")),
        name="mb_conv",
    )(zx, zx, zx, w8, conv_b.reshape(1, c))


def _softplus(x):
    return jnp.maximum(x, 0.0) + jnp.log(1.0 + jnp.exp(-jnp.abs(x)))


def _lane_bcast(col, width):
    return jnp.broadcast_to(col, (col.shape[0], width))


def _ssd_body(x_ref, b_ref, c_ref, dtc_ref, dtr_ref, bc_ref, br_ref, ac_ref, ar_ref, y_ref, st_ref):
    d = pl.program_id(1)
    q = MB_CHUNK
    hpg = MB_HEADS // MB_GROUPS

    @pl.when(pl.program_id(2) == 0)
    def _():
        st_ref[...] = jnp.zeros_like(st_ref)

    sgn = 1 - 2 * d
    dt_c = _softplus(dtc_ref[...] + bc_ref[...])
    dt_r = _softplus(dtr_ref[...] + br_ref[...])
    dta_c = dt_c * (-jnp.exp(ac_ref[...]))
    dta_r = dt_r * (-jnp.exp(ar_ref[...]))
    ii = lax.broadcasted_iota(jnp.int32, (q, q), 0)
    jj = lax.broadcasted_iota(jnp.int32, (q, q), 1)
    mask = (ii - jj) * sgn >= 0
    cum_c = _dot01_left(mask, dta_c)
    cum_r = _dot01_right(dta_r, (jj - ii) * sgn >= 0)
    tot_c = jnp.sum(dta_c, axis=0, keepdims=True)
    ecum_c = jnp.exp(cum_c)
    f_c = jnp.exp(tot_c - cum_c) * dt_c
    etot_c = jnp.exp(tot_c)
    lane = lax.broadcasted_iota(jnp.int32, (q, LANES), 1)
    lo = lane < HEAD
    lane1 = lax.broadcasted_iota(jnp.int32, (1, LANES), 1)

    for g in range(MB_GROUPS):
        bg = b_ref[:, g * MB_STATE:(g + 1) * MB_STATE].astype(BF16)
        cg = c_ref[:, g * MB_STATE:(g + 1) * MB_STATE].astype(BF16)
        gmat = _bdot_nt(cg, bg)
        for pq in range(hpg // 2):
            p = g * (hpg // 2) + pq
            h0 = 2 * p
            xp = x_ref[:, p * LANES:(p + 1) * LANES]
            xpb = xp.astype(BF16)
            ys = []
            for h in (h0, h0 + 1):
                seg = _lane_bcast(cum_c[:, h:h + 1], q) - cum_r[h:h + 1, :]
                wmat = gmat * (jnp.exp(jnp.where(mask, seg, -jnp.inf)) * dt_r[h:h + 1, :])
                ys.append(_bdot(wmat, xpb))
            y_intra = jnp.where(lo, ys[0], ys[1])
            pair = lambda a: jnp.where(lo, _lane_bcast(a[:, h0:h0 + 1], LANES),
                                       _lane_bcast(a[:, h0 + 1:h0 + 2], LANES))
            st = st_ref[p]
            y_ref[:, p * LANES:(p + 1) * LANES] = y_intra + _bdot(cg, st) * pair(ecum_c)
            upd = _bdot_tn(bg, xp * pair(f_c))
            et = jnp.where(lane1 < HEAD, _lane_bcast(etot_c[:, h0:h0 + 1], LANES),
                           _lane_bcast(etot_c[:, h0 + 1:h0 + 2], LANES))
            st_ref[p] = st * et + upd


def _scan_chunk(d, c, n_ctx, n_all):
    rev = jnp.where(c < n_ctx, n_ctx - 1 - c, n_all - 1 - (c - n_ctx))
    return jnp.where(d == 0, c, rev)


def _ssd(xa, dtc, dtr, dt_bias, a_log, *, n_ctx_rows):
    b, t, _ = xa.shape
    q = MB_CHUNK
    nh = MB_HEADS
    di = nh * HEAD
    gn = MB_GROUPS * MB_STATE
    nc = t // q
    ncc = n_ctx_rows // q
    tc = functools.partial(_scan_chunk, n_ctx=ncc, n_all=nc)
    bias_c = dt_bias.reshape(2, 1, nh)
    bias_r = dt_bias.reshape(2, nh, 1)
    a_c = a_log.reshape(2, 1, nh)
    a_r = a_log.reshape(2, nh, 1)
    small_c = pl.BlockSpec((None, 1, nh), lambda b_, d, c: (d, 0, 0))
    small_r = pl.BlockSpec((None, nh, 1), lambda b_, d, c: (d, 0, 0))
    return pl.pallas_call(
        _ssd_body,
        out_shape=jax.ShapeDtypeStruct((2, b, t, di), F32),
        grid=(b, 2, nc),
        in_specs=[pl.BlockSpec((None, q, di), lambda b_, d, c: (b_, tc(d, c), 0)),
                  pl.BlockSpec((None, q, gn), lambda b_, d, c: (b_, tc(d, c), di // gn)),
                  pl.BlockSpec((None, q, gn), lambda b_, d, c: (b_, tc(d, c), di // gn + 1)),
                  pl.BlockSpec((None, None, q, nh), lambda b_, d, c: (d, b_, tc(d, c), 0)),
                  pl.BlockSpec((None, None, nh, q), lambda b_, d, c: (d, b_, 0, tc(d, c))),
                  small_c, small_r, small_c, small_r],
        out_specs=pl.BlockSpec((None, None, q, di), lambda b_, d, c: (d, b_, tc(d, c), 0)),
        scratch_shapes=[pltpu.VMEM((nh // 2, MB_STATE, LANES), F32)],
        compiler_params=_params(("parallel", "parallel", "arbitrary")),
        name="ssd_scan",
    )(xa, xa, xa, dtc, dtr, bias_c, bias_r, a_c, a_r)


def _mb_gate_body(y0_ref, y1_ref, xs_ref, z_ref, dv_ref, g_ref, o_ref):
    y = y0_ref[...] + y1_ref[...] + xs_ref[...] * dv_ref[...]
    o_ref[...] = _rms(y * _silu(z_ref[...]), g_ref[...]).astype(o_ref.dtype)


def _mb_gate(y, xa, zx, dvec, norm_g):
    _, b, t, di = y.shape
    return pl.pallas_call(
        _mb_gate_body,
        out_shape=jax.ShapeDtypeStruct((b, t, di), BF16),
        grid=(b, t // TM),
        in_specs=[_stream2(di, 0), _stream2(di, 1), _stream(di), _stream(di), _vec(di), _vec(di)],
        out_specs=_stream(di),
        compiler_params=_params(("parallel", "parallel")),
        name="mb_gate",
    )(y, y, xa, zx, dvec, norm_g.reshape(1, di))


def _mamba_layer(h, w, *, n_ctx_rows):
    b, t, d = h.shape
    m = b * t
    di = MB_HEADS * HEAD
    xbc = di + 2 * MB_GROUPS * MB_STATE
    h2 = h.reshape(m, d)
    zx = _matmul(h2, w["in_w"], n_cols=di + xbc).reshape(b, t, di + xbc)
    dt_raw = _matmul(h2, w["in_w_dt"], tn=LANES).reshape(b, t, LANES)[..., :2 * MB_HEADS]
    dtc = jnp.moveaxis(dt_raw.reshape(b, t, 2, MB_HEADS), 2, 0)
    dtr = jnp.swapaxes(dtc, 2, 3)
    xa = _mb_conv(zx, w["conv_w"], w["conv_b"], col0=di, nct=n_ctx_rows // TM)
    y = _ssd(xa, dtc, dtr, w["dt_bias"], w["a_log"], n_ctx_rows=n_ctx_rows)
    gated = _mb_gate(y, xa, zx, w["dvec"], w["norm_g"])
    return _matmul(gated.reshape(m, di), w["out_w"]).reshape(b, t, d)


def _rw_mix_body(h_ref, hp_ref, hn_ref, mix_ref, *o_refs, nt, nct):
    i = pl.program_id(1)
    first, last = _seq_edges(i, nt, nct)
    h = h_ref[...]
    prow = jnp.where(first, 0.0, hp_ref[7:8, :])
    nrow = jnp.where(last, 0.0, hn_ref[0:1, :])
    row = lax.broadcasted_iota(jnp.int32, h.shape, 0)
    prev = jnp.where(row == 0, prow, pltpu.roll(h, 1, axis=0))
    nxt = jnp.where(row == TM - 1, nrow, pltpu.roll(h, TM - 1, axis=0))
    xx = 0.5 * (prev + nxt) - h
    for j, o_ref in enumerate(o_refs):
        o_ref[...] = (h + xx * mix_ref[j:j + 1, :]).astype(o_ref.dtype)


def _rw_mix(h, mix, *, nct):
    b, t, d = h.shape
    nt = t // TM
    mix8 = jnp.pad(mix, ((0, 8 - mix.shape[0]), (0, 0)))
    return pl.pallas_call(
        functools.partial(_rw_mix_body, nt=nt, nct=nct),
        out_shape=tuple(jax.ShapeDtypeStruct((b, t, d), BF16) for _ in range(6)),
        grid=(b, nt),
        in_specs=[_stream(d), _halo(d, 8, t, -1), _halo(d, 8, t, +1), _vec(d, 8)],
        out_specs=tuple(_stream(d) for _ in range(6)),
        compiler_params=_params(("parallel", "parallel")),
        name="rw_mix",
    )(h, h, h, mix8)


def _bmm(a, b):
    return jnp.einsum('pij,pjk->pik', a.astype(BF16), b.astype(BF16), preferred_element_type=F32)


def _bmm_nt(a, b):
    return jnp.einsum('pik,pjk->pij', a.astype(BF16), b.astype(BF16), preferred_element_type=F32)


def _lane_pairs(x):
    return jnp.stack([x[:, p * LANES:(p + 1) * LANES] for p in range(x.shape[1] // LANES)], axis=0)


def _rw_chunk_body(r_ref, k_ref, v_ref, wl_ref, al_ref, kkk_ref, ka_ref, rk_ref,
                   rhat_ref, yhat_ref, w_ref, g_ref, gam_ref, bonus_ref):
    c = RW_CHUNK
    c2 = 2 * c
    d = r_ref.shape[1]
    r = r_ref[...]
    k = k_ref[...]
    v = v_ref[...]
    kkr = k * kkk_ref[...]
    kk = kkr / jnp.maximum(jnp.sqrt(_head_sum(kkr * kkr)), 1e-12)
    lws, kds, bbs = [], [], []
    for dr in range(2):
        lws.append(-RW_DECAY_SCALE * _sigmoid(wl_ref[:, dr * d:(dr + 1) * d]))
        a = _sigmoid(al_ref[:, dr * d:(dr + 1) * d])
        kds.append(k * (1.0 + (a - 1.0) * ka_ref[...]))
        bbs.append(kk * a)
    bonus_ref[...] = _head_sum(r * rk_ref[...] * (0.5 * (kds[0] + kds[1]))) * v

    ti = lax.broadcasted_iota(jnp.int32, (c, c), 0)
    si = lax.broadcasted_iota(jnp.int32, (c, c), 1)
    ri = lax.broadcasted_iota(jnp.int32, (c2, c2), 0)
    ci = lax.broadcasted_iota(jnp.int32, (c2, c2), 1)
    same = (ri >= c) == (ci >= c)
    dtm = (ri & (c - 1)) - (ci & (c - 1))
    eye = jnp.where(ri == ci, 1.0, 0.0)
    lo = lax.broadcasted_iota(jnp.int32, (c, LANES), 1) < HEAD

    def stack(x):
        return jnp.concatenate([jnp.where(lo, x, 0.0), jnp.where(lo, 0.0, x)], axis=1)

    def unstack(x2):
        return x2[:, 0:c] + x2[:, c:c2]

    v_p = _lane_pairs(v)
    v2 = stack(v_p)
    npair = v_p.shape[0]
    for dr in range(2):
        sgn = 1 - 2 * dr
        lw = lws[dr]
        cum = _dot01_left((ti - si) * sgn >= 0, lw)
        eg = jnp.exp(cum)
        einv = jnp.exp(-cum)
        at_p = _lane_pairs(-kk * jnp.exp(cum - lw))
        rt_p = _lane_pairs(r * eg)
        kt_p = _lane_pairs(kds[dr] * einv)
        bt_p = _lane_pairs(bbs[dr] * einv)
        gam_ref[dr] = eg[c - 1:c, :] if dr == 0 else eg[0:1, :]
        m_strict = jnp.logical_and(same, dtm * sgn > 0)
        m_incl = jnp.logical_and(same, dtm * sgn >= 0)

        at2 = stack(at_p)
        lhs = jnp.concatenate([at2, stack(rt_p)], axis=1)
        rhs = jnp.concatenate([bt_p, bt_p, kt_p, kt_p], axis=1)
        nn = _bmm_nt(lhs, rhs)
        n_ab = jnp.where(m_strict, nn[:, 0:c2, 0:c2], 0.0)
        n_ak = jnp.where(m_strict, nn[:, 0:c2, c2:2 * c2], 0.0)
        n_qb = jnp.where(m_incl, nn[:, c2:2 * c2, 0:c2], 0.0)
        n_qk = jnp.where(m_incl, nn[:, c2:2 * c2, c2:2 * c2], 0.0)
        tinv = eye + n_ab
        pw = n_ab
        for _ in range(int(math.log2(c)) - 1):
            pw = _bmm(pw, pw)
            tinv = tinv + _bmm(pw, tinv)
        z = _bmm(tinv, jnp.concatenate([at2, _bmm(n_ak, v2)], axis=2))
        x = _bmm(n_qb, z)
        rhat = rt_p + unstack(x[:, :, 0:LANES])
        yhat = unstack(x[:, :, LANES:] + _bmm(n_qk, v2))
        ah = unstack(z[:, :, 0:LANES])
        uh = unstack(z[:, :, LANES:])
        for p in range(npair):
            sl = slice(p * LANES, (p + 1) * LANES)
            rhat_ref[dr, :, sl] = rhat[p].astype(rhat_ref.dtype)
            yhat_ref[dr, :, sl] = yhat[p]
            w_ref[dr, p] = jnp.where(same, _bdot_tn(ah[p], bt_p[p]), 0.0).astype(w_ref.dtype)
            g = _bdot_tn(jnp.concatenate([v_p[p], uh[p]], axis=0), jnp.concatenate([kt_p[p], bt_p[p]], axis=0))
            g_ref[dr, p] = jnp.where(same, g, 0.0)


def _rw_chunks(r, k, v, wl, al, k_k, k_a, r_k):
    b, t, d = r.shape
    c = RW_CHUNK
    nc = t // c
    npair = d // LANES
    one = pl.BlockSpec((None, c, d), lambda b_, ch: (b_, ch, 0))
    wide = pl.BlockSpec((None, c, 2 * d), lambda b_, ch: (b_, ch, 0))
    two = pl.BlockSpec((2, None, c, d), lambda b_, ch: (0, b_, ch, 0))
    mats = pl.BlockSpec((2, None, None, npair, LANES, LANES), lambda b_, ch: (0, b_, ch, 0, 0, 0))
    gam = pl.BlockSpec((2, None, None, 1, d), lambda b_, ch: (0, b_, ch, 0, 0))
    vec = pl.BlockSpec((1, d), lambda b_, ch: (0, 0))
    return pl.pallas_call(
        _rw_chunk_body,
        out_shape=(jax.ShapeDtypeStruct((2, b, t, d), BF16), jax.ShapeDtypeStruct((2, b, t, d), F32),
                   jax.ShapeDtypeStruct((2, b, nc, npair, LANES, LANES), BF16),
                   jax.ShapeDtypeStruct((2, b, nc, npair, LANES, LANES), F32),
                   jax.ShapeDtypeStruct((2, b, nc, 1, d), F32),
                   jax.ShapeDtypeStruct((b, t, d), F32)),
        grid=(b, nc),
        in_specs=[one, one, one, wide, wide, vec, vec, vec],
        out_specs=(two, two, mats, mats, gam, one),
        compiler_params=_params(("parallel", "parallel")),
        name="rw_chunks",
    )(r, k, v, wl, al, k_k.reshape(1, d), k_a.reshape(1, d), r_k.reshape(1, d))


def _rw_state_body(rhat_ref, yhat_ref, w_ref, g_ref, gam_ref, y_ref, s_ref):
    @pl.when(pl.program_id(2) == 0)
    def _():
        s_ref[...] = jnp.zeros_like(s_ref)

    s = s_ref[...]
    sb = s.astype(BF16)
    y = _bmm_nt(_lane_pairs(rhat_ref[...]), sb)
    for p in range(s.shape[0]):
        sl = slice(p * LANES, (p + 1) * LANES)
        y_ref[:, sl] = y[p] + yhat_ref[:, sl]
    gam = _lane_pairs(gam_ref[...])
    s_ref[...] = (s + _bmm(sb, w_ref[...]) + g_ref[...]) * gam


def _rw_state(rhat, yhat, wm, gm, gam, *, n_ctx_rows):
    _, b, t, d = rhat.shape
    c = RW_CHUNK
    nc = t // c
    npair = d // LANES
    tc = functools.partial(_scan_chunk, n_ctx=n_ctx_rows // c, n_all=nc)
    two = pl.BlockSpec((None, None, c, d), lambda b_, dr, ch: (dr, b_, tc(dr, ch), 0))
    mats = pl.BlockSpec((None, None, None, npair, LANES, LANES), lambda b_, dr, ch: (dr, b_, tc(dr, ch), 0, 0, 0))
    gsp = pl.BlockSpec((None, None, None, 1, d), lambda b_, dr, ch: (dr, b_, tc(dr, ch), 0, 0))
    return pl.pallas_call(
        _rw_state_body,
        out_shape=jax.ShapeDtypeStruct((2, b, t, d), F32),
        grid=(b, 2, nc),
        in_specs=[two, two, mats, mats, gsp],
        out_specs=two,
        scratch_shapes=[pltpu.VMEM((npair, LANES, LANES), F32)],
        compiler_params=_params(("parallel", "parallel", "arbitrary")),
        name="rw_state",
    )(rhat, yhat, wm, gm, gam)


def _rw_out_body(y0_ref, y1_ref, bonus_ref, g_ref, lg_ref, lb_ref, o_ref):
    y = y0_ref[...] + y1_ref[...]
    mu = _head_sum(y) * (1.0 / HEAD)
    yc = y - mu
    var = _head_sum(yc * yc) * (1.0 / HEAD)
    yn = yc * lax.rsqrt(var + RW_LN_EPS) * lg_ref[...] + lb_ref[...]
    o_ref[...] = ((yn + bonus_ref[...]) * g_ref[...]).astype(o_ref.dtype)


def _rw_out(y, bonus, g, ln_g, ln_b):
    b, t, d = bonus.shape
    return pl.pallas_call(
        _rw_out_body,
        out_shape=jax.ShapeDtypeStruct((b, t, d), BF16),
        grid=(b, t // TM),
        in_specs=[_stream2(d, 0), _stream2(d, 1), _stream(d), _stream(d), _vec(d), _vec(d)],
        out_specs=_stream(d),
        compiler_params=_params(("parallel", "parallel")),
        name="rw_out",
    )(y, y, bonus, g, ln_g.reshape(1, d), ln_b.reshape(1, d))


def _rwkv_layer(h, w, *, n_ctx_rows):
    b, t, d = h.shape
    m = b * t
    xr, xw, xk, xv, xa, xg = [a.reshape(m, d) for a in _rw_mix(h, w["mix"], nct=n_ctx_rows // TM)]
    r = _matmul(xr, w["r_w"]).reshape(b, t, d)
    k = _matmul(xk, w["k_w"]).reshape(b, t, d)
    v = _matmul(xv, w["v_w"]).reshape(b, t, d)
    g = _lora(xg, w["g1"], w["g2"], jnp.zeros((1, d), F32), "sigmoid").reshape(b, t, d)
    wl = _lora(xw, w["w1"], w["w2"], w["w0"], "tanh").reshape(b, t, 2 * d)
    al = _lora(xa, w["a1"], w["a2"], w["a0"], "none").reshape(b, t, 2 * d)
    rhat, yhat, wm, gm, gam, bonus = _rw_chunks(r, k, v, wl, al, w["k_k"], w["k_a"], w["r_k"])
    y = _rw_state(rhat, yhat, wm, gm, gam, n_ctx_rows=n_ctx_rows)
    o = _rw_out(y, bonus, g, w["ln_g"], w["ln_b"])
    return _matmul(o.reshape(m, d), w["out_w"]).reshape(b, t, d)


def _pool_body(h_ref, hp_ref, hn_ref, w_ref, sc_ref, o_ref, *, nt, nct):
    i = pl.program_id(1)
    first, last = _seq_edges(i, nt, nct)
    h = h_ref[...]
    halo = POOL_HALO
    ext = jnp.concatenate([jnp.where(first, 0.0, hp_ref[...]), h, jnp.where(last, 0.0, hn_ref[...])], axis=0)
    seq_start = jnp.where(i < nct, 0, nct)
    seq_len = jnp.where(i < nct, nct, nt - nct) * TM
    gw = h.shape[1] // len(POOL_WINDOWS)
    pos = (i - seq_start) * TM + lax.broadcasted_iota(jnp.int32, (TM, gw), 0)
    tr = lax.broadcasted_iota(jnp.int32, (TM, TM + 2 * halo), 0)
    er = lax.broadcasted_iota(jnp.int32, (TM, TM + 2 * halo), 1)
    for gi, win in enumerate(POOL_WINDOWS):
        half = win // 2
        band = jnp.logical_and(er >= tr + halo - half, er < tr + halo + half)
        cols = slice(gi * gw, (gi + 1) * gw)
        wsum = _dot01_left(band, ext[:, cols])
        cnt = (jnp.minimum(pos + half, seq_len) - jnp.maximum(pos - half, 0)).astype(F32)
        pooled = wsum / cnt - h[:, cols]
        o_ref[:, cols] = _bdot(pooled, w_ref[gi]) * sc_ref[:, cols]


def _pool_layer(h, pl_w, scale, *, nct):
    b, t, d = h.shape
    nt = t // TM
    ng, gw, _ = pl_w.shape
    return pl.pallas_call(
        functools.partial(_pool_body, nt=nt, nct=nct),
        out_shape=jax.ShapeDtypeStruct((b, t, d), F32),
        grid=(b, nt),
        in_specs=[_stream(d), _halo(d, POOL_HALO, t, -1), _halo(d, POOL_HALO, t, +1),
                  pl.BlockSpec((ng, gw, gw), lambda b_, i: (0, 0, 0)), _vec(d)],
        out_specs=_stream(d),
        compiler_params=_params(("parallel", "parallel")),
        name="pool_mixer",
    )(h, h, h, pl_w, scale.reshape(1, d))


def _at_prep_body(qkv_ref, qg_ref, kg_ref, cos_ref, sin_ref, q_ref, k_ref, v_ref, *, scale):
    nq = q_ref.shape[1]
    nk = AT_KV_HEADS * HEAD
    cos = cos_ref[...]
    sin = sin_ref[...]
    lane = lax.broadcasted_iota(jnp.int32, cos.shape, 1)
    up = (lane & 31) < 16
    lo = lane < HEAD

    def norm_rope(x, g):
        xn = x * lax.rsqrt(_head_sum(x * x) * (1.0 / HEAD) + NORM_EPS) * g
        swapped = jnp.where(up, pltpu.roll(xn, LANES - 16, axis=1), pltpu.roll(xn, 16, axis=1))
        return xn * cos + swapped * sin

    for s in range(nq // LANES):
        sl = slice(s * LANES, (s + 1) * LANES)
        q_ref[:, sl] = (norm_rope(qkv_ref[:, sl], qg_ref[...]) * scale).astype(q_ref.dtype)
    for s in range(nk // LANES):
        kslab = norm_rope(qkv_ref[:, nq + s * LANES:nq + (s + 1) * LANES], kg_ref[...])
        vslab = qkv_ref[:, nq + nk + s * LANES:nq + nk + (s + 1) * LANES]
        for slab, ref in ((kslab, k_ref), (vslab, v_ref)):
            rolled = pltpu.roll(slab, HEAD, axis=1)
            ref[2 * s] = jnp.where(lo, slab, rolled).astype(ref.dtype)
            ref[2 * s + 1] = jnp.where(lo, rolled, slab).astype(ref.dtype)


def _at_prep(qkv, q_g, k_g, cos, sin):
    b, t, _ = qkv.shape
    nq = AT_HEADS * HEAD
    width = qkv.shape[2]
    kvs = jax.ShapeDtypeStruct((b, AT_KV_HEADS, t, LANES), BF16)
    kv_spec = pl.BlockSpec((None, AT_KV_HEADS, TM, LANES), lambda b_, i: (b_, 0, i, 0))
    tab = pl.BlockSpec((TM, LANES), lambda b_, i: (i, 0))
    tile2 = lambda g: jnp.tile(g.reshape(1, HEAD), (1, LANES // HEAD))
    return pl.pallas_call(
        functools.partial(_at_prep_body, scale=HEAD ** -0.5 * math.log2(math.e)),
        out_shape=(jax.ShapeDtypeStruct((b, t, nq), BF16), kvs, kvs),
        grid=(b, t // TM),
        in_specs=[_stream(width), _vec(LANES), _vec(LANES), tab, tab],
        out_specs=(_stream(nq), kv_spec, kv_spec),
        compiler_params=_params(("parallel", "parallel")),
        name="at_prep",
    )(qkv, tile2(q_g), tile2(k_g), cos, sin)


def _flash_body(q_ref, k_ref, v_ref, o_ref, sa_ref, sb_ref, *, tk):
    tq = q_ref.shape[0]
    n = k_ref.shape[0] // tk
    lane = lax.broadcasted_iota(jnp.int32, (tq, LANES), 1)
    lo = lane < HEAD
    q = q_ref[...]
    zero = jnp.zeros_like(q)
    qs = jnp.concatenate([jnp.where(lo, q, zero), jnp.where(lo, zero, q)], axis=0)

    def scores(j, dst_ref):
        start = pl.multiple_of(j * tk, tk)
        dst_ref[...] = _bdot_nt(qs, k_ref[pl.ds(start, tk), :])

    def absorb(j, src_ref, carry):
        m, l, acc = carry
        start = pl.multiple_of(j * tk, tk)
        s = src_ref[...]
        m_new = jnp.maximum(m, jnp.max(s, axis=-1, keepdims=True))
        alpha = jnp.exp2(m - m_new)
        p = jnp.exp2(s - m_new)
        l = alpha * l + jnp.sum(p, axis=-1, keepdims=True)
        acc = alpha * acc + _bdot(p, v_ref[pl.ds(start, tk), :])
        return m_new, l, acc

    def two_chunks(i, carry):
        j = 2 * i
        scores(j + 1, sb_ref)
        carry = absorb(j, sa_ref, carry)
        scores(j + 2, sa_ref)
        return absorb(j + 1, sb_ref, carry)

    carry = (jnp.full((2 * tq, 1), -jnp.inf, F32), jnp.zeros((2 * tq, 1), F32), jnp.zeros((2 * tq, LANES), F32))
    scores(0, sa_ref)
    carry = lax.fori_loop(0, (n - 1) // 2, two_chunks, carry)
    if n % 2 == 0:
        scores(n - 1, sb_ref)
        carry = absorb(n - 2, sa_ref, carry)
        carry = absorb(n - 1, sb_ref, carry)
    else:
        carry = absorb(n - 1, sa_ref, carry)
    _, l, acc = carry
    o2 = acc / l
    o_ref[...] = jnp.where(lo, o2[0:tq], o2[tq:2 * tq]).astype(o_ref.dtype)


def _flash(q, k2, v2, *, n_ctx_rows):
    b, t, nq = q.shape
    tq = TM
    tl = t - n_ctx_rows
    off = n_ctx_rows // tq
    tk = _pick(t, (768, 512, 256, 128))
    npair = nq // LANES
    hp = AT_HEADS // AT_KV_HEADS // 2
    kv = pl.BlockSpec((None, None, t, LANES), lambda b_, p, i: (b_, p // hp, 0, 0))
    return pl.pallas_call(
        functools.partial(_flash_body, tk=tk),
        out_shape=jax.ShapeDtypeStruct((b, tl, nq), BF16),
        grid=(b, npair, tl // tq),
        in_specs=[pl.BlockSpec((None, tq, LANES), lambda b_, p, i: (b_, i + off, p)), kv, kv],
        out_specs=pl.BlockSpec((None, tq, LANES), lambda b_, p, i: (b_, i, p)),
        scratch_shapes=[pltpu.VMEM((2 * tq, tk), F32), pltpu.VMEM((2 * tq, tk), F32)],
        compiler_params=_params(("parallel", "parallel", "parallel")),
        name="flash_gqa",
    )(q, k2, v2)


def _rope_tables(n_ctx_rows, seq):
    quarter = HEAD // 4
    inv = ROPE_THETA ** (-jnp.arange(quarter, dtype=F32) / quarter)
    rows = jnp.repeat(jnp.arange(seq // GRID_W, dtype=jnp.int32), GRID_W).astype(F32)
    cols = (jnp.arange(seq, dtype=jnp.int32) % GRID_W).astype(F32)
    ar = rows[:, None] * inv
    ac = cols[:, None] * inv
    cos = jnp.concatenate([jnp.cos(ar), jnp.cos(ar), jnp.cos(ac), jnp.cos(ac)], axis=1)
    sin = jnp.concatenate([-jnp.sin(ar), jnp.sin(ar), -jnp.sin(ac), jnp.sin(ac)], axis=1)
    cos = jnp.concatenate([jnp.ones((n_ctx_rows, HEAD), F32), cos], axis=0)
    sin = jnp.concatenate([jnp.zeros((n_ctx_rows, HEAD), F32), sin], axis=0)
    return jnp.tile(cos, (1, LANES // HEAD)), jnp.tile(sin, (1, LANES // HEAD))


def _attn_layer(h, w, *, n_ctx_rows):
    b, t, d = h.shape
    qkv = _matmul(h.reshape(b * t, d), w["qkv_w"]).reshape(b, t, -1)
    cos, sin = _rope_tables(n_ctx_rows, t - n_ctx_rows)
    q, k2, v2 = _at_prep(qkv, w["q_g"], w["k_g"], cos, sin)
    o = _flash(q, k2, v2, n_ctx_rows=n_ctx_rows)
    tl = t - n_ctx_rows
    return _matmul(o.reshape(b * tl, -1), w["out_w"]).reshape(b, tl, d)


def _ffn(h2, w_in, w_out):
    return _matmul(_swiglu_in(h2, w_in), w_out)


def _moe_body(h_ref, g_ref, wg_ref, wu_ref, wo_ref, o_ref):
    e = pl.program_id(1)

    @pl.when(jnp.logical_and(e == 0, pl.program_id(2) == 0))
    def _():
        o_ref[...] = jnp.zeros_like(o_ref)

    h = h_ref[...]
    act = _silu(_bdot(h, wg_ref[...])) * _bdot(h, wu_ref[...])
    gates = g_ref[...]
    lane = lax.broadcasted_iota(jnp.int32, gates.shape, 1)
    gate = jnp.sum(jnp.where(lane == e, gates, 0.0), axis=-1, keepdims=True)
    o_ref[...] += gate * _bdot(act, wo_ref[...])


def _moe(h2, gates, w_in, w_out):
    m, d = h2.shape
    ne, _, f2 = w_in.shape
    f = f2 // 2
    tm = _pick(m, (1024, 768, 512, 256))
    fk = _pick(f, (512, 256, 128))
    nf = f // fk
    return pl.pallas_call(
        _moe_body,
        out_shape=jax.ShapeDtypeStruct((m, d), F32),
        grid=(m // tm, ne, nf),
        in_specs=[pl.BlockSpec((tm, d), lambda i, e, j: (i, 0)),
                  pl.BlockSpec((tm, LANES), lambda i, e, j: (i, 0)),
                  pl.BlockSpec((None, d, fk), lambda i, e, j: (e, 0, j)),
                  pl.BlockSpec((None, d, fk), lambda i, e, j: (e, 0, j + nf)),
                  pl.BlockSpec((None, fk, d), lambda i, e, j: (e, j, 0))],
        out_specs=pl.BlockSpec((tm, d), lambda i, e, j: (i, 0)),
        compiler_params=_params(("parallel", "arbitrary", "arbitrary")),
        name="moe_dense",
    )(h2, gates, w_in, w_in, w_out)


def kernel(x, c, ctx, c_ctx, mod_w, mod_b, norm1_g, norm2_g, final_g, mb_in_w, mb_conv_w, mb_conv_b, mb_dt_bias, mb_a_log, mb_d, mb_norm_g, mb_out_w, rw_mix, rw_rkv_w, rw_w0, rw_w1, rw_w2, rw_a0, rw_a1, rw_a2, rw_g1, rw_g2, rw_k_k, rw_k_a, rw_r_k, rw_ln_g, rw_ln_b, rw_out_w, pl_w, pl_scale, at_qkv_w, at_q_g, at_k_g, at_out_w, ff_in_w, ff_out_w, moe_router_w, moe_in_w, moe_out_w):
    b, seq, d = x.shape
    n_ctx = ctx.shape[1]
    depth = mod_w.shape[0]
    t = n_ctx + seq
    nct = n_ctx // TM
    assert depth == 4 and n_ctx % TM == 0 and seq % TM == 0 and b + 1 <= 8
    bf = lambda a: a.astype(BF16)

    svec = jnp.concatenate([c, c_ctx[None, :], jnp.zeros((8 - b - 1, d), F32)], axis=0)
    mods = _mods(svec, bf(mod_w), mod_b)
    mv = jnp.pad(mods[:, :b + 1].reshape(depth, b + 1, 6, d), ((0, 0), (0, 0), (0, 2), (0, 0)))

    xs = jnp.concatenate([ctx, x], axis=1)
    m = b * t
    resid = functools.partial(_resid_norm, nct=nct, x_off=0)

    di = MB_HEADS * HEAD
    xbc = di + 2 * MB_GROUPS * MB_STATE
    mamba_w = dict(in_w=bf(mb_in_w[0]),
                   in_w_dt=bf(jnp.pad(mb_in_w[0][:, di + xbc:], ((0, 0), (0, LANES - 2 * MB_HEADS)))),
                   conv_w=mb_conv_w[0], conv_b=mb_conv_b[0], dt_bias=mb_dt_bias[0], a_log=mb_a_log[0],
                   dvec=jnp.repeat(mb_d[0], HEAD).reshape(1, di), norm_g=mb_norm_g[0], out_w=bf(mb_out_w[0]))
    h = _norm_mod(xs, norm1_g[0], mv, 0, nct, BF16)
    y = _mamba_layer(h, mamba_w, n_ctx_rows=n_ctx)
    xs, h2 = resid(xs, y, mv, norm2_g[0], gate_layer=0, jg=2, mod_layer=0, js=3, jc=4, out_dtype=BF16)
    f = _ffn(h2.reshape(m, d), bf(ff_in_w[0]), bf(ff_out_w[0])).reshape(b, t, d)
    xs, h = resid(xs, f, mv, norm1_g[1], gate_layer=0, jg=5, mod_layer=1, js=0, jc=1, out_dtype=F32)

    blockdiag = lambda u: jnp.concatenate(
        [jnp.concatenate([u[0], jnp.zeros_like(u[0])], axis=1),
         jnp.concatenate([jnp.zeros_like(u[1]), u[1]], axis=1)], axis=0)
    rwkv_w = dict(mix=rw_mix[0], r_w=bf(rw_rkv_w[0, 0]), k_w=bf(rw_rkv_w[0, 1]), v_w=bf(rw_rkv_w[0, 2]),
                  g1=bf(rw_g1[0]), g2=bf(rw_g2[0]),
                  w1=bf(jnp.concatenate([rw_w1[0, 0], rw_w1[0, 1]], axis=1)), w2=bf(blockdiag(rw_w2[0])),
                  w0=rw_w0[0].reshape(1, 2 * d),
                  a1=bf(jnp.concatenate([rw_a1[0, 0], rw_a1[0, 1]], axis=1)), a2=bf(blockdiag(rw_a2[0])),
                  a0=rw_a0[0].reshape(1, 2 * d),
                  k_k=rw_k_k[0], k_a=rw_k_a[0], r_k=rw_r_k[0].reshape(d), ln_g=rw_ln_g[0], ln_b=rw_ln_b[0],
                  out_w=bf(rw_out_w[0]))
    y = _rwkv_layer(h, rwkv_w, n_ctx_rows=n_ctx)
    xs, h2, gates = _resid_norm_router(xs, y, mv, norm2_g[1], moe_router_w[0], gate_layer=1, jg=2,
                                       mod_layer=1, js=3, jc=4, nct=nct, x_off=0)
    f = _moe(h2.reshape(m, d), gates.reshape(m, LANES), bf(moe_in_w[0]), bf(moe_out_w[0])).reshape(b, t, d)
    xs, h = resid(xs, f, mv, norm1_g[2], gate_layer=1, jg=5, mod_layer=2, js=0, jc=1, out_dtype=F32)

    y = _pool_layer(h, bf(pl_w[0]), pl_scale[0], nct=nct)
    xs, h2 = resid(xs, y, mv, norm2_g[2], gate_layer=2, jg=2, mod_layer=2, js=3, jc=4, out_dtype=BF16)
    f = _ffn(h2.reshape(m, d), bf(ff_in_w[1]), bf(ff_out_w[1])).reshape(b, t, d)
    xs, h = resid(xs, f, mv, norm1_g[3], gate_layer=2, jg=5, mod_layer=3, js=0, jc=1, out_dtype=BF16)

    attn_w = dict(qkv_w=bf(at_qkv_w[0]), q_g=at_q_g[0], k_g=at_k_g[0], out_w=bf(at_out_w[0]))
    y = _attn_layer(h, attn_w, n_ctx_rows=n_ctx)
    xl, h2, gates = _resid_norm_router(xs, y, mv, norm2_g[3], moe_router_w[1], gate_layer=3, jg=2,
                                       mod_layer=3, js=3, jc=4, nct=0, x_off=nct)
    ml = b * seq
    f = _moe(h2.reshape(ml, d), gates.reshape(ml, LANES), bf(moe_in_w[1]), bf(moe_out_w[1])).reshape(b, seq, d)
    return _resid_final(xl, f, mv, final_g, gate_layer=3, jg=5)
```

```python
import functools
import math

import jax
import jax.numpy as jnp
from jax import lax
from jax.experimental import pallas as pl
from jax.experimental.pallas import tpu as pltpu

F32 = jnp.float32
BF16 = jnp.bfloat16

NORM_EPS = 1e-6
TM = 256
LANES = 128
HEAD = 64
VMEM_LIMIT = 48 * 1024 * 1024

MB_HEADS = 32
MB_GROUPS = 4
MB_STATE = 128
MB_CHUNK = 128
MB_CONV = 5
RW_CHUNK = 64
RW_DECAY_SCALE = 0.606531
RW_LN_EPS = 64e-5
POOL_WINDOWS = (2, 4, 8, 16)
POOL_HALO = 16
AT_HEADS = 16
AT_KV_HEADS = 4
ROPE_THETA = 10000.0
GRID_W = 64
N_EXPERTS = 8


def _params(sem, vmem=VMEM_LIMIT):
    return pltpu.CompilerParams(dimension_semantics=sem, vmem_limit_bytes=vmem)


def _bdot(a, b):
    return jnp.dot(a.astype(BF16), b.astype(BF16), preferred_element_type=F32)


def _bdot_nt(a, b):
    return lax.dot_general(a.astype(BF16), b.astype(BF16), (((1,), (1,)), ((), ())),
                           preferred_element_type=F32)


def _bdot_tn(a, b):
    return lax.dot_general(a.astype(BF16), b.astype(BF16), (((0,), (0,)), ((), ())),
                           preferred_element_type=F32)


def _split3(x):
    p0 = x.astype(BF16)
    r1 = x - p0.astype(F32)
    p1 = r1.astype(BF16)
    p2 = (r1 - p1.astype(F32)).astype(BF16)
    return p0, p1, p2


def _as01(mask):
    return jnp.where(mask, 1.0, 0.0).astype(BF16)


def _dot01_left(sel, x):
    sel = _as01(sel)
    p0, p1, p2 = _split3(x)
    return (jnp.dot(sel, p0, preferred_element_type=F32)
            + jnp.dot(sel, p1, preferred_element_type=F32)
            + jnp.dot(sel, p2, preferred_element_type=F32))


def _dot01_right(x, sel):
    sel = _as01(sel)
    p0, p1, p2 = _split3(x)
    return (jnp.dot(p0, sel, preferred_element_type=F32)
            + jnp.dot(p1, sel, preferred_element_type=F32)
            + jnp.dot(p2, sel, preferred_element_type=F32))


def _head_sum(x):
    r = lax.broadcasted_iota(jnp.int32, (LANES, LANES), 0) // HEAD
    c = lax.broadcasted_iota(jnp.int32, (LANES, LANES), 1) // HEAD
    ones_bd = r == c
    slabs = [_dot01_right(x[:, s:s + LANES], ones_bd) for s in range(0, x.shape[1], LANES)]
    return slabs[0] if len(slabs) == 1 else jnp.concatenate(slabs, axis=1)


def _sigmoid(x):
    return 1.0 / (1.0 + jnp.exp(-x))


def _silu(x):
    return x * _sigmoid(x)


def _rms(x, g):
    ms = jnp.mean(x * x, axis=-1, keepdims=True)
    return x * lax.rsqrt(ms + NORM_EPS) * g


def _mods_body(s_ref, w_ref, b_ref, o_ref):
    o_ref[...] = _bdot(_silu(s_ref[...]), w_ref[...]) + b_ref[...]


def _mods(svec, mod_w, mod_b):
    depth, d, n6 = mod_w.shape
    tn = 1536
    return pl.pallas_call(
        _mods_body,
        out_shape=jax.ShapeDtypeStruct((depth, 8, n6), F32),
        grid=(depth, n6 // tn),
        in_specs=[pl.BlockSpec((8, d), lambda l, j: (0, 0)),
                  pl.BlockSpec((None, d, tn), lambda l, j: (l, 0, j)),
                  pl.BlockSpec((None, 1, tn), lambda l, j: (l, 0, j))],
        out_specs=pl.BlockSpec((None, 8, tn), lambda l, j: (l, 0, j)),
        compiler_params=_params(("parallel", "parallel")),
        name="mods",
    )(svec, mod_w, mod_b.reshape(depth, 1, n6))


def _stream(width, off=0, rows=TM):
    return pl.BlockSpec((None, rows, width), lambda b, i: (b, i + off, 0))


def _stream2(width, d, off=0):
    return pl.BlockSpec((None, None, TM, width), lambda b, i: (d, b, i + off, 0))


def _vec(width, rows=1):
    return pl.BlockSpec((rows, width), lambda b, i: (0, 0))


def _mv(layer, nct, nb, d):
    return pl.BlockSpec((None, None, 8, d), lambda b, i: (layer, jnp.where(i < nct, nb, b), 0, 0))


def _halo(width, nrows, t_total, side):
    per = TM // nrows
    last = t_total // nrows - 1
    if side < 0:
        return pl.BlockSpec((None, nrows, width), lambda b, i: (b, jnp.maximum(i * per - 1, 0), 0))
    return pl.BlockSpec((None, nrows, width), lambda b, i: (b, jnp.minimum((i + 1) * per, last), 0))


def _seq_edges(i, nt, nct):
    first = jnp.logical_or(i == 0, i == nct)
    last = jnp.logical_or(i == nct - 1, i == nt - 1)
    return first, last


def _shift_rows(x, prev8, next8, o):
    rows = x.shape[0]
    r8 = lax.broadcasted_iota(jnp.int32, (8, x.shape[1]), 0)
    if o < 0:
        k = -o
        s = pltpu.roll(x, k, axis=0)
        top = jnp.where(r8 < k, pltpu.roll(prev8, k, axis=0), s[0:8])
        return jnp.concatenate([top, s[8:]], axis=0)
    s = pltpu.roll(x, rows - o, axis=0)
    bot = jnp.where(r8 >= 8 - o, pltpu.roll(next8, 8 - o, axis=0), s[rows - 8:])
    return jnp.concatenate([s[:rows - 8], bot], axis=0)


def _norm_mod_body(x_ref, g_ref, mv_ref, h_ref, *, js, jc):
    h = _rms(x_ref[...], g_ref[...]) * (1.0 + mv_ref[jc:jc + 1, :]) + mv_ref[js:js + 1, :]
    h_ref[...] = h.astype(h_ref.dtype)


def _norm_mod(x, g, mv, layer, nct, out_dtype):
    b, t, d = x.shape
    nb = b
    return pl.pallas_call(
        functools.partial(_norm_mod_body, js=0, jc=1),
        out_shape=jax.ShapeDtypeStruct((b, t, d), out_dtype),
        grid=(b, t // TM),
        in_specs=[_stream(d), _vec(d), _mv(layer, nct, nb, d)],
        out_specs=_stream(d),
        compiler_params=_params(("parallel", "parallel")),
        name="norm_mod",
    )(x, g.reshape(1, d), mv)


def _resid_norm_body(x_ref, y_ref, mvg_ref, g_ref, mvm_ref, xo_ref, h_ref, *, jg, js, jc):
    xn = x_ref[...] + mvg_ref[jg:jg + 1, :] * y_ref[...].astype(F32)
    xo_ref[...] = xn
    h = _rms(xn, g_ref[...]) * (1.0 + mvm_ref[jc:jc + 1, :]) + mvm_ref[js:js + 1, :]
    h_ref[...] = h.astype(h_ref.dtype)


def _resid_norm(x, y, mv, g, *, gate_layer, jg, mod_layer, js, jc, nct, x_off, out_dtype):
    b, t, d = y.shape
    return pl.pallas_call(
        functools.partial(_resid_norm_body, jg=jg, js=js, jc=jc),
        out_shape=(jax.ShapeDtypeStruct((b, t, d), F32), jax.ShapeDtypeStruct((b, t, d), out_dtype)),
        grid=(b, t // TM),
        in_specs=[_stream(d, x_off), _stream(d), _mv(gate_layer, nct, b, d), _vec(d),
                  _mv(mod_layer, nct, b, d)],
        out_specs=(_stream(d), _stream(d)),
        compiler_params=_params(("parallel", "parallel")),
        name="resid_norm",
    )(x, y, mv, g.reshape(1, d), mv)


def _top2_gates(logits):
    lane = lax.broadcasted_iota(jnp.int32, logits.shape, 1)
    neg = jnp.float32(-jnp.inf)
    lg = jnp.where(lane < N_EXPERTS, logits, neg)
    v1 = jnp.max(lg, axis=-1, keepdims=True)
    i1 = jnp.min(jnp.where(lg == v1, lane, LANES), axis=-1, keepdims=True)
    lg2 = jnp.where(lane == i1, neg, lg)
    v2 = jnp.max(lg2, axis=-1, keepdims=True)
    i2 = jnp.min(jnp.where(lg2 == v2, lane, LANES), axis=-1, keepdims=True)
    e = jnp.exp(v2 - v1)
    w1 = 1.0 / (1.0 + e)
    w2 = e / (1.0 + e)
    gates = jnp.where(lane == i1, w1, 0.0) + jnp.where(lane == i2, w2, 0.0)
    sel = jnp.where(jnp.logical_or(lane == i1, lane == i2), 1.0, 0.0)
    return gates, sel


def _resid_norm_router_body(x_ref, y_ref, mvg_ref, g_ref, mvm_ref, rw_ref, xo_ref, h_ref, gates_ref, sel_ref,
                            *, jg, js, jc):
    xn = x_ref[...] + mvg_ref[jg:jg + 1, :] * y_ref[...].astype(F32)
    xo_ref[...] = xn
    h = _rms(xn, g_ref[...]) * (1.0 + mvm_ref[jc:jc + 1, :]) + mvm_ref[js:js + 1, :]
    h_ref[...] = h.astype(h_ref.dtype)
    h0, h1, h2 = _split3(h)
    w0, w1, w2 = _split3(rw_ref[...])
    dot = lambda a, c: jnp.dot(a, c, preferred_element_type=F32)
    logits = (dot(h0, w0) + (dot(h0, w1) + dot(h1, w0))
              + (dot(h0, w2) + dot(h1, w1) + dot(h2, w0)))
    gates_ref[...], sel_ref[...] = _top2_gates(logits)


def _resid_norm_router(x, y, mv, g, router_w, *, gate_layer, jg, mod_layer, js, jc, nct, x_off):
    b, t, d = y.shape
    rw = jnp.pad(router_w, ((0, 0), (0, LANES - router_w.shape[1])))
    lanes = jax.ShapeDtypeStruct((b, t, LANES), F32)
    return pl.pallas_call(
        functools.partial(_resid_norm_router_body, jg=jg, js=js, jc=jc),
        out_shape=(jax.ShapeDtypeStruct((b, t, d), F32), jax.ShapeDtypeStruct((b, t, d), BF16), lanes, lanes),
        grid=(b, t // TM),
        in_specs=[_stream(d, x_off), _stream(d), _mv(gate_layer, nct, b, d), _vec(d),
                  _mv(mod_layer, nct, b, d), _vec(LANES, d)],
        out_specs=(_stream(d), _stream(d), _stream(LANES), _stream(LANES)),
        compiler_params=_params(("parallel", "parallel")),
        name="resid_norm_router",
    )(x, y, mv, g.reshape(1, d), mv, rw)


def _resid_final_body(x_ref, y_ref, mvg_ref, g_ref, o_ref, *, jg):
    xn = x_ref[...] + mvg_ref[jg:jg + 1, :] * y_ref[...].astype(F32)
    o_ref[...] = _rms(xn, g_ref[...])


def _resid_final(x, y, mv, g, *, gate_layer, jg):
    b, t, d = y.shape
    return pl.pallas_call(
        functools.partial(_resid_final_body, jg=jg),
        out_shape=jax.ShapeDtypeStruct((b, t, d), F32),
        grid=(b, t // TM),
        in_specs=[_stream(d), _stream(d), _mv(gate_layer, 0, b, d), _vec(d)],
        out_specs=_stream(d),
        compiler_params=_params(("parallel", "parallel")),
        name="resid_final",
    )(x, y, mv, g.reshape(1, d))


def _mm_body(x_ref, w_ref, o_ref):
    o_ref[...] = _bdot(x_ref[...], w_ref[...]).astype(o_ref.dtype)


def _pick(n, prefs):
    for p in prefs:
        if n % p == 0:
            return p
    raise ValueError(f"no tile for {n}")


def _matmul(x, w, *, n_cols=None, out_dtype=F32, tm=None, tn=None):
    m, k = x.shape
    n = n_cols or w.shape[1]
    tm = tm or _pick(m, (1024, 768, 512, 256))
    tn = tn or _pick(n, (1024, 768, 512, 256, 128))
    return pl.pallas_call(
        _mm_body,
        out_shape=jax.ShapeDtypeStruct((m, n), out_dtype),
        grid=(n // tn, m // tm),
        in_specs=[pl.BlockSpec((tm, k), lambda j, i: (i, 0)),
                  pl.BlockSpec((k, tn), lambda j, i: (0, j))],
        out_specs=pl.BlockSpec((tm, tn), lambda j, i: (i, j)),
        compiler_params=_params(("parallel", "parallel")),
        name="matmul",
    )(x, w)


def _swiglu_body(x_ref, wg_ref, wu_ref, o_ref):
    x = x_ref[...]
    o_ref[...] = (_silu(_bdot(x, wg_ref[...])) * _bdot(x, wu_ref[...])).astype(o_ref.dtype)


def _swiglu_in(x, w_in, *, tm=None, tn=None):
    m, k = x.shape
    f = w_in.shape[1] // 2
    tm = tm or _pick(m, (768, 512, 256))
    tn = tn or _pick(f, (1408, 896, 512, 256, 128))
    nf = f // tn
    return pl.pallas_call(
        _swiglu_body,
        out_shape=jax.ShapeDtypeStruct((m, f), BF16),
        grid=(nf, m // tm),
        in_specs=[pl.BlockSpec((tm, k), lambda j, i: (i, 0)),
                  pl.BlockSpec((k, tn), lambda j, i: (0, j)),
                  pl.BlockSpec((k, tn), lambda j, i: (0, j + nf))],
        out_specs=pl.BlockSpec((tm, tn), lambda j, i: (i, j)),
        compiler_params=_params(("parallel", "parallel")),
        name="swiglu_in",
    )(x, w_in, w_in)


def _lora_body(x_ref, a_ref, b_ref, bias_ref, o_ref, *, act):
    t = _bdot(x_ref[...], a_ref[...])
    if act == "tanh":
        t = jnp.tanh(t)
    elif act == "sigmoid":
        t = _sigmoid(t)
    o_ref[...] = _bdot(t, b_ref[...]) + bias_ref[...]


def _lora(x, a, bm, bias, act):
    m, k = x.shape
    r = a.shape[1]
    n = bm.shape[1]
    tm = _pick(m, (512, 256))
    return pl.pallas_call(
        functools.partial(_lora_body, act=act),
        out_shape=jax.ShapeDtypeStruct((m, n), F32),
        grid=(m // tm,),
        in_specs=[pl.BlockSpec((tm, k), lambda i: (i, 0)),
                  pl.BlockSpec((k, r), lambda i: (0, 0)),
                  pl.BlockSpec((r, n), lambda i: (0, 0)),
                  pl.BlockSpec((1, n), lambda i: (0, 0))],
        out_specs=pl.BlockSpec((tm, n), lambda i: (i, 0)),
        compiler_params=_params(("parallel",)),
        name="lora",
    )(x, a, bm, bias)


def _mb_conv_body(x_ref, xp_ref, xn_ref, w_ref, b_ref, o_ref, *, nt, nct):
    i = pl.program_id(1)
    first, last = _seq_edges(i, nt, nct)
    x = x_ref[...]
    prev8 = jnp.where(first, 0.0, xp_ref[...])
    next8 = jnp.where(last, 0.0, xn_ref[...])
    pad = (MB_CONV - 1) // 2
    acc = x * w_ref[pad:pad + 1, :] + b_ref[...]
    for o in range(-pad, pad + 1):
        if o != 0:
            acc = acc + _shift_rows(x, prev8, next8, o) * w_ref[pad + o:pad + o + 1, :]
    o_ref[...] = _silu(acc)


def _mb_conv(zx, conv_w, conv_b, *, col0, nct):
    b, t, _ = zx.shape
    c = conv_w.shape[1]
    tc = 512
    cb = col0 // tc
    nt = t // TM
    per = TM // 8
    lastb = t // 8 - 1
    w8 = jnp.pad(conv_w, ((0, 8 - conv_w.shape[0]), (0, 0)))
    return pl.pallas_call(
        functools.partial(_mb_conv_body, nt=nt, nct=nct),
        out_shape=jax.ShapeDtypeStruct((b, t, c), F32),
        grid=(b, nt, c // tc),
        in_specs=[pl.BlockSpec((None, TM, tc), lambda b_, i, j: (b_, i, cb + j)),
                  pl.BlockSpec((None, 8, tc), lambda b_, i, j: (b_, jnp.maximum(i * per - 1, 0), cb + j)),
                  pl.BlockSpec((None, 8, tc),
                               lambda b_, i, j: (b_, jnp.minimum((i + 1) * per, lastb), cb + j)),
                  pl.BlockSpec((8, tc), lambda b_, i, j: (0, j)),
                  pl.BlockSpec((1, tc), lambda b_, i, j: (0, j))],
        out_specs=pl.BlockSpec((None, TM, tc), lambda b_, i, j: (b_, i, j)),
        compiler_params=_params(("parallel", "parallel", "parallel")),
        name="mb_conv",
    )(zx, zx, zx, w8, conv_b.reshape(1, c))


def _softplus(x):
    return jnp.maximum(x, 0.0) + jnp.log(1.0 + jnp.exp(-jnp.abs(x)))


def _lane_bcast(col, width):
    return jnp.broadcast_to(col, (col.shape[0], width))


def _ssd_body(x_ref, b_ref, c_ref, dtc_ref, dtr_ref, bc_ref, br_ref, ac_ref, ar_ref, y_ref, st_ref):
    d = pl.program_id(1)
    q = MB_CHUNK
    hpg = MB_HEADS // MB_GROUPS

    @pl.when(pl.program_id(2) == 0)
    def _():
        st_ref[...] = jnp.zeros_like(st_ref)

    sgn = 1 - 2 * d
    dt_c = _softplus(dtc_ref[...] + bc_ref[...])
    dt_r = _softplus(dtr_ref[...] + br_ref[...])
    dta_c = dt_c * (-jnp.exp(ac_ref[...]))
    dta_r = dt_r * (-jnp.exp(ar_ref[...]))
    ii = lax.broadcasted_iota(jnp.int32, (q, q), 0)
    jj = lax.broadcasted_iota(jnp.int32, (q, q), 1)
    mask = (ii - jj) * sgn >= 0
    cum_c = _dot01_left(mask, dta_c)
    cum_r = _dot01_right(dta_r, (jj - ii) * sgn >= 0)
    tot_c = jnp.sum(dta_c, axis=0, keepdims=True)
    ecum_c = jnp.exp(cum_c)
    f_c = jnp.exp(tot_c - cum_c) * dt_c
    etot_c = jnp.exp(tot_c)
    lane = lax.broadcasted_iota(jnp.int32, (q, LANES), 1)
    lo = lane < HEAD
    lane1 = lax.broadcasted_iota(jnp.int32, (1, LANES), 1)

    for g in range(MB_GROUPS):
        bg = b_ref[:, g * MB_STATE:(g + 1) * MB_STATE].astype(BF16)
        cg = c_ref[:, g * MB_STATE:(g + 1) * MB_STATE].astype(BF16)
        gmat = _bdot_nt(cg, bg)
        for pq in range(hpg // 2):
            p = g * (hpg // 2) + pq
            h0 = 2 * p
            xp = x_ref[:, p * LANES:(p + 1) * LANES]
            xpb = xp.astype(BF16)
            ys = []
            for h in (h0, h0 + 1):
                seg = _lane_bcast(cum_c[:, h:h + 1], q) - cum_r[h:h + 1, :]
                wmat = gmat * (jnp.exp(jnp.where(mask, seg, -jnp.inf)) * dt_r[h:h + 1, :])
                ys.append(_bdot(wmat, xpb))
            y_intra = jnp.where(lo, ys[0], ys[1])
            pair = lambda a: jnp.where(lo, _lane_bcast(a[:, h0:h0 + 1], LANES),
                                       _lane_bcast(a[:, h0 + 1:h0 + 2], LANES))
            st = st_ref[p]
            y_ref[:, p * LANES:(p + 1) * LANES] = y_intra + _bdot(cg, st) * pair(ecum_c)
            upd = _bdot_tn(bg, xp * pair(f_c))
            et = jnp.where(lane1 < HEAD, _lane_bcast(etot_c[:, h0:h0 + 1], LANES),
                           _lane_bcast(etot_c[:, h0 + 1:h0 + 2], LANES))
            st_ref[p] = st * et + upd


def _scan_chunk(d, c, n_ctx, n_all):
    rev = jnp.where(c < n_ctx, n_ctx - 1 - c, n_all - 1 - (c - n_ctx))
    return jnp.where(d == 0, c, rev)


def _ssd(xa, dtc, dtr, dt_bias, a_log, *, n_ctx_rows):
    b, t, _ = xa.shape
    q = MB_CHUNK
    nh = MB_HEADS
    di = nh * HEAD
    gn = MB_GROUPS * MB_STATE
    nc = t // q
    ncc = n_ctx_rows // q
    tc = functools.partial(_scan_chunk, n_ctx=ncc, n_all=nc)
    bias_c = dt_bias.reshape(2, 1, nh)
    bias_r = dt_bias.reshape(2, nh, 1)
    a_c = a_log.reshape(2, 1, nh)
    a_r = a_log.reshape(2, nh, 1)
    small_c = pl.BlockSpec((None, 1, nh), lambda b_, d, c: (d, 0, 0))
    small_r = pl.BlockSpec((None, nh, 1), lambda b_, d, c: (d, 0, 0))
    return pl.pallas_call(
        _ssd_body,
        out_shape=jax.ShapeDtypeStruct((2, b, t, di), F32),
        grid=(b, 2, nc),
        in_specs=[pl.BlockSpec((None, q, di), lambda b_, d, c: (b_, tc(d, c), 0)),
                  pl.BlockSpec((None, q, gn), lambda b_, d, c: (b_, tc(d, c), di // gn)),
                  pl.BlockSpec((None, q, gn), lambda b_, d, c: (b_, tc(d, c), di // gn + 1)),
                  pl.BlockSpec((None, None, q, nh), lambda b_, d, c: (d, b_, tc(d, c), 0)),
                  pl.BlockSpec((None, None, nh, q), lambda b_, d, c: (d, b_, 0, tc(d, c))),
                  small_c, small_r, small_c, small_r],
        out_specs=pl.BlockSpec((None, None, q, di), lambda b_, d, c: (d, b_, tc(d, c), 0)),
        scratch_shapes=[pltpu.VMEM((nh // 2, MB_STATE, LANES), F32)],
        compiler_params=_params(("parallel", "parallel", "arbitrary")),
        name="ssd_scan",
    )(xa, xa, xa, dtc, dtr, bias_c, bias_r, a_c, a_r)


def _mb_gate_body(y0_ref, y1_ref, xs_ref, z_ref, dv_ref, g_ref, o_ref):
    y = y0_ref[...] + y1_ref[...] + xs_ref[...] * dv_ref[...]
    o_ref[...] = _rms(y * _silu(z_ref[...]), g_ref[...]).astype(o_ref.dtype)


def _mb_gate(y, xa, zx, dvec, norm_g):
    _, b, t, di = y.shape
    return pl.pallas_call(
        _mb_gate_body,
        out_shape=jax.ShapeDtypeStruct((b, t, di), BF16),
        grid=(b, t // TM),
        in_specs=[_stream2(di, 0), _stream2(di, 1), _stream(di), _stream(di), _vec(di), _vec(di)],
        out_specs=_stream(di),
        compiler_params=_params(("parallel", "parallel")),
        name="mb_gate",
    )(y, y, xa, zx, dvec, norm_g.reshape(1, di))


def _mamba_layer(h, w, *, n_ctx_rows):
    b, t, d = h.shape
    m = b * t
    di = MB_HEADS * HEAD
    xbc = di + 2 * MB_GROUPS * MB_STATE
    h2 = h.reshape(m, d)
    zx = _matmul(h2, w["in_w"], n_cols=di + xbc).reshape(b, t, di + xbc)
    dt_raw = _matmul(h2, w["in_w_dt"], tn=LANES).reshape(b, t, LANES)[..., :2 * MB_HEADS]
    dtc = jnp.moveaxis(dt_raw.reshape(b, t, 2, MB_HEADS), 2, 0)
    dtr = jnp.swapaxes(dtc, 2, 3)
    xa = _mb_conv(zx, w["conv_w"], w["conv_b"], col0=di, nct=n_ctx_rows // TM)
    y = _ssd(xa, dtc, dtr, w["dt_bias"], w["a_log"], n_ctx_rows=n_ctx_rows)
    gated = _mb_gate(y, xa, zx, w["dvec"], w["norm_g"])
    return _matmul(gated.reshape(m, di), w["out_w"]).reshape(b, t, d)


def _rw_mix_body(h_ref, hp_ref, hn_ref, mix_ref, *o_refs, nt, nct):
    i = pl.program_id(1)
    first, last = _seq_edges(i, nt, nct)
    h = h_ref[...]
    prow = jnp.where(first, 0.0, hp_ref[7:8, :])
    nrow = jnp.where(last, 0.0, hn_ref[0:1, :])
    row = lax.broadcasted_iota(jnp.int32, h.shape, 0)
    prev = jnp.where(row == 0, prow, pltpu.roll(h, 1, axis=0))
    nxt = jnp.where(row == TM - 1, nrow, pltpu.roll(h, TM - 1, axis=0))
    xx = 0.5 * (prev + nxt) - h
    for j, o_ref in enumerate(o_refs):
        o_ref[...] = (h + xx * mix_ref[j:j + 1, :]).astype(o_ref.dtype)


def _rw_mix(h, mix, *, nct):
    b, t, d = h.shape
    nt = t // TM
    mix8 = jnp.pad(mix, ((0, 8 - mix.shape[0]), (0, 0)))
    return pl.pallas_call(
        functools.partial(_rw_mix_body, nt=nt, nct=nct),
        out_shape=tuple(jax.ShapeDtypeStruct((b, t, d), BF16) for _ in range(6)),
        grid=(b, nt),
        in_specs=[_stream(d), _halo(d, 8, t, -1), _halo(d, 8, t, +1), _vec(d, 8)],
        out_specs=tuple(_stream(d) for _ in range(6)),
        compiler_params=_params(("parallel", "parallel")),
        name="rw_mix",
    )(h, h, h, mix8)


def _bmm(a, b):
    return jnp.einsum('pij,pjk->pik', a.astype(BF16), b.astype(BF16), preferred_element_type=F32)


def _bmm_nt(a, b):
    return jnp.einsum('pik,pjk->pij', a.astype(BF16), b.astype(BF16), preferred_element_type=F32)


def _lane_pairs(x):
    return jnp.stack([x[:, p * LANES:(p + 1) * LANES] for p in range(x.shape[1] // LANES)], axis=0)


def _rw_chunk_body(r_ref, k_ref, v_ref, wl_ref, al_ref, kkk_ref, ka_ref, rk_ref,
                   rhat_ref, yhat_ref, w_ref, g_ref, gam_ref, bonus_ref):
    c = RW_CHUNK
    c2 = 2 * c
    d = r_ref.shape[1]
    r = r_ref[...]
    k = k_ref[...]
    v = v_ref[...]
    kkr = k * kkk_ref[...]
    kk = kkr / jnp.maximum(jnp.sqrt(_head_sum(kkr * kkr)), 1e-12)
    lws, kds, bbs = [], [], []
    for dr in range(2):
        lws.append(-RW_DECAY_SCALE * _sigmoid(wl_ref[:, dr * d:(dr + 1) * d]))
        a = _sigmoid(al_ref[:, dr * d:(dr + 1) * d])
        kds.append(k * (1.0 + (a - 1.0) * ka_ref[...]))
        bbs.append(kk * a)
    bonus_ref[...] = _head_sum(r * rk_ref[...] * (0.5 * (kds[0] + kds[1]))) * v

    ti = lax.broadcasted_iota(jnp.int32, (c, c), 0)
    si = lax.broadcasted_iota(jnp.int32, (c, c), 1)
    ri = lax.broadcasted_iota(jnp.int32, (c2, c2), 0)
    ci = lax.broadcasted_iota(jnp.int32, (c2, c2), 1)
    same = (ri >= c) == (ci >= c)
    dtm = (ri & (c - 1)) - (ci & (c - 1))
    eye = jnp.where(ri == ci, 1.0, 0.0)
    lo = lax.broadcasted_iota(jnp.int32, (c, LANES), 1) < HEAD

    def stack(x):
        return jnp.concatenate([jnp.where(lo, x, 0.0), jnp.where(lo, 0.0, x)], axis=1)

    def unstack(x2):
        return x2[:, 0:c] + x2[:, c:c2]

    v_p = _lane_pairs(v)
    v2 = stack(v_p)
    npair = v_p.shape[0]
    for dr in range(2):
        sgn = 1 - 2 * dr
        lw = lws[dr]
        cum = _dot01_left((ti - si) * sgn >= 0, lw)
        eg = jnp.exp(cum)
        einv = jnp.exp(-cum)
        at_p = _lane_pairs(-kk * jnp.exp(cum - lw))
        rt_p = _lane_pairs(r * eg)
        kt_p = _lane_pairs(kds[dr] * einv)
        bt_p = _lane_pairs(bbs[dr] * einv)
        gam_ref[dr] = eg[c - 1:c, :] if dr == 0 else eg[0:1, :]
        m_strict = jnp.logical_and(same, dtm * sgn > 0)
        m_incl = jnp.logical_and(same, dtm * sgn >= 0)

        at2 = stack(at_p)
        lhs = jnp.concatenate([at2, stack(rt_p)], axis=1)
        rhs = jnp.concatenate([bt_p, bt_p, kt_p, kt_p], axis=1)
        nn = _bmm_nt(lhs, rhs)
        n_ab = jnp.where(m_strict, nn[:, 0:c2, 0:c2], 0.0)
        n_ak = jnp.where(m_strict, nn[:, 0:c2, c2:2 * c2], 0.0)
        n_qb = jnp.where(m_incl, nn[:, c2:2 * c2, 0:c2], 0.0)
        n_qk = jnp.where(m_incl, nn[:, c2:2 * c2, c2:2 * c2], 0.0)
        tinv = eye + n_ab
        pw = n_ab
        for _ in range(int(math.log2(c)) - 1):
            pw = _bmm(pw, pw)
            tinv = tinv + _bmm(pw, tinv)
        z = _bmm(tinv, jnp.concatenate([at2, _bmm(n_ak, v2)], axis=2))
        x = _bmm(n_qb, z)
        rhat = rt_p + unstack(x[:, :, 0:LANES])
        yhat = unstack(x[:, :, LANES:] + _bmm(n_qk, v2))
        ah = unstack(z[:, :, 0:LANES])
        uh = unstack(z[:, :, LANES:])
        for p in range(npair):
            sl = slice(p * LANES, (p + 1) * LANES)
            rhat_ref[dr, :, sl] = rhat[p].astype(rhat_ref.dtype)
            yhat_ref[dr, :, sl] = yhat[p]
            w_ref[dr, p] = jnp.where(same, _bdot_tn(ah[p], bt_p[p]), 0.0).astype(w_ref.dtype)
            g = _bdot_tn(jnp.concatenate([v_p[p], uh[p]], axis=0), jnp.concatenate([kt_p[p], bt_p[p]], axis=0))
            g_ref[dr, p] = jnp.where(same, g, 0.0)


def _rw_chunks(r, k, v, wl, al, k_k, k_a, r_k):
    b, t, d = r.shape
    c = RW_CHUNK
    nc = t // c
    npair = d // LANES
    one = pl.BlockSpec((None, c, d), lambda b_, ch: (b_, ch, 0))
    wide = pl.BlockSpec((None, c, 2 * d), lambda b_, ch: (b_, ch, 0))
    two = pl.BlockSpec((2, None, c, d), lambda b_, ch: (0, b_, ch, 0))
    mats = pl.BlockSpec((2, None, None, npair, LANES, LANES), lambda b_, ch: (0, b_, ch, 0, 0, 0))
    gam = pl.BlockSpec((2, None, None, 1, d), lambda b_, ch: (0, b_, ch, 0, 0))
    vec = pl.BlockSpec((1, d), lambda b_, ch: (0, 0))
    return pl.pallas_call(
        _rw_chunk_body,
        out_shape=(jax.ShapeDtypeStruct((2, b, t, d), BF16), jax.ShapeDtypeStruct((2, b, t, d), F32),
                   jax.ShapeDtypeStruct((2, b, nc, npair, LANES, LANES), BF16),
                   jax.ShapeDtypeStruct((2, b, nc, npair, LANES, LANES), F32),
                   jax.ShapeDtypeStruct((2, b, nc, 1, d), F32),
                   jax.ShapeDtypeStruct((b, t, d), F32)),
        grid=(b, nc),
        in_specs=[one, one, one, wide, wide, vec, vec, vec],
        out_specs=(two, two, mats, mats, gam, one),
        compiler_params=_params(("parallel", "parallel")),
        name="rw_chunks",
    )(r, k, v, wl, al, k_k.reshape(1, d), k_a.reshape(1, d), r_k.reshape(1, d))


def _rw_state_body(rhat_ref, yhat_ref, w_ref, g_ref, gam_ref, y_ref, s_ref):
    @pl.when(pl.program_id(2) == 0)
    def _():
        s_ref[...] = jnp.zeros_like(s_ref)

    s = s_ref[...]
    sb = s.astype(BF16)
    y = _bmm_nt(_lane_pairs(rhat_ref[...]), sb)
    for p in range(s.shape[0]):
        sl = slice(p * LANES, (p + 1) * LANES)
        y_ref[:, sl] = y[p] + yhat_ref[:, sl]
    gam = _lane_pairs(gam_ref[...])
    s_ref[...] = (s + _bmm(sb, w_ref[...]) + g_ref[...]) * gam


def _rw_state(rhat, yhat, wm, gm, gam, *, n_ctx_rows):
    _, b, t, d = rhat.shape
    c = RW_CHUNK
    nc = t // c
    npair = d // LANES
    tc = functools.partial(_scan_chunk, n_ctx=n_ctx_rows // c, n_all=nc)
    two = pl.BlockSpec((None, None, c, d), lambda b_, dr, ch: (dr, b_, tc(dr, ch), 0))
    mats = pl.BlockSpec((None, None, None, npair, LANES, LANES), lambda b_, dr, ch: (dr, b_, tc(dr, ch), 0, 0, 0))
    gsp = pl.BlockSpec((None, None, None, 1, d), lambda b_, dr, ch: (dr, b_, tc(dr, ch), 0, 0))
    return pl.pallas_call(
        _rw_state_body,
        out_shape=jax.ShapeDtypeStruct((2, b, t, d), F32),
        grid=(b, 2, nc),
        in_specs=[two, two, mats, mats, gsp],
        out_specs=two,
        scratch_shapes=[pltpu.VMEM((npair, LANES, LANES), F32)],
        compiler_params=_params(("parallel", "parallel", "arbitrary")),
        name="rw_state",
    )(rhat, yhat, wm, gm, gam)


def _rw_out_body(y0_ref, y1_ref, bonus_ref, g_ref, lg_ref, lb_ref, o_ref):
    y = y0_ref[...] + y1_ref[...]
    mu = _head_sum(y) * (1.0 / HEAD)
    yc = y - mu
    var = _head_sum(yc * yc) * (1.0 / HEAD)
    yn = yc * lax.rsqrt(var + RW_LN_EPS) * lg_ref[...] + lb_ref[...]
    o_ref[...] = ((yn + bonus_ref[...]) * g_ref[...]).astype(o_ref.dtype)


def _rw_out(y, bonus, g, ln_g, ln_b):
    b, t, d = bonus.shape
    return pl.pallas_call(
        _rw_out_body,
        out_shape=jax.ShapeDtypeStruct((b, t, d), BF16),
        grid=(b, t // TM),
        in_specs=[_stream2(d, 0), _stream2(d, 1), _stream(d), _stream(d), _vec(d), _vec(d)],
        out_specs=_stream(d),
        compiler_params=_params(("parallel", "parallel")),
        name="rw_out",
    )(y, y, bonus, g, ln_g.reshape(1, d), ln_b.reshape(1, d))


def _rwkv_layer(h, w, *, n_ctx_rows):
    b, t, d = h.shape
    m = b * t
    xr, xw, xk, xv, xa, xg = [a.reshape(m, d) for a in _rw_mix(h, w["mix"], nct=n_ctx_rows // TM)]
    r = _matmul(xr, w["r_w"]).reshape(b, t, d)
    k = _matmul(xk, w["k_w"]).reshape(b, t, d)
    v = _matmul(xv, w["v_w"]).reshape(b, t, d)
    g = _lora(xg, w["g1"], w["g2"], jnp.zeros((1, d), F32), "sigmoid").reshape(b, t, d)
    wl = _lora(xw, w["w1"], w["w2"], w["w0"], "tanh").reshape(b, t, 2 * d)
    al = _lora(xa, w["a1"], w["a2"], w["a0"], "none").reshape(b, t, 2 * d)
    rhat, yhat, wm, gm, gam, bonus = _rw_chunks(r, k, v, wl, al, w["k_k"], w["k_a"], w["r_k"])
    y = _rw_state(rhat, yhat, wm, gm, gam, n_ctx_rows=n_ctx_rows)
    o = _rw_out(y, bonus, g, w["ln_g"], w["ln_b"])
    return _matmul(o.reshape(m, d), w["out_w"]).reshape(b, t, d)


def _pool_body(h_ref, hp_ref, hn_ref, w_ref, sc_ref, o_ref, *, nt, nct):
    i = pl.program_id(1)
    first, last = _seq_edges(i, nt, nct)
    h = h_ref[...]
    halo = POOL_HALO
    ext = jnp.concatenate([jnp.where(first, 0.0, hp_ref[...]), h, jnp.where(last, 0.0, hn_ref[...])], axis=0)
    seq_start = jnp.where(i < nct, 0, nct)
    seq_len = jnp.where(i < nct, nct, nt - nct) * TM
    gw = h.shape[1] // len(POOL_WINDOWS)
    pos = (i - seq_start) * TM + lax.broadcasted_iota(jnp.int32, (TM, gw), 0)
    tr = lax.broadcasted_iota(jnp.int32, (TM, TM + 2 * halo), 0)
    er = lax.broadcasted_iota(jnp.int32, (TM, TM + 2 * halo), 1)
    for gi, win in enumerate(POOL_WINDOWS):
        half = win // 2
        band = jnp.logical_and(er >= tr + halo - half, er < tr + halo + half)
        cols = slice(gi * gw, (gi + 1) * gw)
        wsum = _dot01_left(band, ext[:, cols])
        cnt = (jnp.minimum(pos + half, seq_len) - jnp.maximum(pos - half, 0)).astype(F32)
        pooled = wsum / cnt - h[:, cols]
        o_ref[:, cols] = _bdot(pooled, w_ref[gi]) * sc_ref[:, cols]


def _pool_layer(h, pl_w, scale, *, nct):
    b, t, d = h.shape
    nt = t // TM
    ng, gw, _ = pl_w.shape
    return pl.pallas_call(
        functools.partial(_pool_body, nt=nt, nct=nct),
        out_shape=jax.ShapeDtypeStruct((b, t, d), F32),
        grid=(b, nt),
        in_specs=[_stream(d), _halo(d, POOL_HALO, t, -1), _halo(d, POOL_HALO, t, +1),
                  pl.BlockSpec((ng, gw, gw), lambda b_, i: (0, 0, 0)), _vec(d)],
        out_specs=_stream(d),
        compiler_params=_params(("parallel", "parallel")),
        name="pool_mixer",
    )(h, h, h, pl_w, scale.reshape(1, d))


def _at_prep_body(qkv_ref, qg_ref, kg_ref, cos_ref, sin_ref, q_ref, k_ref, v_ref, *, scale):
    nq = q_ref.shape[1]
    nk = AT_KV_HEADS * HEAD
    cos = cos_ref[...]
    sin = sin_ref[...]
    lane = lax.broadcasted_iota(jnp.int32, cos.shape, 1)
    up = (lane & 31) < 16
    lo = lane < HEAD

    def norm_rope(x, g):
        xn = x * lax.rsqrt(_head_sum(x * x) * (1.0 / HEAD) + NORM_EPS) * g
        swapped = jnp.where(up, pltpu.roll(xn, LANES - 16, axis=1), pltpu.roll(xn, 16, axis=1))
        return xn * cos + swapped * sin

    for s in range(nq // LANES):
        sl = slice(s * LANES, (s + 1) * LANES)
        q_ref[:, sl] = (norm_rope(qkv_ref[:, sl], qg_ref[...]) * scale).astype(q_ref.dtype)
    for s in range(nk // LANES):
        kslab = norm_rope(qkv_ref[:, nq + s * LANES:nq + (s + 1) * LANES], kg_ref[...])
        vslab = qkv_ref[:, nq + nk + s * LANES:nq + nk + (s + 1) * LANES]
        for slab, ref in ((kslab, k_ref), (vslab, v_ref)):
            rolled = pltpu.roll(slab, HEAD, axis=1)
            ref[2 * s] = jnp.where(lo, slab, rolled).astype(ref.dtype)
            ref[2 * s + 1] = jnp.where(lo, rolled, slab).astype(ref.dtype)


def _at_prep(qkv, q_g, k_g, cos, sin):
    b, t, _ = qkv.shape
    nq = AT_HEADS * HEAD
    width = qkv.shape[2]
    kvs = jax.ShapeDtypeStruct((b, AT_KV_HEADS, t, LANES), BF16)
    kv_spec = pl.BlockSpec((None, AT_KV_HEADS, TM, LANES), lambda b_, i: (b_, 0, i, 0))
    tab = pl.BlockSpec((TM, LANES), lambda b_, i: (i, 0))
    tile2 = lambda g: jnp.tile(g.reshape(1, HEAD), (1, LANES // HEAD))
    return pl.pallas_call(
        functools.partial(_at_prep_body, scale=HEAD ** -0.5 * math.log2(math.e)),
        out_shape=(jax.ShapeDtypeStruct((b, t, nq), BF16), kvs, kvs),
        grid=(b, t // TM),
        in_specs=[_stream(width), _vec(LANES), _vec(LANES), tab, tab],
        out_specs=(_stream(nq), kv_spec, kv_spec),
        compiler_params=_params(("parallel", "parallel")),
        name="at_prep",
    )(qkv, tile2(q_g), tile2(k_g), cos, sin)


def _flash_body(q_ref, k_ref, v_ref, o_ref, sa_ref, sb_ref, *, tk):
    tq = q_ref.shape[0]
    n = k_ref.shape[0] // tk
    lane = lax.broadcasted_iota(jnp.int32, (tq, LANES), 1)
    lo = lane < HEAD
    q = q_ref[...]
    zero = jnp.zeros_like(q)
    qs = jnp.concatenate([jnp.where(lo, q, zero), jnp.where(lo, zero, q)], axis=0)

    def scores(j, dst_ref):
        start = pl.multiple_of(j * tk, tk)
        dst_ref[...] = _bdot_nt(qs, k_ref[pl.ds(start, tk), :])

    def absorb(j, src_ref, carry):
        m, l, acc = carry
        start = pl.multiple_of(j * tk, tk)
        s = src_ref[...]
        m_new = jnp.maximum(m, jnp.max(s, axis=-1, keepdims=True))
        alpha = jnp.exp2(m - m_new)
        p = jnp.exp2(s - m_new)
        l = alpha * l + jnp.sum(p, axis=-1, keepdims=True)
        acc = alpha * acc + _bdot(p, v_ref[pl.ds(start, tk), :])
        return m_new, l, acc

    def two_chunks(i, carry):
        j = 2 * i
        scores(j + 1, sb_ref)
        carry = absorb(j, sa_ref, carry)
        scores(j + 2, sa_ref)
        return absorb(j + 1, sb_ref, carry)

    carry = (jnp.full((2 * tq, 1), -jnp.inf, F32), jnp.zeros((2 * tq, 1), F32), jnp.zeros((2 * tq, LANES), F32))
    scores(0, sa_ref)
    carry = lax.fori_loop(0, (n - 1) // 2, two_chunks, carry)
    if n % 2 == 0:
        scores(n - 1, sb_ref)
        carry = absorb(n - 2, sa_ref, carry)
        carry = absorb(n - 1, sb_ref, carry)
    else:
        carry = absorb(n - 1, sa_ref, carry)
    _, l, acc = carry
    o2 = acc / l
    o_ref[...] = jnp.where(lo, o2[0:tq], o2[tq:2 * tq]).astype(o_ref.dtype)


def _flash(q, k2, v2, *, n_ctx_rows):
    b, t, nq = q.shape
    tq = TM
    tl = t - n_ctx_rows
    off = n_ctx_rows // tq
    tk = _pick(t, (768, 512, 256, 128))
    npair = nq // LANES
    hp = AT_HEADS // AT_KV_HEADS // 2
    kv = pl.BlockSpec((None, None, t, LANES), lambda b_, p, i: (b_, p // hp, 0, 0))
    return pl.pallas_call(
        functools.partial(_flash_body, tk=tk),
        out_shape=jax.ShapeDtypeStruct((b, tl, nq), BF16),
        grid=(b, npair, tl // tq),
        in_specs=[pl.BlockSpec((None, tq, LANES), lambda b_, p, i: (b_, i + off, p)), kv, kv],
        out_specs=pl.BlockSpec((None, tq, LANES), lambda b_, p, i: (b_, i, p)),
        scratch_shapes=[pltpu.VMEM((2 * tq, tk), F32), pltpu.VMEM((2 * tq, tk), F32)],
        compiler_params=_params(("parallel", "parallel", "parallel")),
        name="flash_gqa",
    )(q, k2, v2)


def _rope_tables(n_ctx_rows, seq):
    quarter = HEAD // 4
    inv = ROPE_THETA ** (-jnp.arange(quarter, dtype=F32) / quarter)
    rows = jnp.repeat(jnp.arange(seq // GRID_W, dtype=jnp.int32), GRID_W).astype(F32)
    cols = (jnp.arange(seq, dtype=jnp.int32) % GRID_W).astype(F32)
    ar = rows[:, None] * inv
    ac = cols[:, None] * inv
    cos = jnp.concatenate([jnp.cos(ar), jnp.cos(ar), jnp.cos(ac), jnp.cos(ac)], axis=1)
    sin = jnp.concatenate([-jnp.sin(ar), jnp.sin(ar), -jnp.sin(ac), jnp.sin(ac)], axis=1)
    cos = jnp.concatenate([jnp.ones((n_ctx_rows, HEAD), F32), cos], axis=0)
    sin = jnp.concatenate([jnp.zeros((n_ctx_rows, HEAD), F32), sin], axis=0)
    return jnp.tile(cos, (1, LANES // HEAD)), jnp.tile(sin, (1, LANES // HEAD))


def _attn_layer(h, w, *, n_ctx_rows):
    b, t, d = h.shape
    qkv = _matmul(h.reshape(b * t, d), w["qkv_w"]).reshape(b, t, -1)
    cos, sin = _rope_tables(n_ctx_rows, t - n_ctx_rows)
    q, k2, v2 = _at_prep(qkv, w["q_g"], w["k_g"], cos, sin)
    o = _flash(q, k2, v2, n_ctx_rows=n_ctx_rows)
    tl = t - n_ctx_rows
    return _matmul(o.reshape(b * tl, -1), w["out_w"]).reshape(b, tl, d)


def _ffn(h2, w_in, w_out):
    return _matmul(_swiglu_in(h2, w_in), w_out)


MOE_ROWS = 512
MOE_TOK = 512


def _moe_rank_body(sel_ref, rank_ref, rankt_ref, selt_ref, tot_ref, carry_ref, carryt_ref):
    @pl.when(pl.program_id(0) == 0)
    def _():
        carry_ref[...] = jnp.zeros_like(carry_ref)
        carryt_ref[...] = jnp.zeros_like(carryt_ref)

    sel = sel_ref[...]
    tb = sel.shape[0]
    ri = lax.broadcasted_iota(jnp.int32, (tb, tb), 0)
    ci = lax.broadcasted_iota(jnp.int32, (tb, tb), 1)
    rank_ref[...] = _bdot(_as01(ri > ci), sel) + carry_ref[0:1, :]
    pick = lax.broadcasted_iota(jnp.int32, (8, LANES), 0) == lax.broadcasted_iota(jnp.int32, (8, LANES), 1)
    selt = _bdot_nt(_as01(pick), sel)
    selt_ref[...] = selt
    rankt_ref[...] = _bdot(selt, _as01(ri < ci)) + carryt_ref[:, 0:1]
    carry_ref[...] = carry_ref[...] + jnp.sum(sel, axis=0, keepdims=True)
    carryt_ref[...] = carryt_ref[...] + jnp.sum(selt, axis=1, keepdims=True)
    tot_ref[...] = carry_ref[...]


def _moe_rank(sel):
    m = sel.shape[0]
    tb = MOE_TOK
    return pl.pallas_call(
        _moe_rank_body,
        out_shape=(jax.ShapeDtypeStruct((m, LANES), F32), jax.ShapeDtypeStruct((8, m), F32),
                   jax.ShapeDtypeStruct((8, m), F32), jax.ShapeDtypeStruct((8, LANES), F32)),
        grid=(m // tb,),
        in_specs=[pl.BlockSpec((tb, LANES), lambda i: (i, 0))],
        out_specs=(pl.BlockSpec((tb, LANES), lambda i: (i, 0)), pl.BlockSpec((8, tb), lambda i: (0, i)),
                   pl.BlockSpec((8, tb), lambda i: (0, i)), pl.BlockSpec((8, LANES), lambda i: (0, 0))),
        scratch_shapes=[pltpu.VMEM((8, LANES), F32), pltpu.VMEM((8, LANES), F32)],
        compiler_params=_params(("arbitrary",)),
        name="moe_rank",
    )(sel)


def _moe_plan(sel, rank, rankt, selt, tot, m):
    tr, tb, ne = MOE_ROWS, MOE_TOK, N_EXPERTS
    nb = m // tb
    nt = 2 * m // tr + ne
    i32 = jnp.int32
    cnt = tot[0, :ne].astype(i32)
    tile_start = jnp.concatenate([jnp.zeros((1,), i32), jnp.cumsum((cnt + tr - 1) // tr)])
    off = tile_start[:ne] * tr
    n_tiles = tile_start[ne]
    tiles = jnp.arange(nt, dtype=i32)
    tile_valid = tiles < n_tiles
    tile_exp = jnp.minimum(jnp.searchsorted(tile_start[1:], tiles, side="right").astype(i32), ne - 1)
    last_exp = tile_exp[jnp.maximum(n_tiles - 1, 0)]
    tile_exp = jnp.where(tile_valid, tile_exp, last_exp)
    offp = jnp.pad(off.astype(F32), (0, LANES - ne))
    posmat = jnp.where(sel > 0, rank + offp[None, :], -1.0)
    post = jnp.where(selt > 0, rankt + jnp.pad(off.astype(F32), (0, 8 - ne))[:, None], -1.0)
    blkcum = jnp.concatenate([rank[::tb, :ne].astype(i32).T, cnt[:, None]], axis=1)

    k_lo = tiles * tr - off[tile_exp]
    k_hi = jnp.minimum(cnt[tile_exp], k_lo + tr) - 1
    ends = blkcum[tile_exp][:, 1:]
    lo_blk = jnp.minimum(jnp.sum(ends <= k_lo[:, None], axis=1), nb - 1).astype(i32)
    hi_blk = jnp.minimum(jnp.sum(ends <= k_hi[:, None], axis=1), nb - 1).astype(i32)
    span = jnp.where(tile_valid, hi_blk - lo_blk + 1, 1)
    lo_blk = jnp.where(tile_valid, lo_blk, 0)
    g_end = jnp.cumsum(span)
    g_start = g_end - span
    ns = nt + ne * nb
    steps = jnp.arange(ns, dtype=i32)
    g_tile = jnp.minimum(jnp.searchsorted(g_end, steps, side="right").astype(i32), nt - 1)
    g_valid = (steps < g_end[-1]).astype(i32)
    g_blk = jnp.clip(lo_blk[g_tile] + steps - g_start[g_tile], 0, nb - 1).astype(i32)
    g_first = (steps == g_start[g_tile]).astype(i32)
    gather = (g_tile, g_blk, tile_exp[g_tile], g_first, g_valid)

    r_lo = off[:, None] + blkcum[:, :-1]
    r_hi = off[:, None] + blkcum[:, 1:] - 1
    t_lo = (r_lo // tr).T.reshape(-1)
    n_t = jnp.where(r_hi >= r_lo, r_hi // tr - r_lo // tr + 1, 0).T.reshape(-1)
    c_end = jnp.cumsum(n_t)
    c_start = c_end - n_t
    idx = jnp.minimum(jnp.searchsorted(c_end, steps, side="right").astype(i32), nb * ne - 1)
    c_valid = (steps < c_end[-1]).astype(i32)
    c_blk = idx // ne
    c_tile = jnp.clip(t_lo[idx] + steps - c_start[idx], 0, nt - 1).astype(i32)
    c_first = (steps == c_start[c_blk * ne]).astype(i32)
    combine = (c_blk, c_tile, idx % ne, c_first, c_valid)
    return dict(nt=nt, ns=ns, tile_exp=tile_exp, n_tiles=n_tiles.reshape(1), posmat=posmat, post=post,
                gather=gather, combine=combine)


def _moe_gather_body(g_tile, g_blk, g_exp, g_first, g_valid, h_ref, post_ref, gates_ref, xs_ref, ws_ref):
    s = pl.program_id(0)

    @pl.when(g_valid[s] == 1)
    def _():
        tr, tb = xs_ref.shape[0], h_ref.shape[0]
        rows = (lax.broadcasted_iota(jnp.int32, (tr, tb), 0) + g_tile[s] * tr).astype(F32)
        onehot = post_ref[pl.ds(g_exp[s], 1), :] == rows
        x = _bdot(_as01(onehot), h_ref[...])
        w = _dot01_left(onehot, gates_ref[...])

        @pl.when(g_first[s] == 1)
        def _():
            xs_ref[...] = x.astype(xs_ref.dtype)
            ws_ref[...] = w

        @pl.when(g_first[s] == 0)
        def _():
            xs_ref[...] = xs_ref[...] + x.astype(xs_ref.dtype)
            ws_ref[...] = ws_ref[...] + w


def _moe_gather(h2, post, gates, plan):
    m, d = h2.shape
    tr, tb = MOE_ROWS, MOE_TOK
    grid_spec = pltpu.PrefetchScalarGridSpec(
        num_scalar_prefetch=5, grid=(plan["ns"],),
        in_specs=[pl.BlockSpec((tb, d), lambda s, gt, gb, ge, gf, gv: (gb[s], 0)),
                  pl.BlockSpec((8, tb), lambda s, gt, gb, ge, gf, gv: (0, gb[s])),
                  pl.BlockSpec((tb, LANES), lambda s, gt, gb, ge, gf, gv: (gb[s], 0))],
        out_specs=(pl.BlockSpec((tr, d), lambda s, gt, gb, ge, gf, gv: (gt[s], 0)),
                   pl.BlockSpec((tr, LANES), lambda s, gt, gb, ge, gf, gv: (gt[s], 0))))
    return pl.pallas_call(
        _moe_gather_body,
        out_shape=(jax.ShapeDtypeStruct((plan["nt"] * tr, d), BF16),
                   jax.ShapeDtypeStruct((plan["nt"] * tr, LANES), F32)),
        grid_spec=grid_spec,
        compiler_params=_params(("arbitrary",)),
        name="moe_gather",
    )(*plan["gather"], h2, post, gates)


def _moe_expert_body(t_exp, n_tiles, xs_ref, ws_ref, wg_ref, wu_ref, wo_ref, yh_ref, yl_ref, acc_ref):
    j = pl.program_id(0)
    f = pl.program_id(1)

    @pl.when(f == 0)
    def _():
        acc_ref[...] = jnp.zeros_like(acc_ref)

    @pl.when(j < n_tiles[0])
    def _():
        x = xs_ref[...]
        act = _silu(_bdot(x, wg_ref[...])) * _bdot(x, wu_ref[...])
        acc_ref[...] += _bdot(act, wo_ref[...])

    @pl.when(f == pl.num_programs(1) - 1)
    def _():
        ws = ws_ref[...]
        lane = lax.broadcasted_iota(jnp.int32, ws.shape, 1)
        gate = jnp.sum(jnp.where(lane == t_exp[j], ws, 0.0), axis=-1, keepdims=True)
        y = acc_ref[...] * gate
        hi = y.astype(BF16)
        yh_ref[...] = hi
        yl_ref[...] = (y - hi.astype(F32)).astype(BF16)


def _moe_experts(xs, ws, w_in, w_out, layer, plan):
    rows, d = xs.shape
    tr = MOE_ROWS
    f = w_in.shape[3] // 2
    fk = _pick(f, (512, 256, 128))
    nf = f // fk
    hold = lambda j, fi, nt: jnp.where(j < nt[0], fi, nf - 1)
    grid_spec = pltpu.PrefetchScalarGridSpec(
        num_scalar_prefetch=2, grid=(rows // tr, nf),
        in_specs=[pl.BlockSpec((tr, d), lambda j, fi, te, nt: (j, 0)),
                  pl.BlockSpec((tr, LANES), lambda j, fi, te, nt: (j, 0)),
                  pl.BlockSpec((None, None, d, fk), lambda j, fi, te, nt: (layer, te[j], 0, hold(j, fi, nt))),
                  pl.BlockSpec((None, None, d, fk), lambda j, fi, te, nt: (layer, te[j], 0, hold(j, fi, nt) + nf)),
                  pl.BlockSpec((None, None, fk, d), lambda j, fi, te, nt: (layer, te[j], hold(j, fi, nt), 0))],
        out_specs=(pl.BlockSpec((tr, d), lambda j, fi, te, nt: (j, 0)),
                   pl.BlockSpec((tr, d), lambda j, fi, te, nt: (j, 0))),
        scratch_shapes=[pltpu.VMEM((tr, d), F32)])
    ysh = jax.ShapeDtypeStruct((rows, d), BF16)
    return pl.pallas_call(
        _moe_expert_body,
        out_shape=(ysh, ysh),
        grid_spec=grid_spec,
        compiler_params=_params(("arbitrary", "arbitrary")),
        name="moe_experts",
    )(plan["tile_exp"], plan["n_tiles"], xs, ws, w_in, w_in, w_out)


def _moe_combine_body(c_blk, c_tile, c_exp, c_first, c_valid, pos_ref, yh_ref, yl_ref, o_ref):
    s = pl.program_id(0)

    @pl.when(c_valid[s] == 1)
    def _():
        tb, tr = o_ref.shape[0], yh_ref.shape[0]
        pos = pos_ref[...]
        lane = lax.broadcasted_iota(jnp.int32, pos.shape, 1)
        pos_e = jnp.sum(jnp.where(lane == c_exp[s], pos, 0.0), axis=-1, keepdims=True)
        rows = (lax.broadcasted_iota(jnp.int32, (tb, tr), 1) + c_tile[s] * tr).astype(F32)
        onehot = _as01(pos_e == rows)
        y = (jnp.dot(onehot, yh_ref[...], preferred_element_type=F32)
             + jnp.dot(onehot, yl_ref[...], preferred_element_type=F32))

        @pl.when(c_first[s] == 1)
        def _():
            o_ref[...] = y

        @pl.when(c_first[s] == 0)
        def _():
            o_ref[...] = o_ref[...] + y


def _moe_combine(posmat, yh, yl, plan):
    m = posmat.shape[0]
    d = yh.shape[1]
    tr, tb = MOE_ROWS, MOE_TOK
    grid_spec = pltpu.PrefetchScalarGridSpec(
        num_scalar_prefetch=5, grid=(plan["ns"],),
        in_specs=[pl.BlockSpec((tb, LANES), lambda s, cb, ct, ce, cf, cv: (cb[s], 0)),
                  pl.BlockSpec((tr, d), lambda s, cb, ct, ce, cf, cv: (ct[s], 0)),
                  pl.BlockSpec((tr, d), lambda s, cb, ct, ce, cf, cv: (ct[s], 0))],
        out_specs=pl.BlockSpec((tb, d), lambda s, cb, ct, ce, cf, cv: (cb[s], 0)))
    return pl.pallas_call(
        _moe_combine_body,
        out_shape=jax.ShapeDtypeStruct((m, d), F32),
        grid_spec=grid_spec,
        compiler_params=_params(("arbitrary",)),
        name="moe_combine",
    )(*plan["combine"], posmat, yh, yl)


def _moe(h2, gates, sel, w_in, w_out, layer):
    m = h2.shape[0]
    rank, rankt, selt, tot = _moe_rank(sel)
    plan = _moe_plan(sel, rank, rankt, selt, tot, m)
    xs, ws = _moe_gather(h2, plan["post"], gates, plan)
    yh, yl = _moe_experts(xs, ws, w_in, w_out, layer, plan)
    return _moe_combine(plan["posmat"], yh, yl, plan)


def kernel(x, c, ctx, c_ctx, mod_w, mod_b, norm1_g, norm2_g, final_g, mb_in_w, mb_conv_w, mb_conv_b, mb_dt_bias, mb_a_log, mb_d, mb_norm_g, mb_out_w, rw_mix, rw_rkv_w, rw_w0, rw_w1, rw_w2, rw_a0, rw_a1, rw_a2, rw_g1, rw_g2, rw_k_k, rw_k_a, rw_r_k, rw_ln_g, rw_ln_b, rw_out_w, pl_w, pl_scale, at_qkv_w, at_q_g, at_k_g, at_out_w, ff_in_w, ff_out_w, moe_router_w, moe_in_w, moe_out_w):
    b, seq, d = x.shape
    n_ctx = ctx.shape[1]
    depth = mod_w.shape[0]
    t = n_ctx + seq
    nct = n_ctx // TM
    assert depth == 4 and n_ctx % TM == 0 and seq % TM == 0 and b + 1 <= 8
    bf = lambda a: a.astype(BF16)

    svec = jnp.concatenate([c, c_ctx[None, :], jnp.zeros((8 - b - 1, d), F32)], axis=0)
    mods = _mods(svec, bf(mod_w), mod_b)
    mv = jnp.pad(mods[:, :b + 1].reshape(depth, b + 1, 6, d), ((0, 0), (0, 0), (0, 2), (0, 0)))

    xs = jnp.concatenate([ctx, x], axis=1)
    m = b * t
    resid = functools.partial(_resid_norm, nct=nct, x_off=0)

    di = MB_HEADS * HEAD
    xbc = di + 2 * MB_GROUPS * MB_STATE
    mamba_w = dict(in_w=bf(mb_in_w[0]),
                   in_w_dt=bf(jnp.pad(mb_in_w[0][:, di + xbc:], ((0, 0), (0, LANES - 2 * MB_HEADS)))),
                   conv_w=mb_conv_w[0], conv_b=mb_conv_b[0], dt_bias=mb_dt_bias[0], a_log=mb_a_log[0],
                   dvec=jnp.repeat(mb_d[0], HEAD).reshape(1, di), norm_g=mb_norm_g[0], out_w=bf(mb_out_w[0]))
    h = _norm_mod(xs, norm1_g[0], mv, 0, nct, BF16)
    y = _mamba_layer(h, mamba_w, n_ctx_rows=n_ctx)
    xs, h2 = resid(xs, y, mv, norm2_g[0], gate_layer=0, jg=2, mod_layer=0, js=3, jc=4, out_dtype=BF16)
    f = _ffn(h2.reshape(m, d), bf(ff_in_w[0]), bf(ff_out_w[0])).reshape(b, t, d)
    xs, h = resid(xs, f, mv, norm1_g[1], gate_layer=0, jg=5, mod_layer=1, js=0, jc=1, out_dtype=F32)

    blockdiag = lambda u: jnp.concatenate(
        [jnp.concatenate([u[0], jnp.zeros_like(u[0])], axis=1),
         jnp.concatenate([jnp.zeros_like(u[1]), u[1]], axis=1)], axis=0)
    rwkv_w = dict(mix=rw_mix[0], r_w=bf(rw_rkv_w[0, 0]), k_w=bf(rw_rkv_w[0, 1]), v_w=bf(rw_rkv_w[0, 2]),
                  g1=bf(rw_g1[0]), g2=bf(rw_g2[0]),
                  w1=bf(jnp.concatenate([rw_w1[0, 0], rw_w1[0, 1]], axis=1)), w2=bf(blockdiag(rw_w2[0])),
                  w0=rw_w0[0].reshape(1, 2 * d),
                  a1=bf(jnp.concatenate([rw_a1[0, 0], rw_a1[0, 1]], axis=1)), a2=bf(blockdiag(rw_a2[0])),
                  a0=rw_a0[0].reshape(1, 2 * d),
                  k_k=rw_k_k[0], k_a=rw_k_a[0], r_k=rw_r_k[0].reshape(d), ln_g=rw_ln_g[0], ln_b=rw_ln_b[0],
                  out_w=bf(rw_out_w[0]))
    y = _rwkv_layer(h, rwkv_w, n_ctx_rows=n_ctx)
    moe_in_b, moe_out_b = bf(moe_in_w), bf(moe_out_w)
    xs, h2, gates, sel = _resid_norm_router(xs, y, mv, norm2_g[1], moe_router_w[0], gate_layer=1, jg=2,
                                            mod_layer=1, js=3, jc=4, nct=nct, x_off=0)
    f = _moe(h2.reshape(m, d), gates.reshape(m, LANES), sel.reshape(m, LANES), moe_in_b, moe_out_b, 0)
    f = f.reshape(b, t, d)
    xs, h = resid(xs, f, mv, norm1_g[2], gate_layer=1, jg=5, mod_layer=2, js=0, jc=1, out_dtype=F32)

    y = _pool_layer(h, bf(pl_w[0]), pl_scale[0], nct=nct)
    xs, h2 = resid(xs, y, mv, norm2_g[2], gate_layer=2, jg=2, mod_layer=2, js=3, jc=4, out_dtype=BF16)
    f = _ffn(h2.reshape(m, d), bf(ff_in_w[1]), bf(ff_out_w[1])).reshape(b, t, d)
    xs, h = resid(xs, f, mv, norm1_g[3], gate_layer=2, jg=5, mod_layer=3, js=0, jc=1, out_dtype=BF16)

    attn_w = dict(qkv_w=bf(at_qkv_w[0]), q_g=at_q_g[0], k_g=at_k_g[0], out_w=bf(at_out_w[0]))
    y = _attn_layer(h, attn_w, n_ctx_rows=n_ctx)
    xl, h2, gates, sel = _resid_norm_router(xs, y, mv, norm2_g[3], moe_router_w[1], gate_layer=3, jg=2,
                                            mod_layer=3, js=3, jc=4, nct=0, x_off=nct)
    ml = b * seq
    f = _moe(h2.reshape(ml, d), gates.reshape(ml, LANES), sel.reshape(ml, LANES), moe_in_b, moe_out_b, 1)
    return _resid_final(xl, f.reshape(b, seq, d), mv, final_g, gate_layer=3, jg=5)
```

```python
import functools
import math

import jax
import jax.numpy as jnp
from jax import lax
from jax.experimental import pallas as pl
from jax.experimental.pallas import tpu as pltpu

F32 = jnp.float32
BF16 = jnp.bfloat16

NORM_EPS = 1e-6
TM = 256
LANES = 128
HEAD = 64
VMEM_LIMIT = 48 * 1024 * 1024

MB_HEADS = 32
MB_GROUPS = 4
MB_STATE = 128
MB_CHUNK = 128
MB_CONV = 5
RW_CHUNK = 64
RW_DECAY_SCALE = 0.606531
RW_LN_EPS = 64e-5
POOL_WINDOWS = (2, 4, 8, 16)
POOL_HALO = 16
AT_HEADS = 16
AT_KV_HEADS = 4
ROPE_THETA = 10000.0
GRID_W = 64
N_EXPERTS = 8


def _params(sem, vmem=VMEM_LIMIT):
    return pltpu.CompilerParams(dimension_semantics=sem, vmem_limit_bytes=vmem)


def _bdot(a, b):
    return jnp.dot(a.astype(BF16), b.astype(BF16), preferred_element_type=F32)


def _bdot_nt(a, b):
    return lax.dot_general(a.astype(BF16), b.astype(BF16), (((1,), (1,)), ((), ())),
                           preferred_element_type=F32)


def _bdot_tn(a, b):
    return lax.dot_general(a.astype(BF16), b.astype(BF16), (((0,), (0,)), ((), ())),
                           preferred_element_type=F32)


def _split3(x):
    p0 = x.astype(BF16)
    r1 = x - p0.astype(F32)
    p1 = r1.astype(BF16)
    p2 = (r1 - p1.astype(F32)).astype(BF16)
    return p0, p1, p2


def _as01(mask):
    return jnp.where(mask, 1.0, 0.0).astype(BF16)


def _dot01_left(sel, x):
    sel = _as01(sel)
    p0, p1, p2 = _split3(x)
    return (jnp.dot(sel, p0, preferred_element_type=F32)
            + jnp.dot(sel, p1, preferred_element_type=F32)
            + jnp.dot(sel, p2, preferred_element_type=F32))


def _dot01_right(x, sel):
    sel = _as01(sel)
    p0, p1, p2 = _split3(x)
    return (jnp.dot(p0, sel, preferred_element_type=F32)
            + jnp.dot(p1, sel, preferred_element_type=F32)
            + jnp.dot(p2, sel, preferred_element_type=F32))


def _head_sum(x):
    r = lax.broadcasted_iota(jnp.int32, (LANES, LANES), 0) // HEAD
    c = lax.broadcasted_iota(jnp.int32, (LANES, LANES), 1) // HEAD
    ones_bd = r == c
    slabs = [_dot01_right(x[:, s:s + LANES], ones_bd) for s in range(0, x.shape[1], LANES)]
    return slabs[0] if len(slabs) == 1 else jnp.concatenate(slabs, axis=1)


def _sigmoid(x):
    return 1.0 / (1.0 + jnp.exp(-x))


def _silu(x):
    return x * _sigmoid(x)


def _rms(x, g):
    ms = jnp.mean(x * x, axis=-1, keepdims=True)
    return x * lax.rsqrt(ms + NORM_EPS) * g


def _mods_body(s_ref, w_ref, b_ref, o_ref):
    o_ref[...] = _bdot(_silu(s_ref[...]), w_ref[...]) + b_ref[...]


def _mods(svec, mod_w, mod_b):
    depth, d, n6 = mod_w.shape
    tn = 1536
    return pl.pallas_call(
        _mods_body,
        out_shape=jax.ShapeDtypeStruct((depth, 8, n6), F32),
        grid=(depth, n6 // tn),
        in_specs=[pl.BlockSpec((8, d), lambda l, j: (0, 0)),
                  pl.BlockSpec((None, d, tn), lambda l, j: (l, 0, j)),
                  pl.BlockSpec((None, 1, tn), lambda l, j: (l, 0, j))],
        out_specs=pl.BlockSpec((None, 8, tn), lambda l, j: (l, 0, j)),
        compiler_params=_params(("parallel", "parallel")),
        name="mods",
    )(svec, mod_w, mod_b.reshape(depth, 1, n6))


def _stream(width, off=0, rows=TM):
    return pl.BlockSpec((None, rows, width), lambda b, i: (b, i + off, 0))


def _stream2(width, d, off=0):
    return pl.BlockSpec((None, None, TM, width), lambda b, i: (d, b, i + off, 0))


def _vec(width, rows=1):
    return pl.BlockSpec((rows, width), lambda b, i: (0, 0))


def _mv(layer, nct, nb, d):
    return pl.BlockSpec((None, None, 8, d), lambda b, i: (layer, jnp.where(i < nct, nb, b), 0, 0))


def _halo(width, nrows, t_total, side):
    per = TM // nrows
    last = t_total // nrows - 1
    if side < 0:
        return pl.BlockSpec((None, nrows, width), lambda b, i: (b, jnp.maximum(i * per - 1, 0), 0))
    return pl.BlockSpec((None, nrows, width), lambda b, i: (b, jnp.minimum((i + 1) * per, last), 0))


def _seq_edges(i, nt, nct):
    first = jnp.logical_or(i == 0, i == nct)
    last = jnp.logical_or(i == nct - 1, i == nt - 1)
    return first, last


def _shift_rows(x, prev8, next8, o):
    rows = x.shape[0]
    r8 = lax.broadcasted_iota(jnp.int32, (8, x.shape[1]), 0)
    if o < 0:
        k = -o
        s = pltpu.roll(x, k, axis=0)
        top = jnp.where(r8 < k, pltpu.roll(prev8, k, axis=0), s[0:8])
        return jnp.concatenate([top, s[8:]], axis=0)
    s = pltpu.roll(x, rows - o, axis=0)
    bot = jnp.where(r8 >= 8 - o, pltpu.roll(next8, 8 - o, axis=0), s[rows - 8:])
    return jnp.concatenate([s[:rows - 8], bot], axis=0)


def _norm_mod_body(x_ref, g_ref, mv_ref, h_ref, *, js, jc):
    h = _rms(x_ref[...], g_ref[...]) * (1.0 + mv_ref[jc:jc + 1, :]) + mv_ref[js:js + 1, :]
    h_ref[...] = h.astype(h_ref.dtype)


def _norm_mod(x, g, mv, layer, nct, out_dtype):
    b, t, d = x.shape
    nb = b
    return pl.pallas_call(
        functools.partial(_norm_mod_body, js=0, jc=1),
        out_shape=jax.ShapeDtypeStruct((b, t, d), out_dtype),
        grid=(b, t // TM),
        in_specs=[_stream(d), _vec(d), _mv(layer, nct, nb, d)],
        out_specs=_stream(d),
        compiler_params=_params(("parallel", "parallel")),
        name="norm_mod",
    )(x, g.reshape(1, d), mv)


def _resid_norm_body(x_ref, y_ref, mvg_ref, g_ref, mvm_ref, xo_ref, h_ref, *, jg, js, jc):
    xn = x_ref[...] + mvg_ref[jg:jg + 1, :] * y_ref[...].astype(F32)
    xo_ref[...] = xn
    h = _rms(xn, g_ref[...]) * (1.0 + mvm_ref[jc:jc + 1, :]) + mvm_ref[js:js + 1, :]
    h_ref[...] = h.astype(h_ref.dtype)


def _resid_norm(x, y, mv, g, *, gate_layer, jg, mod_layer, js, jc, nct, x_off, out_dtype):
    b, t, d = y.shape
    return pl.pallas_call(
        functools.partial(_resid_norm_body, jg=jg, js=js, jc=jc),
        out_shape=(jax.ShapeDtypeStruct((b, t, d), F32), jax.ShapeDtypeStruct((b, t, d), out_dtype)),
        grid=(b, t // TM),
        in_specs=[_stream(d, x_off), _stream(d), _mv(gate_layer, nct, b, d), _vec(d),
                  _mv(mod_layer, nct, b, d)],
        out_specs=(_stream(d), _stream(d)),
        compiler_params=_params(("parallel", "parallel")),
        name="resid_norm",
    )(x, y, mv, g.reshape(1, d), mv)


def _top2_gates(logits):
    lane = lax.broadcasted_iota(jnp.int32, logits.shape, 1)
    neg = jnp.float32(-jnp.inf)
    lg = jnp.where(lane < N_EXPERTS, logits, neg)
    v1 = jnp.max(lg, axis=-1, keepdims=True)
    i1 = jnp.min(jnp.where(lg == v1, lane, LANES), axis=-1, keepdims=True)
    lg2 = jnp.where(lane == i1, neg, lg)
    v2 = jnp.max(lg2, axis=-1, keepdims=True)
    i2 = jnp.min(jnp.where(lg2 == v2, lane, LANES), axis=-1, keepdims=True)
    e = jnp.exp(v2 - v1)
    w1 = 1.0 / (1.0 + e)
    w2 = e / (1.0 + e)
    gates = jnp.where(lane == i1, w1, 0.0) + jnp.where(lane == i2, w2, 0.0)
    sel = jnp.where(jnp.logical_or(lane == i1, lane == i2), 1.0, 0.0)
    return gates, sel


def _resid_norm_router_body(x_ref, y_ref, mvg_ref, g_ref, mvm_ref, rw_ref, xo_ref, h_ref, gates_ref, sel_ref,
                            *, jg, js, jc):
    xn = x_ref[...] + mvg_ref[jg:jg + 1, :] * y_ref[...].astype(F32)
    xo_ref[...] = xn
    h = _rms(xn, g_ref[...]) * (1.0 + mvm_ref[jc:jc + 1, :]) + mvm_ref[js:js + 1, :]
    h_ref[...] = h.astype(h_ref.dtype)
    h0, h1, h2 = _split3(h)
    w0, w1, w2 = _split3(rw_ref[...])
    dot = lambda a, c: jnp.dot(a, c, preferred_element_type=F32)
    logits = (dot(h0, w0) + (dot(h0, w1) + dot(h1, w0))
              + (dot(h0, w2) + dot(h1, w1) + dot(h2, w0)))
    gates_ref[...], sel_ref[...] = _top2_gates(logits)


def _resid_norm_router(x, y, mv, g, router_w, *, gate_layer, jg, mod_layer, js, jc, nct, x_off):
    b, t, d = y.shape
    rw = jnp.pad(router_w, ((0, 0), (0, LANES - router_w.shape[1])))
    lanes = jax.ShapeDtypeStruct((b, t, LANES), F32)
    return pl.pallas_call(
        functools.partial(_resid_norm_router_body, jg=jg, js=js, jc=jc),
        out_shape=(jax.ShapeDtypeStruct((b, t, d), F32), jax.ShapeDtypeStruct((b, t, d), BF16), lanes, lanes),
        grid=(b, t // TM),
        in_specs=[_stream(d, x_off), _stream(d), _mv(gate_layer, nct, b, d), _vec(d),
                  _mv(mod_layer, nct, b, d), _vec(LANES, d)],
        out_specs=(_stream(d), _stream(d), _stream(LANES), _stream(LANES)),
        compiler_params=_params(("parallel", "parallel")),
        name="resid_norm_router",
    )(x, y, mv, g.reshape(1, d), mv, rw)


def _resid_final_body(x_ref, y_ref, mvg_ref, g_ref, o_ref, *, jg):
    xn = x_ref[...] + mvg_ref[jg:jg + 1, :] * y_ref[...].astype(F32)
    o_ref[...] = _rms(xn, g_ref[...])


def _resid_final(x, y, mv, g, *, gate_layer, jg):
    b, t, d = y.shape
    return pl.pallas_call(
        functools.partial(_resid_final_body, jg=jg),
        out_shape=jax.ShapeDtypeStruct((b, t, d), F32),
        grid=(b, t // TM),
        in_specs=[_stream(d), _stream(d), _mv(gate_layer, 0, b, d), _vec(d)],
        out_specs=_stream(d),
        compiler_params=_params(("parallel", "parallel")),
        name="resid_final",
    )(x, y, mv, g.reshape(1, d))


def _mm_body(x_ref, w_ref, o_ref):
    o_ref[...] = _bdot(x_ref[...], w_ref[...]).astype(o_ref.dtype)


def _pick(n, prefs):
    for p in prefs:
        if n % p == 0:
            return p
    raise ValueError(f"no tile for {n}")


def _matmul(x, w, *, n_cols=None, out_dtype=F32, tm=None, tn=None):
    m, k = x.shape
    n = n_cols or w.shape[1]
    tm = tm or _pick(m, (1024, 768, 512, 256))
    tn = tn or _pick(n, (1024, 768, 512, 256, 128))
    return pl.pallas_call(
        _mm_body,
        out_shape=jax.ShapeDtypeStruct((m, n), out_dtype),
        grid=(n // tn, m // tm),
        in_specs=[pl.BlockSpec((tm, k), lambda j, i: (i, 0)),
                  pl.BlockSpec((k, tn), lambda j, i: (0, j))],
        out_specs=pl.BlockSpec((tm, tn), lambda j, i: (i, j)),
        compiler_params=_params(("parallel", "parallel")),
        name="matmul",
    )(x, w)


def _swiglu_body(x_ref, wg_ref, wu_ref, o_ref):
    x = x_ref[...]
    o_ref[...] = (_silu(_bdot(x, wg_ref[...])) * _bdot(x, wu_ref[...])).astype(o_ref.dtype)


def _swiglu_in(x, w_in, *, tm=None, tn=None):
    m, k = x.shape
    f = w_in.shape[1] // 2
    tm = tm or _pick(m, (768, 512, 256))
    tn = tn or _pick(f, (1408, 896, 512, 256, 128))
    nf = f // tn
    return pl.pallas_call(
        _swiglu_body,
        out_shape=jax.ShapeDtypeStruct((m, f), BF16),
        grid=(nf, m // tm),
        in_specs=[pl.BlockSpec((tm, k), lambda j, i: (i, 0)),
                  pl.BlockSpec((k, tn), lambda j, i: (0, j)),
                  pl.BlockSpec((k, tn), lambda j, i: (0, j + nf))],
        out_specs=pl.BlockSpec((tm, tn), lambda j, i: (i, j)),
        compiler_params=_params(("parallel", "parallel")),
        name="swiglu_in",
    )(x, w_in, w_in)


def _lora_body(x_ref, a_ref, b_ref, bias_ref, o_ref, *, act):
    t = _bdot(x_ref[...], a_ref[...])
    if act == "tanh":
        t = jnp.tanh(t)
    elif act == "sigmoid":
        t = _sigmoid(t)
    o_ref[...] = _bdot(t, b_ref[...]) + bias_ref[...]


def _lora(x, a, bm, bias, act):
    m, k = x.shape
    r = a.shape[1]
    n = bm.shape[1]
    tm = _pick(m, (512, 256))
    return pl.pallas_call(
        functools.partial(_lora_body, act=act),
        out_shape=jax.ShapeDtypeStruct((m, n), F32),
        grid=(m // tm,),
        in_specs=[pl.BlockSpec((tm, k), lambda i: (i, 0)),
                  pl.BlockSpec((k, r), lambda i: (0, 0)),
                  pl.BlockSpec((r, n), lambda i: (0, 0)),
                  pl.BlockSpec((1, n), lambda i: (0, 0))],
        out_specs=pl.BlockSpec((tm, n), lambda i: (i, 0)),
        compiler_params=_params(("parallel",)),
        name="lora",
    )(x, a, bm, bias)


def _mb_conv_body(x_ref, xp_ref, xn_ref, w_ref, b_ref, o_ref, *, nt, nct):
    i = pl.program_id(1)
    first, last = _seq_edges(i, nt, nct)
    x = x_ref[...]
    prev8 = jnp.where(first, 0.0, xp_ref[...])
    next8 = jnp.where(last, 0.0, xn_ref[...])
    pad = (MB_CONV - 1) // 2
    acc = x * w_ref[pad:pad + 1, :] + b_ref[...]
    for o in range(-pad, pad + 1):
        if o != 0:
            acc = acc + _shift_rows(x, prev8, next8, o) * w_ref[pad + o:pad + o + 1, :]
    o_ref[...] = _silu(acc)


def _mb_conv(zx, conv_w, conv_b, *, col0, nct):
    b, t, _ = zx.shape
    c = conv_w.shape[1]
    tc = 512
    cb = col0 // tc
    nt = t // TM
    per = TM // 8
    lastb = t // 8 - 1
    w8 = jnp.pad(conv_w, ((0, 8 - conv_w.shape[0]), (0, 0)))
    return pl.pallas_call(
        functools.partial(_mb_conv_body, nt=nt, nct=nct),
        out_shape=jax.ShapeDtypeStruct((b, t, c), F32),
        grid=(b, nt, c // tc),
        in_specs=[pl.BlockSpec((None, TM, tc), lambda b_, i, j: (b_, i, cb + j)),
                  pl.BlockSpec((None, 8, tc), lambda b_, i, j: (b_, jnp.maximum(i * per - 1, 0), cb + j)),
                  pl.BlockSpec((None, 8, tc),
                               lambda b_, i, j: (b_, jnp.minimum((i + 1) * per, lastb), cb + j)),
                  pl.BlockSpec((8, tc), lambda b_, i, j: (0, j)),
                  pl.BlockSpec((1, tc), lambda b_, i, j: (0, j))],
        out_specs=pl.BlockSpec((None, TM, tc), lambda b_, i, j: (b_, i, j)),
        compiler_params=_params(("parallel", "parallel", "parallel")),
        name="mb_conv",
    )(zx, zx, zx, w8, conv_b.reshape(1, c))


def _softplus(x):
    return jnp.maximum(x, 0.0) + jnp.log(1.0 + jnp.exp(-jnp.abs(x)))


def _lane_bcast(col, width):
    return jnp.broadcast_to(col, (col.shape[0], width))


def _ssd_body(x_ref, b_ref, c_ref, dtc_ref, dtr_ref, bc_ref, br_ref, ac_ref, ar_ref, y_ref, st_ref):
    d = pl.program_id(1)
    q = MB_CHUNK
    hpg = MB_HEADS // MB_GROUPS

    @pl.when(pl.program_id(2) == 0)
    def _():
        st_ref[...] = jnp.zeros_like(st_ref)

    sgn = 1 - 2 * d
    dt_c = _softplus(dtc_ref[...] + bc_ref[...])
    dt_r = _softplus(dtr_ref[...] + br_ref[...])
    dta_c = dt_c * (-jnp.exp(ac_ref[...]))
    dta_r = dt_r * (-jnp.exp(ar_ref[...]))
    ii = lax.broadcasted_iota(jnp.int32, (q, q), 0)
    jj = lax.broadcasted_iota(jnp.int32, (q, q), 1)
    mask = (ii - jj) * sgn >= 0
    cum_c = _dot01_left(mask, dta_c)
    cum_r = _dot01_right(dta_r, (jj - ii) * sgn >= 0)
    tot_c = jnp.sum(dta_c, axis=0, keepdims=True)
    ecum_c = jnp.exp(cum_c)
    f_c = jnp.exp(tot_c - cum_c) * dt_c
    etot_c = jnp.exp(tot_c)
    lane = lax.broadcasted_iota(jnp.int32, (q, LANES), 1)
    lo = lane < HEAD
    lane1 = lax.broadcasted_iota(jnp.int32, (1, LANES), 1)

    for g in range(MB_GROUPS):
        bg = b_ref[:, g * MB_STATE:(g + 1) * MB_STATE].astype(BF16)
        cg = c_ref[:, g * MB_STATE:(g + 1) * MB_STATE].astype(BF16)
        gmat = _bdot_nt(cg, bg)
        for pq in range(hpg // 2):
            p = g * (hpg // 2) + pq
            h0 = 2 * p
            xp = x_ref[:, p * LANES:(p + 1) * LANES]
            xpb = xp.astype(BF16)
            ys = []
            for h in (h0, h0 + 1):
                seg = _lane_bcast(cum_c[:, h:h + 1], q) - cum_r[h:h + 1, :]
                wmat = gmat * (jnp.exp(jnp.where(mask, seg, -jnp.inf)) * dt_r[h:h + 1, :])
                ys.append(_bdot(wmat, xpb))
            y_intra = jnp.where(lo, ys[0], ys[1])
            pair = lambda a: jnp.where(lo, _lane_bcast(a[:, h0:h0 + 1], LANES),
                                       _lane_bcast(a[:, h0 + 1:h0 + 2], LANES))
            st = st_ref[p]
            y_ref[:, p * LANES:(p + 1) * LANES] = y_intra + _bdot(cg, st) * pair(ecum_c)
            upd = _bdot_tn(bg, xp * pair(f_c))
            et = jnp.where(lane1 < HEAD, _lane_bcast(etot_c[:, h0:h0 + 1], LANES),
                           _lane_bcast(etot_c[:, h0 + 1:h0 + 2], LANES))
            st_ref[p] = st * et + upd


def _scan_chunk(d, c, n_ctx, n_all):
    rev = jnp.where(c < n_ctx, n_ctx - 1 - c, n_all - 1 - (c - n_ctx))
    return jnp.where(d == 0, c, rev)


def _ssd(xa, dtc, dtr, dt_bias, a_log, *, n_ctx_rows):
    b, t, _ = xa.shape
    q = MB_CHUNK
    nh = MB_HEADS
    di = nh * HEAD
    gn = MB_GROUPS * MB_STATE
    nc = t // q
    ncc = n_ctx_rows // q
    tc = functools.partial(_scan_chunk, n_ctx=ncc, n_all=nc)
    bias_c = dt_bias.reshape(2, 1, nh)
    bias_r = dt_bias.reshape(2, nh, 1)
    a_c = a_log.reshape(2, 1, nh)
    a_r = a_log.reshape(2, nh, 1)
    small_c = pl.BlockSpec((None, 1, nh), lambda b_, d, c: (d, 0, 0))
    small_r = pl.BlockSpec((None, nh, 1), lambda b_, d, c: (d, 0, 0))
    return pl.pallas_call(
        _ssd_body,
        out_shape=jax.ShapeDtypeStruct((2, b, t, di), F32),
        grid=(b, 2, nc),
        in_specs=[pl.BlockSpec((None, q, di), lambda b_, d, c: (b_, tc(d, c), 0)),
                  pl.BlockSpec((None, q, gn), lambda b_, d, c: (b_, tc(d, c), di // gn)),
                  pl.BlockSpec((None, q, gn), lambda b_, d, c: (b_, tc(d, c), di // gn + 1)),
                  pl.BlockSpec((None, None, q, nh), lambda b_, d, c: (d, b_, tc(d, c), 0)),
                  pl.BlockSpec((None, None, nh, q), lambda b_, d, c: (d, b_, 0, tc(d, c))),
                  small_c, small_r, small_c, small_r],
        out_specs=pl.BlockSpec((None, None, q, di), lambda b_, d, c: (d, b_, tc(d, c), 0)),
        scratch_shapes=[pltpu.VMEM((nh // 2, MB_STATE, LANES), F32)],
        compiler_params=_params(("parallel", "parallel", "arbitrary")),
        name="ssd_scan",
    )(xa, xa, xa, dtc, dtr, bias_c, bias_r, a_c, a_r)


def _mb_gate_body(y0_ref, y1_ref, xs_ref, z_ref, dv_ref, g_ref, o_ref):
    y = y0_ref[...] + y1_ref[...] + xs_ref[...] * dv_ref[...]
    o_ref[...] = _rms(y * _silu(z_ref[...]), g_ref[...]).astype(o_ref.dtype)


def _mb_gate(y, xa, zx, dvec, norm_g):
    _, b, t, di = y.shape
    return pl.pallas_call(
        _mb_gate_body,
        out_shape=jax.ShapeDtypeStruct((b, t, di), BF16),
        grid=(b, t // TM),
        in_specs=[_stream2(di, 0), _stream2(di, 1), _stream(di), _stream(di), _vec(di), _vec(di)],
        out_specs=_stream(di),
        compiler_params=_params(("parallel", "parallel")),
        name="mb_gate",
    )(y, y, xa, zx, dvec, norm_g.reshape(1, di))


def _mamba_layer(h, w, *, n_ctx_rows):
    b, t, d = h.shape
    m = b * t
    di = MB_HEADS * HEAD
    xbc = di + 2 * MB_GROUPS * MB_STATE
    h2 = h.reshape(m, d)
    zx = _matmul(h2, w["in_w"], n_cols=di + xbc).reshape(b, t, di + xbc)
    dt_raw = _matmul(h2, w["in_w_dt"], tn=LANES).reshape(b, t, LANES)[..., :2 * MB_HEADS]
    dtc = jnp.moveaxis(dt_raw.reshape(b, t, 2, MB_HEADS), 2, 0)
    dtr = jnp.swapaxes(dtc, 2, 3)
    xa = _mb_conv(zx, w["conv_w"], w["conv_b"], col0=di, nct=n_ctx_rows // TM)
    y = _ssd(xa, dtc, dtr, w["dt_bias"], w["a_log"], n_ctx_rows=n_ctx_rows)
    gated = _mb_gate(y, xa, zx, w["dvec"], w["norm_g"])
    return _matmul(gated.reshape(m, di), w["out_w"]).reshape(b, t, d)


def _rw_mix_body(h_ref, hp_ref, hn_ref, mix_ref, *o_refs, nt, nct):
    i = pl.program_id(1)
    first, last = _seq_edges(i, nt, nct)
    h = h_ref[...]
    prow = jnp.where(first, 0.0, hp_ref[7:8, :])
    nrow = jnp.where(last, 0.0, hn_ref[0:1, :])
    row = lax.broadcasted_iota(jnp.int32, h.shape, 0)
    prev = jnp.where(row == 0, prow, pltpu.roll(h, 1, axis=0))
    nxt = jnp.where(row == TM - 1, nrow, pltpu.roll(h, TM - 1, axis=0))
    xx = 0.5 * (prev + nxt) - h
    for j, o_ref in enumerate(o_refs):
        o_ref[...] = (h + xx * mix_ref[j:j + 1, :]).astype(o_ref.dtype)


def _rw_mix(h, mix, *, nct):
    b, t, d = h.shape
    nt = t // TM
    mix8 = jnp.pad(mix, ((0, 8 - mix.shape[0]), (0, 0)))
    return pl.pallas_call(
        functools.partial(_rw_mix_body, nt=nt, nct=nct),
        out_shape=tuple(jax.ShapeDtypeStruct((b, t, d), BF16) for _ in range(6)),
        grid=(b, nt),
        in_specs=[_stream(d), _halo(d, 8, t, -1), _halo(d, 8, t, +1), _vec(d, 8)],
        out_specs=tuple(_stream(d) for _ in range(6)),
        compiler_params=_params(("parallel", "parallel")),
        name="rw_mix",
    )(h, h, h, mix8)


def _bmm(a, b):
    return jnp.einsum('pij,pjk->pik', a.astype(BF16), b.astype(BF16), preferred_element_type=F32)


def _bmm_nt(a, b):
    return jnp.einsum('pik,pjk->pij', a.astype(BF16), b.astype(BF16), preferred_element_type=F32)


def _lane_pairs(x):
    return jnp.stack([x[:, p * LANES:(p + 1) * LANES] for p in range(x.shape[1] // LANES)], axis=0)


def _rw_chunk_body(r_ref, k_ref, v_ref, wl_ref, al_ref, kkk_ref, ka_ref, rk_ref,
                   rhat_ref, yhat_ref, w_ref, g_ref, gam_ref, bonus_ref):
    c = RW_CHUNK
    c2 = 2 * c
    d = r_ref.shape[1]
    r = r_ref[...]
    k = k_ref[...]
    v = v_ref[...]
    kkr = k * kkk_ref[...]
    kk = kkr / jnp.maximum(jnp.sqrt(_head_sum(kkr * kkr)), 1e-12)
    lws, kds, bbs = [], [], []
    for dr in range(2):
        lws.append(-RW_DECAY_SCALE * _sigmoid(wl_ref[:, dr * d:(dr + 1) * d]))
        a = _sigmoid(al_ref[:, dr * d:(dr + 1) * d])
        kds.append(k * (1.0 + (a - 1.0) * ka_ref[...]))
        bbs.append(kk * a)
    bonus_ref[...] = _head_sum(r * rk_ref[...] * (0.5 * (kds[0] + kds[1]))) * v

    ti = lax.broadcasted_iota(jnp.int32, (c, c), 0)
    si = lax.broadcasted_iota(jnp.int32, (c, c), 1)
    ri = lax.broadcasted_iota(jnp.int32, (c2, c2), 0)
    ci = lax.broadcasted_iota(jnp.int32, (c2, c2), 1)
    same = (ri >= c) == (ci >= c)
    dtm = (ri & (c - 1)) - (ci & (c - 1))
    eye = jnp.where(ri == ci, 1.0, 0.0)
    lo = lax.broadcasted_iota(jnp.int32, (c, LANES), 1) < HEAD

    def stack(x):
        return jnp.concatenate([jnp.where(lo, x, 0.0), jnp.where(lo, 0.0, x)], axis=1)

    def unstack(x2):
        return x2[:, 0:c] + x2[:, c:c2]

    v_p = _lane_pairs(v)
    v2 = stack(v_p)
    npair = v_p.shape[0]
    for dr in range(2):
        sgn = 1 - 2 * dr
        lw = lws[dr]
        cum = _dot01_left((ti - si) * sgn >= 0, lw)
        eg = jnp.exp(cum)
        einv = jnp.exp(-cum)
        at_p = _lane_pairs(-kk * jnp.exp(cum - lw))
        rt_p = _lane_pairs(r * eg)
        kt_p = _lane_pairs(kds[dr] * einv)
        bt_p = _lane_pairs(bbs[dr] * einv)
        gam_ref[dr] = eg[c - 1:c, :] if dr == 0 else eg[0:1, :]
        m_strict = jnp.logical_and(same, dtm * sgn > 0)
        m_incl = jnp.logical_and(same, dtm * sgn >= 0)

        at2 = stack(at_p)
        lhs = jnp.concatenate([at2, stack(rt_p)], axis=1)
        rhs = jnp.concatenate([bt_p, bt_p, kt_p, kt_p], axis=1)
        nn = _bmm_nt(lhs, rhs)
        n_ab = jnp.where(m_strict, nn[:, 0:c2, 0:c2], 0.0)
        n_ak = jnp.where(m_strict, nn[:, 0:c2, c2:2 * c2], 0.0)
        n_qb = jnp.where(m_incl, nn[:, c2:2 * c2, 0:c2], 0.0)
        n_qk = jnp.where(m_incl, nn[:, c2:2 * c2, c2:2 * c2], 0.0)
        tinv = eye + n_ab
        pw = n_ab
        for _ in range(int(math.log2(c)) - 1):
            pw = _bmm(pw, pw)
            tinv = tinv + _bmm(pw, tinv)
        z = _bmm(tinv, jnp.concatenate([at2, _bmm(n_ak, v2)], axis=2))
        x = _bmm(n_qb, z)
        rhat = rt_p + unstack(x[:, :, 0:LANES])
        yhat = unstack(x[:, :, LANES:] + _bmm(n_qk, v2))
        ah = unstack(z[:, :, 0:LANES])
        uh = unstack(z[:, :, LANES:])
        for p in range(npair):
            sl = slice(p * LANES, (p + 1) * LANES)
            rhat_ref[dr, :, sl] = rhat[p].astype(rhat_ref.dtype)
            yhat_ref[dr, :, sl] = yhat[p]
            w_ref[dr, p] = jnp.where(same, _bdot_tn(ah[p], bt_p[p]), 0.0).astype(w_ref.dtype)
            g = _bdot_tn(jnp.concatenate([v_p[p], uh[p]], axis=0), jnp.concatenate([kt_p[p], bt_p[p]], axis=0))
            g_ref[dr, p] = jnp.where(same, g, 0.0)


def _rw_chunks(r, k, v, wl, al, k_k, k_a, r_k):
    b, t, d = r.shape
    c = RW_CHUNK
    nc = t // c
    npair = d // LANES
    one = pl.BlockSpec((None, c, d), lambda b_, ch: (b_, ch, 0))
    wide = pl.BlockSpec((None, c, 2 * d), lambda b_, ch: (b_, ch, 0))
    two = pl.BlockSpec((2, None, c, d), lambda b_, ch: (0, b_, ch, 0))
    mats = pl.BlockSpec((2, None, None, npair, LANES, LANES), lambda b_, ch: (0, b_, ch, 0, 0, 0))
    gam = pl.BlockSpec((2, None, None, 1, d), lambda b_, ch: (0, b_, ch, 0, 0))
    vec = pl.BlockSpec((1, d), lambda b_, ch: (0, 0))
    return pl.pallas_call(
        _rw_chunk_body,
        out_shape=(jax.ShapeDtypeStruct((2, b, t, d), BF16), jax.ShapeDtypeStruct((2, b, t, d), F32),
                   jax.ShapeDtypeStruct((2, b, nc, npair, LANES, LANES), BF16),
                   jax.ShapeDtypeStruct((2, b, nc, npair, LANES, LANES), F32),
                   jax.ShapeDtypeStruct((2, b, nc, 1, d), F32),
                   jax.ShapeDtypeStruct((b, t, d), F32)),
        grid=(b, nc),
        in_specs=[one, one, one, wide, wide, vec, vec, vec],
        out_specs=(two, two, mats, mats, gam, one),
        compiler_params=_params(("parallel", "parallel")),
        name="rw_chunks",
    )(r, k, v, wl, al, k_k.reshape(1, d), k_a.reshape(1, d), r_k.reshape(1, d))


def _rw_state_body(rhat_ref, yhat_ref, w_ref, g_ref, gam_ref, y_ref, s_ref):
    @pl.when(pl.program_id(2) == 0)
    def _():
        s_ref[...] = jnp.zeros_like(s_ref)

    s = s_ref[...]
    sb = s.astype(BF16)
    y = _bmm_nt(_lane_pairs(rhat_ref[...]), sb)
    for p in range(s.shape[0]):
        sl = slice(p * LANES, (p + 1) * LANES)
        y_ref[:, sl] = y[p] + yhat_ref[:, sl]
    gam = _lane_pairs(gam_ref[...])
    s_ref[...] = (s + _bmm(sb, w_ref[...]) + g_ref[...]) * gam


def _rw_state(rhat, yhat, wm, gm, gam, *, n_ctx_rows):
    _, b, t, d = rhat.shape
    c = RW_CHUNK
    nc = t // c
    npair = d // LANES
    tc = functools.partial(_scan_chunk, n_ctx=n_ctx_rows // c, n_all=nc)
    two = pl.BlockSpec((None, None, c, d), lambda b_, dr, ch: (dr, b_, tc(dr, ch), 0))
    mats = pl.BlockSpec((None, None, None, npair, LANES, LANES), lambda b_, dr, ch: (dr, b_, tc(dr, ch), 0, 0, 0))
    gsp = pl.BlockSpec((None, None, None, 1, d), lambda b_, dr, ch: (dr, b_, tc(dr, ch), 0, 0))
    return pl.pallas_call(
        _rw_state_body,
        out_shape=jax.ShapeDtypeStruct((2, b, t, d), F32),
        grid=(b, 2, nc),
        in_specs=[two, two, mats, mats, gsp],
        out_specs=two,
        scratch_shapes=[pltpu.VMEM((npair, LANES, LANES), F32)],
        compiler_params=_params(("parallel", "parallel", "arbitrary")),
        name="rw_state",
    )(rhat, yhat, wm, gm, gam)


def _rw_out_body(y0_ref, y1_ref, bonus_ref, g_ref, lg_ref, lb_ref, o_ref):
    y = y0_ref[...] + y1_ref[...]
    mu = _head_sum(y) * (1.0 / HEAD)
    yc = y - mu
    var = _head_sum(yc * yc) * (1.0 / HEAD)
    yn = yc * lax.rsqrt(var + RW_LN_EPS) * lg_ref[...] + lb_ref[...]
    o_ref[...] = ((yn + bonus_ref[...]) * g_ref[...]).astype(o_ref.dtype)


def _rw_out(y, bonus, g, ln_g, ln_b):
    b, t, d = bonus.shape
    return pl.pallas_call(
        _rw_out_body,
        out_shape=jax.ShapeDtypeStruct((b, t, d), BF16),
        grid=(b, t // TM),
        in_specs=[_stream2(d, 0), _stream2(d, 1), _stream(d), _stream(d), _vec(d), _vec(d)],
        out_specs=_stream(d),
        compiler_params=_params(("parallel", "parallel")),
        name="rw_out",
    )(y, y, bonus, g, ln_g.reshape(1, d), ln_b.reshape(1, d))


def _rwkv_layer(h, w, *, n_ctx_rows):
    b, t, d = h.shape
    m = b * t
    xr, xw, xk, xv, xa, xg = [a.reshape(m, d) for a in _rw_mix(h, w["mix"], nct=n_ctx_rows // TM)]
    r = _matmul(xr, w["r_w"]).reshape(b, t, d)
    k = _matmul(xk, w["k_w"]).reshape(b, t, d)
    v = _matmul(xv, w["v_w"]).reshape(b, t, d)
    g = _lora(xg, w["g1"], w["g2"], jnp.zeros((1, d), F32), "sigmoid").reshape(b, t, d)
    wl = _lora(xw, w["w1"], w["w2"], w["w0"], "tanh").reshape(b, t, 2 * d)
    al = _lora(xa, w["a1"], w["a2"], w["a0"], "none").reshape(b, t, 2 * d)
    rhat, yhat, wm, gm, gam, bonus = _rw_chunks(r, k, v, wl, al, w["k_k"], w["k_a"], w["r_k"])
    y = _rw_state(rhat, yhat, wm, gm, gam, n_ctx_rows=n_ctx_rows)
    o = _rw_out(y, bonus, g, w["ln_g"], w["ln_b"])
    return _matmul(o.reshape(m, d), w["out_w"]).reshape(b, t, d)


def _pool_body(h_ref, hp_ref, hn_ref, w_ref, sc_ref, o_ref, *, nt, nct):
    i = pl.program_id(1)
    first, last = _seq_edges(i, nt, nct)
    h = h_ref[...]
    halo = POOL_HALO
    ext = jnp.concatenate([jnp.where(first, 0.0, hp_ref[...]), h, jnp.where(last, 0.0, hn_ref[...])], axis=0)
    seq_start = jnp.where(i < nct, 0, nct)
    seq_len = jnp.where(i < nct, nct, nt - nct) * TM
    gw = h.shape[1] // len(POOL_WINDOWS)
    pos = (i - seq_start) * TM + lax.broadcasted_iota(jnp.int32, (TM, gw), 0)
    tr = lax.broadcasted_iota(jnp.int32, (TM, TM + 2 * halo), 0)
    er = lax.broadcasted_iota(jnp.int32, (TM, TM + 2 * halo), 1)
    for gi, win in enumerate(POOL_WINDOWS):
        half = win // 2
        band = jnp.logical_and(er >= tr + halo - half, er < tr + halo + half)
        cols = slice(gi * gw, (gi + 1) * gw)
        wsum = _dot01_left(band, ext[:, cols])
        cnt = (jnp.minimum(pos + half, seq_len) - jnp.maximum(pos - half, 0)).astype(F32)
        pooled = wsum / cnt - h[:, cols]
        o_ref[:, cols] = _bdot(pooled, w_ref[gi]) * sc_ref[:, cols]


def _pool_layer(h, pl_w, scale, *, nct):
    b, t, d = h.shape
    nt = t // TM
    ng, gw, _ = pl_w.shape
    return pl.pallas_call(
        functools.partial(_pool_body, nt=nt, nct=nct),
        out_shape=jax.ShapeDtypeStruct((b, t, d), F32),
        grid=(b, nt),
        in_specs=[_stream(d), _halo(d, POOL_HALO, t, -1), _halo(d, POOL_HALO, t, +1),
                  pl.BlockSpec((ng, gw, gw), lambda b_, i: (0, 0, 0)), _vec(d)],
        out_specs=_stream(d),
        compiler_params=_params(("parallel", "parallel")),
        name="pool_mixer",
    )(h, h, h, pl_w, scale.reshape(1, d))


def _at_prep_body(qkv_ref, qg_ref, kg_ref, cos_ref, sin_ref, q_ref, k_ref, v_ref, *, scale):
    nq = q_ref.shape[1]
    nk = AT_KV_HEADS * HEAD
    cos = cos_ref[...]
    sin = sin_ref[...]
    lane = lax.broadcasted_iota(jnp.int32, cos.shape, 1)
    up = (lane & 31) < 16
    lo = lane < HEAD

    def norm_rope(x, g):
        xn = x * lax.rsqrt(_head_sum(x * x) * (1.0 / HEAD) + NORM_EPS) * g
        swapped = jnp.where(up, pltpu.roll(xn, LANES - 16, axis=1), pltpu.roll(xn, 16, axis=1))
        return xn * cos + swapped * sin

    for s in range(nq // LANES):
        sl = slice(s * LANES, (s + 1) * LANES)
        q_ref[:, sl] = (norm_rope(qkv_ref[:, sl], qg_ref[...]) * scale).astype(q_ref.dtype)
    for s in range(nk // LANES):
        kslab = norm_rope(qkv_ref[:, nq + s * LANES:nq + (s + 1) * LANES], kg_ref[...])
        vslab = qkv_ref[:, nq + nk + s * LANES:nq + nk + (s + 1) * LANES]
        krolled = pltpu.roll(kslab, HEAD, axis=1)
        k_ref[2 * s] = jnp.where(lo, kslab, krolled).astype(k_ref.dtype)
        k_ref[2 * s + 1] = jnp.where(lo, krolled, kslab).astype(k_ref.dtype)
        v_ref[2 * s] = jnp.where(lo, vslab, 1.0).astype(v_ref.dtype)
        v_ref[2 * s + 1] = jnp.where(lo, pltpu.roll(vslab, HEAD, axis=1), 1.0).astype(v_ref.dtype)


def _at_prep(qkv, q_g, k_g, cos, sin):
    b, t, _ = qkv.shape
    nq = AT_HEADS * HEAD
    width = qkv.shape[2]
    kvs = jax.ShapeDtypeStruct((b, AT_KV_HEADS, t, LANES), BF16)
    kv_spec = pl.BlockSpec((None, AT_KV_HEADS, TM, LANES), lambda b_, i: (b_, 0, i, 0))
    tab = pl.BlockSpec((TM, LANES), lambda b_, i: (i, 0))
    tile2 = lambda g: jnp.tile(g.reshape(1, HEAD), (1, LANES // HEAD))
    return pl.pallas_call(
        functools.partial(_at_prep_body, scale=HEAD ** -0.5 * math.log2(math.e)),
        out_shape=(jax.ShapeDtypeStruct((b, t, nq), BF16), kvs, kvs),
        grid=(b, t // TM),
        in_specs=[_stream(width), _vec(LANES), _vec(LANES), tab, tab],
        out_specs=(_stream(nq), kv_spec, kv_spec),
        compiler_params=_params(("parallel", "parallel")),
        name="at_prep",
    )(qkv, tile2(q_g), tile2(k_g), cos, sin)


def _flash_body(q_ref, k_ref, v_ref, o_ref, sa_ref, sb_ref, *, tk):
    tq = q_ref.shape[0]
    n = k_ref.shape[0] // tk
    lane = lax.broadcasted_iota(jnp.int32, (tq, LANES), 1)
    lo = lane < HEAD
    q = q_ref[...]
    zero = jnp.zeros_like(q)
    qs = jnp.concatenate([jnp.where(lo, q, zero), jnp.where(lo, zero, q)], axis=0)

    def scores(j, dst_ref):
        start = pl.multiple_of(j * tk, tk)
        dst_ref[...] = _bdot_nt(qs, k_ref[pl.ds(start, tk), :])

    def absorb(j, src_ref, carry):
        m, acc = carry
        start = pl.multiple_of(j * tk, tk)
        s = src_ref[...]
        m_new = jnp.maximum(m, jnp.max(s, axis=-1, keepdims=True))
        p = jnp.exp2(s - m_new)
        acc = jnp.exp2(m - m_new) * acc + _bdot(p, v_ref[pl.ds(start, tk), :])
        return m_new, acc

    def two_chunks(i, carry):
        j = 2 * i
        scores(j + 1, sb_ref)
        carry = absorb(j, sa_ref, carry)
        scores(j + 2, sa_ref)
        return absorb(j + 1, sb_ref, carry)

    carry = (jnp.full((2 * tq, 1), -jnp.inf, F32), jnp.zeros((2 * tq, LANES), F32))
    scores(0, sa_ref)
    carry = lax.fori_loop(0, (n - 1) // 2, two_chunks, carry)
    if n % 2 == 0:
        scores(n - 1, sb_ref)
        carry = absorb(n - 2, sa_ref, carry)
        carry = absorb(n - 1, sb_ref, carry)
    else:
        carry = absorb(n - 1, sa_ref, carry)
    acc = carry[1]
    o2 = acc / pltpu.roll(acc, HEAD, axis=1)
    o_ref[...] = jnp.where(lo, o2[0:tq], pltpu.roll(o2[tq:2 * tq], HEAD, axis=1)).astype(o_ref.dtype)


def _flash(q, k2, v2, *, n_ctx_rows):
    b, t, nq = q.shape
    tq = TM
    tl = t - n_ctx_rows
    off = n_ctx_rows // tq
    tk = _pick(t, (768, 512, 256, 128))
    npair = nq // LANES
    hp = AT_HEADS // AT_KV_HEADS // 2
    kv = pl.BlockSpec((None, None, t, LANES), lambda b_, p, i: (b_, p // hp, 0, 0))
    return pl.pallas_call(
        functools.partial(_flash_body, tk=tk),
        out_shape=jax.ShapeDtypeStruct((b, tl, nq), BF16),
        grid=(b, npair, tl // tq),
        in_specs=[pl.BlockSpec((None, tq, LANES), lambda b_, p, i: (b_, i + off, p)), kv, kv],
        out_specs=pl.BlockSpec((None, tq, LANES), lambda b_, p, i: (b_, i, p)),
        scratch_shapes=[pltpu.VMEM((2 * tq, tk), F32), pltpu.VMEM((2 * tq, tk), F32)],
        compiler_params=_params(("parallel", "parallel", "parallel")),
        name="flash_gqa",
    )(q, k2, v2)


def _rope_tables(n_ctx_rows, seq):
    quarter = HEAD // 4
    inv = ROPE_THETA ** (-jnp.arange(quarter, dtype=F32) / quarter)
    rows = jnp.repeat(jnp.arange(seq // GRID_W, dtype=jnp.int32), GRID_W).astype(F32)
    cols = (jnp.arange(seq, dtype=jnp.int32) % GRID_W).astype(F32)
    ar = rows[:, None] * inv
    ac = cols[:, None] * inv
    cos = jnp.concatenate([jnp.cos(ar), jnp.cos(ar), jnp.cos(ac), jnp.cos(ac)], axis=1)
    sin = jnp.concatenate([-jnp.sin(ar), jnp.sin(ar), -jnp.sin(ac), jnp.sin(ac)], axis=1)
    cos = jnp.concatenate([jnp.ones((n_ctx_rows, HEAD), F32), cos], axis=0)
    sin = jnp.concatenate([jnp.zeros((n_ctx_rows, HEAD), F32), sin], axis=0)
    return jnp.tile(cos, (1, LANES // HEAD)), jnp.tile(sin, (1, LANES // HEAD))


def _attn_layer(h, w, *, n_ctx_rows):
    b, t, d = h.shape
    qkv = _matmul(h.reshape(b * t, d), w["qkv_w"]).reshape(b, t, -1)
    cos, sin = _rope_tables(n_ctx_rows, t - n_ctx_rows)
    q, k2, v2 = _at_prep(qkv, w["q_g"], w["k_g"], cos, sin)
    o = _flash(q, k2, v2, n_ctx_rows=n_ctx_rows)
    tl = t - n_ctx_rows
    return _matmul(o.reshape(b * tl, -1), w["out_w"]).reshape(b, tl, d)


def _ffn(h2, w_in, w_out):
    return _matmul(_swiglu_in(h2, w_in), w_out)


MOE_ROWS = 512
MOE_TOK = 512


def _moe_rank_body(sel_ref, rank_ref, rankt_ref, selt_ref, tot_ref, carry_ref, carryt_ref):
    @pl.when(pl.program_id(0) == 0)
    def _():
        carry_ref[...] = jnp.zeros_like(carry_ref)
        carryt_ref[...] = jnp.zeros_like(carryt_ref)

    sel = sel_ref[...]
    tb = sel.shape[0]
    ri = lax.broadcasted_iota(jnp.int32, (tb, tb), 0)
    ci = lax.broadcasted_iota(jnp.int32, (tb, tb), 1)
    rank_ref[...] = _bdot(_as01(ri > ci), sel) + carry_ref[0:1, :]
    pick = lax.broadcasted_iota(jnp.int32, (8, LANES), 0) == lax.broadcasted_iota(jnp.int32, (8, LANES), 1)
    selt = _bdot_nt(_as01(pick), sel)
    selt_ref[...] = selt
    rankt_ref[...] = _bdot(selt, _as01(ri < ci)) + carryt_ref[:, 0:1]
    carry_ref[...] = carry_ref[...] + jnp.sum(sel, axis=0, keepdims=True)
    carryt_ref[...] = carryt_ref[...] + jnp.sum(selt, axis=1, keepdims=True)
    tot_ref[...] = carry_ref[...]


def _moe_rank(sel):
    m = sel.shape[0]
    tb = MOE_TOK
    return pl.pallas_call(
        _moe_rank_body,
        out_shape=(jax.ShapeDtypeStruct((m, LANES), F32), jax.ShapeDtypeStruct((8, m), F32),
                   jax.ShapeDtypeStruct((8, m), F32), jax.ShapeDtypeStruct((8, LANES), F32)),
        grid=(m // tb,),
        in_specs=[pl.BlockSpec((tb, LANES), lambda i: (i, 0))],
        out_specs=(pl.BlockSpec((tb, LANES), lambda i: (i, 0)), pl.BlockSpec((8, tb), lambda i: (0, i)),
                   pl.BlockSpec((8, tb), lambda i: (0, i)), pl.BlockSpec((8, LANES), lambda i: (0, 0))),
        scratch_shapes=[pltpu.VMEM((8, LANES), F32), pltpu.VMEM((8, LANES), F32)],
        compiler_params=_params(("arbitrary",)),
        name="moe_rank",
    )(sel)


def _moe_plan(sel, rank, rankt, selt, tot, m):
    tr, tb, ne = MOE_ROWS, MOE_TOK, N_EXPERTS
    nb = m // tb
    nt = 2 * m // tr + ne
    i32 = jnp.int32
    cnt = tot[0, :ne].astype(i32)
    tile_start = jnp.concatenate([jnp.zeros((1,), i32), jnp.cumsum((cnt + tr - 1) // tr)])
    off = tile_start[:ne] * tr
    n_tiles = tile_start[ne]
    count_le = lambda ends, x: jnp.sum(ends[None, :] <= x[:, None], axis=1).astype(i32)
    tiles = jnp.arange(nt, dtype=i32)
    tile_valid = tiles < n_tiles
    tile_exp = jnp.minimum(count_le(tile_start[1:], tiles), ne - 1)
    last_exp = tile_exp[jnp.maximum(n_tiles - 1, 0)]
    tile_exp = jnp.where(tile_valid, tile_exp, last_exp)
    offp = jnp.pad(off.astype(F32), (0, LANES - ne))
    posmat = jnp.where(sel > 0, rank + offp[None, :], -1.0)
    post = jnp.where(selt > 0, rankt + jnp.pad(off.astype(F32), (0, 8 - ne))[:, None], -1.0)
    blkcum = jnp.concatenate([rank[::tb, :ne].astype(i32).T, cnt[:, None]], axis=1)

    k_lo = tiles * tr - off[tile_exp]
    k_hi = jnp.minimum(cnt[tile_exp], k_lo + tr) - 1
    ends = blkcum[tile_exp][:, 1:]
    lo_blk = jnp.minimum(jnp.sum(ends <= k_lo[:, None], axis=1), nb - 1).astype(i32)
    hi_blk = jnp.minimum(jnp.sum(ends <= k_hi[:, None], axis=1), nb - 1).astype(i32)
    span = jnp.where(tile_valid, hi_blk - lo_blk + 1, 1)
    lo_blk = jnp.where(tile_valid, lo_blk, 0)
    g_end = jnp.cumsum(span)
    g_start = g_end - span
    ns = nt + ne * nb
    steps = jnp.arange(ns, dtype=i32)
    g_tile = jnp.minimum(count_le(g_end, steps), nt - 1)
    g_valid = (steps < g_end[-1]).astype(i32)
    g_blk = jnp.clip(lo_blk[g_tile] + steps - g_start[g_tile], 0, nb - 1).astype(i32)
    g_first = (steps == g_start[g_tile]).astype(i32)
    gather = (g_tile, g_blk, tile_exp[g_tile], g_first, g_valid)

    r_lo = off[:, None] + blkcum[:, :-1]
    r_hi = off[:, None] + blkcum[:, 1:] - 1
    t_lo = (r_lo // tr).T.reshape(-1)
    n_t = jnp.where(r_hi >= r_lo, r_hi // tr - r_lo // tr + 1, 0).T.reshape(-1)
    c_end = jnp.cumsum(n_t)
    c_start = c_end - n_t
    idx = jnp.minimum(count_le(c_end, steps), nb * ne - 1)
    c_valid = (steps < c_end[-1]).astype(i32)
    c_blk = idx // ne
    c_tile = jnp.clip(t_lo[idx] + steps - c_start[idx], 0, nt - 1).astype(i32)
    c_first = (steps == c_start[c_blk * ne]).astype(i32)
    combine = (c_blk, c_tile, idx % ne, c_first, c_valid)
    return dict(nt=nt, ns=ns, tile_exp=tile_exp, n_tiles=n_tiles.reshape(1), posmat=posmat, post=post,
                gather=gather, combine=combine)


def _moe_gather_body(g_tile, g_blk, g_exp, g_first, g_valid, h_ref, post_ref, xs_ref):
    s = pl.program_id(0)

    @pl.when(g_valid[s] == 1)
    def _():
        tr, tb = xs_ref.shape[0], h_ref.shape[0]
        rows = (lax.broadcasted_iota(jnp.int32, (tr, tb), 0) + g_tile[s] * tr).astype(F32)
        onehot = post_ref[pl.ds(g_exp[s], 1), :] == rows
        x = _bdot(_as01(onehot), h_ref[...]).astype(xs_ref.dtype)

        @pl.when(g_first[s] == 1)
        def _():
            xs_ref[...] = x

        @pl.when(g_first[s] == 0)
        def _():
            xs_ref[...] = xs_ref[...] + x


def _moe_gather(h2, post, plan):
    m, d = h2.shape
    tr, tb = MOE_ROWS, MOE_TOK
    grid_spec = pltpu.PrefetchScalarGridSpec(
        num_scalar_prefetch=5, grid=(plan["ns"],),
        in_specs=[pl.BlockSpec((tb, d), lambda s, gt, gb, ge, gf, gv: (gb[s], 0)),
                  pl.BlockSpec((8, tb), lambda s, gt, gb, ge, gf, gv: (0, gb[s]))],
        out_specs=pl.BlockSpec((tr, d), lambda s, gt, gb, ge, gf, gv: (gt[s], 0)))
    return pl.pallas_call(
        _moe_gather_body,
        out_shape=jax.ShapeDtypeStruct((plan["nt"] * tr, d), BF16),
        grid_spec=grid_spec,
        compiler_params=_params(("arbitrary",)),
        name="moe_gather",
    )(*plan["gather"], h2, post)


def _moe_expert_body(t_exp, n_tiles, xs_ref, wg_ref, wu_ref, wo_ref, y_ref, acc_ref):
    j = pl.program_id(0)
    f = pl.program_id(1)

    @pl.when(f == 0)
    def _():
        acc_ref[...] = jnp.zeros_like(acc_ref)

    @pl.when(j < n_tiles[0])
    def _():
        x = xs_ref[...]
        act = _silu(_bdot(x, wg_ref[...])) * _bdot(x, wu_ref[...])
        acc_ref[...] += _bdot(act, wo_ref[...])

    @pl.when(f == pl.num_programs(1) - 1)
    def _():
        y_ref[...] = acc_ref[...].astype(y_ref.dtype)


def _moe_experts(xs, w_in, w_out, layer, plan):
    rows, d = xs.shape
    tr = MOE_ROWS
    f = w_in.shape[3] // 2
    fk = _pick(f, (512, 256, 128))
    nf = f // fk
    hold = lambda j, fi, nt: jnp.where(j < nt[0], fi, nf - 1)
    grid_spec = pltpu.PrefetchScalarGridSpec(
        num_scalar_prefetch=2, grid=(rows // tr, nf),
        in_specs=[pl.BlockSpec((tr, d), lambda j, fi, te, nt: (j, 0)),
                  pl.BlockSpec((None, None, d, fk), lambda j, fi, te, nt: (layer, te[j], 0, hold(j, fi, nt))),
                  pl.BlockSpec((None, None, d, fk), lambda j, fi, te, nt: (layer, te[j], 0, hold(j, fi, nt) + nf)),
                  pl.BlockSpec((None, None, fk, d), lambda j, fi, te, nt: (layer, te[j], hold(j, fi, nt), 0))],
        out_specs=pl.BlockSpec((tr, d), lambda j, fi, te, nt: (j, 0)),
        scratch_shapes=[pltpu.VMEM((tr, d), F32)])
    return pl.pallas_call(
        _moe_expert_body,
        out_shape=jax.ShapeDtypeStruct((rows, d), BF16),
        grid_spec=grid_spec,
        compiler_params=_params(("arbitrary", "arbitrary")),
        name="moe_experts",
    )(plan["tile_exp"], plan["n_tiles"], xs, w_in, w_in, w_out)


def _moe_combine_body(c_blk, c_tile, c_exp, c_first, c_valid, pos_ref, gates_ref, y_ref, o_ref):
    s = pl.program_id(0)

    @pl.when(c_valid[s] == 1)
    def _():
        tb, tr = o_ref.shape[0], y_ref.shape[0]
        lane = lax.broadcasted_iota(jnp.int32, pos_ref.shape, 1)
        mine = lane == c_exp[s]
        pos_e = jnp.sum(jnp.where(mine, pos_ref[...], 0.0), axis=-1, keepdims=True)
        gate = jnp.sum(jnp.where(mine, gates_ref[...], 0.0), axis=-1, keepdims=True)
        rows = (lax.broadcasted_iota(jnp.int32, (tb, tr), 1) + c_tile[s] * tr).astype(F32)
        y = gate * jnp.dot(_as01(pos_e == rows), y_ref[...], preferred_element_type=F32)

        @pl.when(c_first[s] == 1)
        def _():
            o_ref[...] = y

        @pl.when(c_first[s] == 0)
        def _():
            o_ref[...] = o_ref[...] + y


def _moe_combine(posmat, gates, y, plan):
    m = posmat.shape[0]
    d = y.shape[1]
    tr, tb = MOE_ROWS, MOE_TOK
    grid_spec = pltpu.PrefetchScalarGridSpec(
        num_scalar_prefetch=5, grid=(plan["ns"],),
        in_specs=[pl.BlockSpec((tb, LANES), lambda s, cb, ct, ce, cf, cv: (cb[s], 0)),
                  pl.BlockSpec((tb, LANES), lambda s, cb, ct, ce, cf, cv: (cb[s], 0)),
                  pl.BlockSpec((tr, d), lambda s, cb, ct, ce, cf, cv: (ct[s], 0))],
        out_specs=pl.BlockSpec((tb, d), lambda s, cb, ct, ce, cf, cv: (cb[s], 0)))
    return pl.pallas_call(
        _moe_combine_body,
        out_shape=jax.ShapeDtypeStruct((m, d), F32),
        grid_spec=grid_spec,
        compiler_params=_params(("arbitrary",)),
        name="moe_combine",
    )(*plan["combine"], posmat, gates, y)


def _moe(h2, gates, sel, w_in, w_out, layer):
    m = h2.shape[0]
    rank, rankt, selt, tot = _moe_rank(sel)
    plan = _moe_plan(sel, rank, rankt, selt, tot, m)
    xs = _moe_gather(h2, plan["post"], plan)
    y = _moe_experts(xs, w_in, w_out, layer, plan)
    return _moe_combine(plan["posmat"], gates, y, plan)


def kernel(x, c, ctx, c_ctx, mod_w, mod_b, norm1_g, norm2_g, final_g, mb_in_w, mb_conv_w, mb_conv_b, mb_dt_bias, mb_a_log, mb_d, mb_norm_g, mb_out_w, rw_mix, rw_rkv_w, rw_w0, rw_w1, rw_w2, rw_a0, rw_a1, rw_a2, rw_g1, rw_g2, rw_k_k, rw_k_a, rw_r_k, rw_ln_g, rw_ln_b, rw_out_w, pl_w, pl_scale, at_qkv_w, at_q_g, at_k_g, at_out_w, ff_in_w, ff_out_w, moe_router_w, moe_in_w, moe_out_w):
    b, seq, d = x.shape
    n_ctx = ctx.shape[1]
    depth = mod_w.shape[0]
    t = n_ctx + seq
    nct = n_ctx // TM
    assert depth == 4 and n_ctx % TM == 0 and seq % TM == 0 and b + 1 <= 8
    bf = lambda a: a.astype(BF16)

    svec = jnp.concatenate([c, c_ctx[None, :], jnp.zeros((8 - b - 1, d), F32)], axis=0)
    mods = _mods(svec, bf(mod_w), mod_b)
    mv = jnp.pad(mods[:, :b + 1].reshape(depth, b + 1, 6, d), ((0, 0), (0, 0), (0, 2), (0, 0)))

    xs = jnp.concatenate([ctx, x], axis=1)
    m = b * t
    resid = functools.partial(_resid_norm, nct=nct, x_off=0)

    di = MB_HEADS * HEAD
    xbc = di + 2 * MB_GROUPS * MB_STATE
    mamba_w = dict(in_w=bf(mb_in_w[0]),
                   in_w_dt=bf(jnp.pad(mb_in_w[0][:, di + xbc:], ((0, 0), (0, LANES - 2 * MB_HEADS)))),
                   conv_w=mb_conv_w[0], conv_b=mb_conv_b[0], dt_bias=mb_dt_bias[0], a_log=mb_a_log[0],
                   dvec=jnp.repeat(mb_d[0], HEAD).reshape(1, di), norm_g=mb_norm_g[0], out_w=bf(mb_out_w[0]))
    h = _norm_mod(xs, norm1_g[0], mv, 0, nct, BF16)
    y = _mamba_layer(h, mamba_w, n_ctx_rows=n_ctx)
    xs, h2 = resid(xs, y, mv, norm2_g[0], gate_layer=0, jg=2, mod_layer=0, js=3, jc=4, out_dtype=BF16)
    f = _ffn(h2.reshape(m, d), bf(ff_in_w[0]), bf(ff_out_w[0])).reshape(b, t, d)
    xs, h = resid(xs, f, mv, norm1_g[1], gate_layer=0, jg=5, mod_layer=1, js=0, jc=1, out_dtype=F32)

    blockdiag = lambda u: jnp.concatenate(
        [jnp.concatenate([u[0], jnp.zeros_like(u[0])], axis=1),
         jnp.concatenate([jnp.zeros_like(u[1]), u[1]], axis=1)], axis=0)
    rwkv_w = dict(mix=rw_mix[0], r_w=bf(rw_rkv_w[0, 0]), k_w=bf(rw_rkv_w[0, 1]), v_w=bf(rw_rkv_w[0, 2]),
                  g1=bf(rw_g1[0]), g2=bf(rw_g2[0]),
                  w1=bf(jnp.concatenate([rw_w1[0, 0], rw_w1[0, 1]], axis=1)), w2=bf(blockdiag(rw_w2[0])),
                  w0=rw_w0[0].reshape(1, 2 * d),
                  a1=bf(jnp.concatenate([rw_a1[0, 0], rw_a1[0, 1]], axis=1)), a2=bf(blockdiag(rw_a2[0])),
                  a0=rw_a0[0].reshape(1, 2 * d),
                  k_k=rw_k_k[0], k_a=rw_k_a[0], r_k=rw_r_k[0].reshape(d), ln_g=rw_ln_g[0], ln_b=rw_ln_b[0],
                  out_w=bf(rw_out_w[0]))
    y = _rwkv_layer(h, rwkv_w, n_ctx_rows=n_ctx)
    moe_in_b, moe_out_b = bf(moe_in_w), bf(moe_out_w)
    xs, h2, gates, sel = _resid_norm_router(xs, y, mv, norm2_g[1], moe_router_w[0], gate_layer=1, jg=2,
                                            mod_layer=1, js=3, jc=4, nct=nct, x_off=0)
    f = _moe(h2.reshape(m, d), gates.reshape(m, LANES), sel.reshape(m, LANES), moe_in_b, moe_out_b, 0)
    f = f.reshape(b, t, d)
    xs, h = resid(xs, f, mv, norm1_g[2], gate_layer=1, jg=5, mod_layer=2, js=0, jc=1, out_dtype=F32)

    y = _pool_layer(h, bf(pl_w[0]), pl_scale[0], nct=nct)
    xs, h2 = resid(xs, y, mv, norm2_g[2], gate_layer=2, jg=2, mod_layer=2, js=3, jc=4, out_dtype=BF16)
    f = _ffn(h2.reshape(m, d), bf(ff_in_w[1]), bf(ff_out_w[1])).reshape(b, t, d)
    xs, h = resid(xs, f, mv, norm1_g[3], gate_layer=2, jg=5, mod_layer=3, js=0, jc=1, out_dtype=BF16)

    attn_w = dict(qkv_w=bf(at_qkv_w[0]), q_g=at_q_g[0], k_g=at_k_g[0], out_w=bf(at_out_w[0]))
    y = _attn_layer(h, attn_w, n_ctx_rows=n_ctx)
    xl, h2, gates, sel = _resid_norm_router(xs, y, mv, norm2_g[3], moe_router_w[1], gate_layer=3, jg=2,
                                            mod_layer=3, js=3, jc=4, nct=0, x_off=nct)
    ml = b * seq
    f = _moe(h2.reshape(ml, d), gates.reshape(ml, LANES), sel.reshape(ml, LANES), moe_in_b, moe_out_b, 1)
    return _resid_final(xl, f.reshape(b, seq, d), mv, final_g, gate_layer=3, jg=5)
```

```python
import functools
import math

import jax
import jax.numpy as jnp
from jax import lax
from jax.experimental import pallas as pl
from jax.experimental.pallas import tpu as pltpu

F32 = jnp.float32
BF16 = jnp.bfloat16

NORM_EPS = 1e-6
TM = 256
LANES = 128
HEAD = 64
VMEM_LIMIT = 48 * 1024 * 1024

MB_HEADS = 32
MB_GROUPS = 4
MB_STATE = 128
MB_CHUNK = 128
MB_CONV = 5
RW_CHUNK = 64
RW_DECAY_SCALE = 0.606531
RW_LN_EPS = 64e-5
POOL_WINDOWS = (2, 4, 8, 16)
POOL_HALO = 16
AT_HEADS = 16
AT_KV_HEADS = 4
ROPE_THETA = 10000.0
GRID_W = 64
N_EXPERTS = 8


def _params(sem, vmem=VMEM_LIMIT):
    return pltpu.CompilerParams(dimension_semantics=sem, vmem_limit_bytes=vmem)


def _bdot(a, b):
    return jnp.dot(a.astype(BF16), b.astype(BF16), preferred_element_type=F32)


def _bdot_nt(a, b):
    return lax.dot_general(a.astype(BF16), b.astype(BF16), (((1,), (1,)), ((), ())),
                           preferred_element_type=F32)


def _bdot_tn(a, b):
    return lax.dot_general(a.astype(BF16), b.astype(BF16), (((0,), (0,)), ((), ())),
                           preferred_element_type=F32)


def _split3(x):
    p0 = x.astype(BF16)
    r1 = x - p0.astype(F32)
    p1 = r1.astype(BF16)
    p2 = (r1 - p1.astype(F32)).astype(BF16)
    return p0, p1, p2


def _as01(mask):
    return jnp.where(mask, 1.0, 0.0).astype(BF16)


def _dot01_left(sel, x):
    sel = _as01(sel)
    p0, p1, p2 = _split3(x)
    return (jnp.dot(sel, p0, preferred_element_type=F32)
            + jnp.dot(sel, p1, preferred_element_type=F32)
            + jnp.dot(sel, p2, preferred_element_type=F32))


def _dot01_right(x, sel):
    sel = _as01(sel)
    p0, p1, p2 = _split3(x)
    return (jnp.dot(p0, sel, preferred_element_type=F32)
            + jnp.dot(p1, sel, preferred_element_type=F32)
            + jnp.dot(p2, sel, preferred_element_type=F32))


def _head_sum(x):
    r = lax.broadcasted_iota(jnp.int32, (LANES, LANES), 0) // HEAD
    c = lax.broadcasted_iota(jnp.int32, (LANES, LANES), 1) // HEAD
    ones_bd = r == c
    slabs = [_dot01_right(x[:, s:s + LANES], ones_bd) for s in range(0, x.shape[1], LANES)]
    return slabs[0] if len(slabs) == 1 else jnp.concatenate(slabs, axis=1)


def _sigmoid(x):
    return 1.0 / (1.0 + jnp.exp(-x))


def _silu(x):
    return x * _sigmoid(x)


def _rms(x, g):
    ms = jnp.mean(x * x, axis=-1, keepdims=True)
    return x * lax.rsqrt(ms + NORM_EPS) * g


def _mods_body(s_ref, w_ref, b_ref, o_ref):
    o_ref[...] = _bdot(_silu(s_ref[...]), w_ref[...]) + b_ref[...]


def _mods(svec, mod_w, mod_b):
    depth, d, n6 = mod_w.shape
    tn = 1536
    return pl.pallas_call(
        _mods_body,
        out_shape=jax.ShapeDtypeStruct((depth, 8, n6), F32),
        grid=(depth, n6 // tn),
        in_specs=[pl.BlockSpec((8, d), lambda l, j: (0, 0)),
                  pl.BlockSpec((None, d, tn), lambda l, j: (l, 0, j)),
                  pl.BlockSpec((None, 1, tn), lambda l, j: (l, 0, j))],
        out_specs=pl.BlockSpec((None, 8, tn), lambda l, j: (l, 0, j)),
        compiler_params=_params(("parallel", "parallel")),
        name="mods",
    )(svec, mod_w, mod_b.reshape(depth, 1, n6))


def _stream(width, off=0, rows=TM):
    return pl.BlockSpec((None, rows, width), lambda b, i: (b, i + off, 0))


def _stream2(width, d, off=0):
    return pl.BlockSpec((None, None, TM, width), lambda b, i: (d, b, i + off, 0))


def _vec(width, rows=1):
    return pl.BlockSpec((rows, width), lambda b, i: (0, 0))


def _mv(layer, nct, nb, d):
    return pl.BlockSpec((None, None, 8, d), lambda b, i: (layer, jnp.where(i < nct, nb, b), 0, 0))


def _halo(width, nrows, t_total, side):
    per = TM // nrows
    last = t_total // nrows - 1
    if side < 0:
        return pl.BlockSpec((None, nrows, width), lambda b, i: (b, jnp.maximum(i * per - 1, 0), 0))
    return pl.BlockSpec((None, nrows, width), lambda b, i: (b, jnp.minimum((i + 1) * per, last), 0))


def _seq_edges(i, nt, nct):
    first = jnp.logical_or(i == 0, i == nct)
    last = jnp.logical_or(i == nct - 1, i == nt - 1)
    return first, last


def _shift_rows(x, prev8, next8, o):
    rows = x.shape[0]
    r8 = lax.broadcasted_iota(jnp.int32, (8, x.shape[1]), 0)
    if o < 0:
        k = -o
        s = pltpu.roll(x, k, axis=0)
        top = jnp.where(r8 < k, pltpu.roll(prev8, k, axis=0), s[0:8])
        return jnp.concatenate([top, s[8:]], axis=0)
    s = pltpu.roll(x, rows - o, axis=0)
    bot = jnp.where(r8 >= 8 - o, pltpu.roll(next8, 8 - o, axis=0), s[rows - 8:])
    return jnp.concatenate([s[:rows - 8], bot], axis=0)


def _norm_mod_body(x_ref, g_ref, mv_ref, h_ref, *, js, jc):
    h = _rms(x_ref[...], g_ref[...]) * (1.0 + mv_ref[jc:jc + 1, :]) + mv_ref[js:js + 1, :]
    h_ref[...] = h.astype(h_ref.dtype)


def _norm_mod(x, g, mv, layer, nct, out_dtype):
    b, t, d = x.shape
    nb = b
    return pl.pallas_call(
        functools.partial(_norm_mod_body, js=0, jc=1),
        out_shape=jax.ShapeDtypeStruct((b, t, d), out_dtype),
        grid=(b, t // TM),
        in_specs=[_stream(d), _vec(d), _mv(layer, nct, nb, d)],
        out_specs=_stream(d),
        compiler_params=_params(("parallel", "parallel")),
        name="norm_mod",
    )(x, g.reshape(1, d), mv)


def _resid_norm_body(x_ref, y_ref, mvg_ref, g_ref, mvm_ref, xo_ref, h_ref, *, jg, js, jc):
    xn = x_ref[...] + mvg_ref[jg:jg + 1, :] * y_ref[...].astype(F32)
    xo_ref[...] = xn
    h = _rms(xn, g_ref[...]) * (1.0 + mvm_ref[jc:jc + 1, :]) + mvm_ref[js:js + 1, :]
    h_ref[...] = h.astype(h_ref.dtype)


def _resid_norm(x, y, mv, g, *, gate_layer, jg, mod_layer, js, jc, nct, x_off, out_dtype):
    b, t, d = y.shape
    return pl.pallas_call(
        functools.partial(_resid_norm_body, jg=jg, js=js, jc=jc),
        out_shape=(jax.ShapeDtypeStruct((b, t, d), F32), jax.ShapeDtypeStruct((b, t, d), out_dtype)),
        grid=(b, t // TM),
        in_specs=[_stream(d, x_off), _stream(d), _mv(gate_layer, nct, b, d), _vec(d),
                  _mv(mod_layer, nct, b, d)],
        out_specs=(_stream(d), _stream(d)),
        compiler_params=_params(("parallel", "parallel")),
        name="resid_norm",
    )(x, y, mv, g.reshape(1, d), mv)


def _top2_gates(logits):
    lane = lax.broadcasted_iota(jnp.int32, logits.shape, 1)
    neg = jnp.float32(-jnp.inf)
    lg = jnp.where(lane < N_EXPERTS, logits, neg)
    v1 = jnp.max(lg, axis=-1, keepdims=True)
    i1 = jnp.min(jnp.where(lg == v1, lane, LANES), axis=-1, keepdims=True)
    lg2 = jnp.where(lane == i1, neg, lg)
    v2 = jnp.max(lg2, axis=-1, keepdims=True)
    i2 = jnp.min(jnp.where(lg2 == v2, lane, LANES), axis=-1, keepdims=True)
    e = jnp.exp(v2 - v1)
    w1 = 1.0 / (1.0 + e)
    w2 = e / (1.0 + e)
    gates = jnp.where(lane == i1, w1, 0.0) + jnp.where(lane == i2, w2, 0.0)
    sel = jnp.where(jnp.logical_or(lane == i1, lane == i2), 1.0, 0.0)
    return gates, sel


def _resid_norm_router_body(x_ref, y_ref, mvg_ref, g_ref, mvm_ref, rw_ref, xo_ref, h_ref, gates_ref, sel_ref,
                            *, jg, js, jc):
    xn = x_ref[...] + mvg_ref[jg:jg + 1, :] * y_ref[...].astype(F32)
    xo_ref[...] = xn
    h = _rms(xn, g_ref[...]) * (1.0 + mvm_ref[jc:jc + 1, :]) + mvm_ref[js:js + 1, :]
    h_ref[...] = h.astype(h_ref.dtype)
    h0, h1, h2 = _split3(h)
    w0, w1, w2 = _split3(rw_ref[...])
    dot = lambda a, c: jnp.dot(a, c, preferred_element_type=F32)
    logits = (dot(h0, w0) + (dot(h0, w1) + dot(h1, w0))
              + (dot(h0, w2) + dot(h1, w1) + dot(h2, w0)))
    gates_ref[...], sel_ref[...] = _top2_gates(logits)


def _resid_norm_router(x, y, mv, g, router_w, *, gate_layer, jg, mod_layer, js, jc, nct, x_off):
    b, t, d = y.shape
    rw = jnp.pad(router_w, ((0, 0), (0, LANES - router_w.shape[1])))
    lanes = jax.ShapeDtypeStruct((b, t, LANES), F32)
    return pl.pallas_call(
        functools.partial(_resid_norm_router_body, jg=jg, js=js, jc=jc),
        out_shape=(jax.ShapeDtypeStruct((b, t, d), F32), jax.ShapeDtypeStruct((b, t, d), BF16), lanes, lanes),
        grid=(b, t // TM),
        in_specs=[_stream(d, x_off), _stream(d), _mv(gate_layer, nct, b, d), _vec(d),
                  _mv(mod_layer, nct, b, d), _vec(LANES, d)],
        out_specs=(_stream(d), _stream(d), _stream(LANES), _stream(LANES)),
        compiler_params=_params(("parallel", "parallel")),
        name="resid_norm_router",
    )(x, y, mv, g.reshape(1, d), mv, rw)


def _resid_final_body(x_ref, y_ref, mvg_ref, g_ref, o_ref, *, jg):
    xn = x_ref[...] + mvg_ref[jg:jg + 1, :] * y_ref[...].astype(F32)
    o_ref[...] = _rms(xn, g_ref[...])


def _resid_final(x, y, mv, g, *, gate_layer, jg):
    b, t, d = y.shape
    return pl.pallas_call(
        functools.partial(_resid_final_body, jg=jg),
        out_shape=jax.ShapeDtypeStruct((b, t, d), F32),
        grid=(b, t // TM),
        in_specs=[_stream(d), _stream(d), _mv(gate_layer, 0, b, d), _vec(d)],
        out_specs=_stream(d),
        compiler_params=_params(("parallel", "parallel")),
        name="resid_final",
    )(x, y, mv, g.reshape(1, d))


def _mm_body(x_ref, w_ref, o_ref):
    o_ref[...] = _bdot(x_ref[...], w_ref[...]).astype(o_ref.dtype)


def _pick(n, prefs):
    for p in prefs:
        if n % p == 0:
            return p
    raise ValueError(f"no tile for {n}")


def _matmul(x, w, *, n_cols=None, out_dtype=F32, tm=None, tn=None):
    m, k = x.shape
    n = n_cols or w.shape[1]
    tm = tm or _pick(m, (1024, 768, 512, 256))
    tn = tn or _pick(n, (1024, 768, 512, 256, 128))
    return pl.pallas_call(
        _mm_body,
        out_shape=jax.ShapeDtypeStruct((m, n), out_dtype),
        grid=(n // tn, m // tm),
        in_specs=[pl.BlockSpec((tm, k), lambda j, i: (i, 0)),
                  pl.BlockSpec((k, tn), lambda j, i: (0, j))],
        out_specs=pl.BlockSpec((tm, tn), lambda j, i: (i, j)),
        compiler_params=_params(("parallel", "parallel")),
        name="matmul",
    )(x, w)


def _swiglu_body(x_ref, wg_ref, wu_ref, o_ref):
    x = x_ref[...]
    o_ref[...] = (_silu(_bdot(x, wg_ref[...])) * _bdot(x, wu_ref[...])).astype(o_ref.dtype)


def _swiglu_in(x, w_in, *, tm=None, tn=None):
    m, k = x.shape
    f = w_in.shape[1] // 2
    tm = tm or _pick(m, (768, 512, 256))
    tn = tn or _pick(f, (1408, 896, 512, 256, 128))
    nf = f // tn
    return pl.pallas_call(
        _swiglu_body,
        out_shape=jax.ShapeDtypeStruct((m, f), BF16),
        grid=(nf, m // tm),
        in_specs=[pl.BlockSpec((tm, k), lambda j, i: (i, 0)),
                  pl.BlockSpec((k, tn), lambda j, i: (0, j)),
                  pl.BlockSpec((k, tn), lambda j, i: (0, j + nf))],
        out_specs=pl.BlockSpec((tm, tn), lambda j, i: (i, j)),
        compiler_params=_params(("parallel", "parallel")),
        name="swiglu_in",
    )(x, w_in, w_in)


def _lora_body(x_ref, a_ref, b_ref, bias_ref, o_ref, *, act):
    t = _bdot(x_ref[...], a_ref[...])
    if act == "tanh":
        t = jnp.tanh(t)
    elif act == "sigmoid":
        t = _sigmoid(t)
    o_ref[...] = _bdot(t, b_ref[...]) + bias_ref[...]


def _lora(x, a, bm, bias, act):
    m, k = x.shape
    r = a.shape[1]
    n = bm.shape[1]
    tm = _pick(m, (512, 256))
    return pl.pallas_call(
        functools.partial(_lora_body, act=act),
        out_shape=jax.ShapeDtypeStruct((m, n), F32),
        grid=(m // tm,),
        in_specs=[pl.BlockSpec((tm, k), lambda i: (i, 0)),
                  pl.BlockSpec((k, r), lambda i: (0, 0)),
                  pl.BlockSpec((r, n), lambda i: (0, 0)),
                  pl.BlockSpec((1, n), lambda i: (0, 0))],
        out_specs=pl.BlockSpec((tm, n), lambda i: (i, 0)),
        compiler_params=_params(("parallel",)),
        name="lora",
    )(x, a, bm, bias)


def _mb_conv_body(x_ref, xp_ref, xn_ref, w_ref, b_ref, o_ref, *, nt, nct):
    i = pl.program_id(1)
    first, last = _seq_edges(i, nt, nct)
    x = x_ref[...]
    prev8 = jnp.where(first, 0.0, xp_ref[...])
    next8 = jnp.where(last, 0.0, xn_ref[...])
    pad = (MB_CONV - 1) // 2
    acc = x * w_ref[pad:pad + 1, :] + b_ref[...]
    for o in range(-pad, pad + 1):
        if o != 0:
            acc = acc + _shift_rows(x, prev8, next8, o) * w_ref[pad + o:pad + o + 1, :]
    o_ref[...] = _silu(acc)


def _mb_conv(zx, conv_w, conv_b, *, col0, nct):
    b, t, _ = zx.shape
    c = conv_w.shape[1]
    tc = 512
    cb = col0 // tc
    nt = t // TM
    per = TM // 8
    lastb = t // 8 - 1
    w8 = jnp.pad(conv_w, ((0, 8 - conv_w.shape[0]), (0, 0)))
    return pl.pallas_call(
        functools.partial(_mb_conv_body, nt=nt, nct=nct),
        out_shape=jax.ShapeDtypeStruct((b, t, c), F32),
        grid=(b, nt, c // tc),
        in_specs=[pl.BlockSpec((None, TM, tc), lambda b_, i, j: (b_, i, cb + j)),
                  pl.BlockSpec((None, 8, tc), lambda b_, i, j: (b_, jnp.maximum(i * per - 1, 0), cb + j)),
                  pl.BlockSpec((None, 8, tc),
                               lambda b_, i, j: (b_, jnp.minimum((i + 1) * per, lastb), cb + j)),
                  pl.BlockSpec((8, tc), lambda b_, i, j: (0, j)),
                  pl.BlockSpec((1, tc), lambda b_, i, j: (0, j))],
        out_specs=pl.BlockSpec((None, TM, tc), lambda b_, i, j: (b_, i, j)),
        compiler_params=_params(("parallel", "parallel", "parallel")),
        name="mb_conv",
    )(zx, zx, zx, w8, conv_b.reshape(1, c))


def _softplus(x):
    return jnp.maximum(x, 0.0) + jnp.log(1.0 + jnp.exp(-jnp.abs(x)))


def _lane_bcast(col, width):
    return jnp.broadcast_to(col, (col.shape[0], width))


def _ssd_body(x_ref, b_ref, c_ref, dtc_ref, dtr_ref, bc_ref, br_ref, ac_ref, ar_ref, y_ref, st_ref):
    d = pl.program_id(1)
    q = MB_CHUNK
    hpg = MB_HEADS // MB_GROUPS

    @pl.when(pl.program_id(2) == 0)
    def _():
        st_ref[...] = jnp.zeros_like(st_ref)

    sgn = 1 - 2 * d
    dt_c = _softplus(dtc_ref[...] + bc_ref[...])
    dt_r = _softplus(dtr_ref[...] + br_ref[...])
    dta_c = dt_c * (-jnp.exp(ac_ref[...]))
    dta_r = dt_r * (-jnp.exp(ar_ref[...]))
    ii = lax.broadcasted_iota(jnp.int32, (q, q), 0)
    jj = lax.broadcasted_iota(jnp.int32, (q, q), 1)
    mask = (ii - jj) * sgn >= 0
    cum_c = _dot01_left(mask, dta_c)
    cum_r = _dot01_right(dta_r, (jj - ii) * sgn >= 0)
    tot_c = jnp.sum(dta_c, axis=0, keepdims=True)
    ecum_c = jnp.exp(cum_c)
    f_c = jnp.exp(tot_c - cum_c) * dt_c
    etot_c = jnp.exp(tot_c)
    lane = lax.broadcasted_iota(jnp.int32, (q, LANES), 1)
    lo = lane < HEAD
    lane1 = lax.broadcasted_iota(jnp.int32, (1, LANES), 1)

    for g in range(MB_GROUPS):
        bg = b_ref[:, g * MB_STATE:(g + 1) * MB_STATE].astype(BF16)
        cg = c_ref[:, g * MB_STATE:(g + 1) * MB_STATE].astype(BF16)
        gmat = _bdot_nt(cg, bg)
        for pq in range(hpg // 2):
            p = g * (hpg // 2) + pq
            h0 = 2 * p
            xp = x_ref[:, p * LANES:(p + 1) * LANES]
            xpb = xp.astype(BF16)
            ys = []
            for h in (h0, h0 + 1):
                seg = _lane_bcast(cum_c[:, h:h + 1], q) - cum_r[h:h + 1, :]
                wmat = gmat * (jnp.exp(jnp.where(mask, seg, -jnp.inf)) * dt_r[h:h + 1, :])
                ys.append(_bdot(wmat, xpb))
            y_intra = jnp.where(lo, ys[0], ys[1])
            pair = lambda a: jnp.where(lo, _lane_bcast(a[:, h0:h0 + 1], LANES),
                                       _lane_bcast(a[:, h0 + 1:h0 + 2], LANES))
            st = st_ref[p]
            y_ref[:, p * LANES:(p + 1) * LANES] = y_intra + _bdot(cg, st) * pair(ecum_c)
            upd = _bdot_tn(bg, xp * pair(f_c))
            et = jnp.where(lane1 < HEAD, _lane_bcast(etot_c[:, h0:h0 + 1], LANES),
                           _lane_bcast(etot_c[:, h0 + 1:h0 + 2], LANES))
            st_ref[p] = st * et + upd


def _scan_chunk(d, c, n_ctx, n_all):
    rev = jnp.where(c < n_ctx, n_ctx - 1 - c, n_all - 1 - (c - n_ctx))
    return jnp.where(d == 0, c, rev)


def _ssd(xa, dtc, dtr, dt_bias, a_log, *, n_ctx_rows):
    b, t, _ = xa.shape
    q = MB_CHUNK
    nh = MB_HEADS
    di = nh * HEAD
    gn = MB_GROUPS * MB_STATE
    nc = t // q
    ncc = n_ctx_rows // q
    tc = functools.partial(_scan_chunk, n_ctx=ncc, n_all=nc)
    bias_c = dt_bias.reshape(2, 1, nh)
    bias_r = dt_bias.reshape(2, nh, 1)
    a_c = a_log.reshape(2, 1, nh)
    a_r = a_log.reshape(2, nh, 1)
    small_c = pl.BlockSpec((None, 1, nh), lambda b_, d, c: (d, 0, 0))
    small_r = pl.BlockSpec((None, nh, 1), lambda b_, d, c: (d, 0, 0))
    return pl.pallas_call(
        _ssd_body,
        out_shape=jax.ShapeDtypeStruct((2, b, t, di), F32),
        grid=(b, 2, nc),
        in_specs=[pl.BlockSpec((None, q, di), lambda b_, d, c: (b_, tc(d, c), 0)),
                  pl.BlockSpec((None, q, gn), lambda b_, d, c: (b_, tc(d, c), di // gn)),
                  pl.BlockSpec((None, q, gn), lambda b_, d, c: (b_, tc(d, c), di // gn + 1)),
                  pl.BlockSpec((None, None, q, nh), lambda b_, d, c: (d, b_, tc(d, c), 0)),
                  pl.BlockSpec((None, None, nh, q), lambda b_, d, c: (d, b_, 0, tc(d, c))),
                  small_c, small_r, small_c, small_r],
        out_specs=pl.BlockSpec((None, None, q, di), lambda b_, d, c: (d, b_, tc(d, c), 0)),
        scratch_shapes=[pltpu.VMEM((nh // 2, MB_STATE, LANES), F32)],
        compiler_params=_params(("parallel", "parallel", "arbitrary")),
        name="ssd_scan",
    )(xa, xa, xa, dtc, dtr, bias_c, bias_r, a_c, a_r)


def _mb_gate_body(y0_ref, y1_ref, xs_ref, z_ref, dv_ref, g_ref, o_ref):
    y = y0_ref[...] + y1_ref[...] + xs_ref[...] * dv_ref[...]
    o_ref[...] = _rms(y * _silu(z_ref[...]), g_ref[...]).astype(o_ref.dtype)


def _mb_gate(y, xa, zx, dvec, norm_g):
    _, b, t, di = y.shape
    return pl.pallas_call(
        _mb_gate_body,
        out_shape=jax.ShapeDtypeStruct((b, t, di), BF16),
        grid=(b, t // TM),
        in_specs=[_stream2(di, 0), _stream2(di, 1), _stream(di), _stream(di), _vec(di), _vec(di)],
        out_specs=_stream(di),
        compiler_params=_params(("parallel", "parallel")),
        name="mb_gate",
    )(y, y, xa, zx, dvec, norm_g.reshape(1, di))


def _mamba_layer(h, w, *, n_ctx_rows):
    b, t, d = h.shape
    m = b * t
    di = MB_HEADS * HEAD
    xbc = di + 2 * MB_GROUPS * MB_STATE
    h2 = h.reshape(m, d)
    zx = _matmul(h2, w["in_w"], n_cols=di + xbc).reshape(b, t, di + xbc)
    dt_raw = _matmul(h2, w["in_w_dt"], tn=LANES).reshape(b, t, LANES)[..., :2 * MB_HEADS]
    dtc = jnp.moveaxis(dt_raw.reshape(b, t, 2, MB_HEADS), 2, 0)
    dtr = jnp.swapaxes(dtc, 2, 3)
    xa = _mb_conv(zx, w["conv_w"], w["conv_b"], col0=di, nct=n_ctx_rows // TM)
    y = _ssd(xa, dtc, dtr, w["dt_bias"], w["a_log"], n_ctx_rows=n_ctx_rows)
    gated = _mb_gate(y, xa, zx, w["dvec"], w["norm_g"])
    return _matmul(gated.reshape(m, di), w["out_w"]).reshape(b, t, d)


def _rw_mix_body(h_ref, hp_ref, hn_ref, mix_ref, *o_refs, nt, nct):
    i = pl.program_id(1)
    first, last = _seq_edges(i, nt, nct)
    h = h_ref[...]
    prow = jnp.where(first, 0.0, hp_ref[7:8, :])
    nrow = jnp.where(last, 0.0, hn_ref[0:1, :])
    row = lax.broadcasted_iota(jnp.int32, h.shape, 0)
    prev = jnp.where(row == 0, prow, pltpu.roll(h, 1, axis=0))
    nxt = jnp.where(row == TM - 1, nrow, pltpu.roll(h, TM - 1, axis=0))
    xx = 0.5 * (prev + nxt) - h
    for j, o_ref in enumerate(o_refs):
        o_ref[...] = (h + xx * mix_ref[j:j + 1, :]).astype(o_ref.dtype)


def _rw_mix(h, mix, *, nct):
    b, t, d = h.shape
    nt = t // TM
    mix8 = jnp.pad(mix, ((0, 8 - mix.shape[0]), (0, 0)))
    return pl.pallas_call(
        functools.partial(_rw_mix_body, nt=nt, nct=nct),
        out_shape=tuple(jax.ShapeDtypeStruct((b, t, d), BF16) for _ in range(6)),
        grid=(b, nt),
        in_specs=[_stream(d), _halo(d, 8, t, -1), _halo(d, 8, t, +1), _vec(d, 8)],
        out_specs=tuple(_stream(d) for _ in range(6)),
        compiler_params=_params(("parallel", "parallel")),
        name="rw_mix",
    )(h, h, h, mix8)


def _bmm(a, b):
    return jnp.einsum('pij,pjk->pik', a.astype(BF16), b.astype(BF16), preferred_element_type=F32)


def _bmm_nt(a, b):
    return jnp.einsum('pik,pjk->pij', a.astype(BF16), b.astype(BF16), preferred_element_type=F32)


def _lane_pairs(x):
    return jnp.stack([x[:, p * LANES:(p + 1) * LANES] for p in range(x.shape[1] // LANES)], axis=0)


def _rw_chunk_body(r_ref, k_ref, v_ref, wl_ref, al_ref, kkk_ref, ka_ref, rk_ref,
                   rhat_ref, yhat_ref, w_ref, g_ref, gam_ref, bonus_ref):
    c = RW_CHUNK
    c2 = 2 * c
    d = r_ref.shape[1]
    r = r_ref[...]
    k = k_ref[...]
    v = v_ref[...]
    kkr = k * kkk_ref[...]
    kk = kkr / jnp.maximum(jnp.sqrt(_head_sum(kkr * kkr)), 1e-12)
    lws, kds, bbs = [], [], []
    for dr in range(2):
        lws.append(-RW_DECAY_SCALE * _sigmoid(wl_ref[:, dr * d:(dr + 1) * d]))
        a = _sigmoid(al_ref[:, dr * d:(dr + 1) * d])
        kds.append(k * (1.0 + (a - 1.0) * ka_ref[...]))
        bbs.append(kk * a)
    bonus_ref[...] = _head_sum(r * rk_ref[...] * (0.5 * (kds[0] + kds[1]))) * v

    ti = lax.broadcasted_iota(jnp.int32, (c, c), 0)
    si = lax.broadcasted_iota(jnp.int32, (c, c), 1)
    ri = lax.broadcasted_iota(jnp.int32, (c2, c2), 0)
    ci = lax.broadcasted_iota(jnp.int32, (c2, c2), 1)
    same = (ri >= c) == (ci >= c)
    dtm = (ri & (c - 1)) - (ci & (c - 1))
    eye = jnp.where(ri == ci, 1.0, 0.0)
    lo = lax.broadcasted_iota(jnp.int32, (c, LANES), 1) < HEAD

    def stack(x):
        return jnp.concatenate([jnp.where(lo, x, 0.0), jnp.where(lo, 0.0, x)], axis=1)

    def unstack(x2):
        return x2[:, 0:c] + x2[:, c:c2]

    v_p = _lane_pairs(v)
    v2 = stack(v_p)
    npair = v_p.shape[0]
    for dr in range(2):
        sgn = 1 - 2 * dr
        lw = lws[dr]
        cum = _dot01_left((ti - si) * sgn >= 0, lw)
        eg = jnp.exp(cum)
        einv = jnp.exp(-cum)
        at_p = _lane_pairs(-kk * jnp.exp(cum - lw))
        rt_p = _lane_pairs(r * eg)
        kt_p = _lane_pairs(kds[dr] * einv)
        bt_p = _lane_pairs(bbs[dr] * einv)
        gam_ref[dr] = eg[c - 1:c, :] if dr == 0 else eg[0:1, :]
        m_strict = jnp.logical_and(same, dtm * sgn > 0)
        m_incl = jnp.logical_and(same, dtm * sgn >= 0)

        at2 = stack(at_p)
        lhs = jnp.concatenate([at2, stack(rt_p)], axis=1)
        rhs = jnp.concatenate([bt_p, bt_p, kt_p, kt_p], axis=1)
        nn = _bmm_nt(lhs, rhs)
        n_ab = jnp.where(m_strict, nn[:, 0:c2, 0:c2], 0.0)
        n_ak = jnp.where(m_strict, nn[:, 0:c2, c2:2 * c2], 0.0)
        n_qb = jnp.where(m_incl, nn[:, c2:2 * c2, 0:c2], 0.0)
        n_qk = jnp.where(m_incl, nn[:, c2:2 * c2, c2:2 * c2], 0.0)
        tinv = eye + n_ab
        pw = n_ab
        for _ in range(int(math.log2(c)) - 1):
            pw = _bmm(pw, pw)
            tinv = tinv + _bmm(pw, tinv)
        z = _bmm(tinv, jnp.concatenate([at2, _bmm(n_ak, v2)], axis=2))
        x = _bmm(n_qb, z)
        rhat = rt_p + unstack(x[:, :, 0:LANES])
        yhat = unstack(x[:, :, LANES:] + _bmm(n_qk, v2))
        ah = unstack(z[:, :, 0:LANES])
        uh = unstack(z[:, :, LANES:])
        for p in range(npair):
            sl = slice(p * LANES, (p + 1) * LANES)
            rhat_ref[dr, :, sl] = rhat[p].astype(rhat_ref.dtype)
            yhat_ref[dr, :, sl] = yhat[p]
            w_ref[dr, p] = jnp.where(same, _bdot_tn(ah[p], bt_p[p]), 0.0).astype(w_ref.dtype)
            g = _bdot_tn(jnp.concatenate([v_p[p], uh[p]], axis=0), jnp.concatenate([kt_p[p], bt_p[p]], axis=0))
            g_ref[dr, p] = jnp.where(same, g, 0.0)


def _rw_chunks(r, k, v, wl, al, k_k, k_a, r_k):
    b, t, d = r.shape
    c = RW_CHUNK
    nc = t // c
    npair = d // LANES
    one = pl.BlockSpec((None, c, d), lambda b_, ch: (b_, ch, 0))
    wide = pl.BlockSpec((None, c, 2 * d), lambda b_, ch: (b_, ch, 0))
    two = pl.BlockSpec((2, None, c, d), lambda b_, ch: (0, b_, ch, 0))
    mats = pl.BlockSpec((2, None, None, npair, LANES, LANES), lambda b_, ch: (0, b_, ch, 0, 0, 0))
    gam = pl.BlockSpec((2, None, None, 1, d), lambda b_, ch: (0, b_, ch, 0, 0))
    vec = pl.BlockSpec((1, d), lambda b_, ch: (0, 0))
    return pl.pallas_call(
        _rw_chunk_body,
        out_shape=(jax.ShapeDtypeStruct((2, b, t, d), BF16), jax.ShapeDtypeStruct((2, b, t, d), F32),
                   jax.ShapeDtypeStruct((2, b, nc, npair, LANES, LANES), BF16),
                   jax.ShapeDtypeStruct((2, b, nc, npair, LANES, LANES), F32),
                   jax.ShapeDtypeStruct((2, b, nc, 1, d), F32),
                   jax.ShapeDtypeStruct((b, t, d), F32)),
        grid=(b, nc),
        in_specs=[one, one, one, wide, wide, vec, vec, vec],
        out_specs=(two, two, mats, mats, gam, one),
        compiler_params=_params(("parallel", "parallel")),
        name="rw_chunks",
    )(r, k, v, wl, al, k_k.reshape(1, d), k_a.reshape(1, d), r_k.reshape(1, d))


def _rw_state_body(rhat_ref, yhat_ref, w_ref, g_ref, gam_ref, y_ref, s_ref):
    @pl.when(pl.program_id(2) == 0)
    def _():
        s_ref[...] = jnp.zeros_like(s_ref)

    s = s_ref[...]
    sb = s.astype(BF16)
    y = _bmm_nt(_lane_pairs(rhat_ref[...]), sb)
    for p in range(s.shape[0]):
        sl = slice(p * LANES, (p + 1) * LANES)
        y_ref[:, sl] = y[p] + yhat_ref[:, sl]
    gam = _lane_pairs(gam_ref[...])
    s_ref[...] = (s + _bmm(sb, w_ref[...]) + g_ref[...]) * gam


def _rw_state(rhat, yhat, wm, gm, gam, *, n_ctx_rows):
    _, b, t, d = rhat.shape
    c = RW_CHUNK
    nc = t // c
    npair = d // LANES
    tc = functools.partial(_scan_chunk, n_ctx=n_ctx_rows // c, n_all=nc)
    two = pl.BlockSpec((None, None, c, d), lambda b_, dr, ch: (dr, b_, tc(dr, ch), 0))
    mats = pl.BlockSpec((None, None, None, npair, LANES, LANES), lambda b_, dr, ch: (dr, b_, tc(dr, ch), 0, 0, 0))
    gsp = pl.BlockSpec((None, None, None, 1, d), lambda b_, dr, ch: (dr, b_, tc(dr, ch), 0, 0))
    return pl.pallas_call(
        _rw_state_body,
        out_shape=jax.ShapeDtypeStruct((2, b, t, d), F32),
        grid=(b, 2, nc),
        in_specs=[two, two, mats, mats, gsp],
        out_specs=two,
        scratch_shapes=[pltpu.VMEM((npair, LANES, LANES), F32)],
        compiler_params=_params(("parallel", "parallel", "arbitrary")),
        name="rw_state",
    )(rhat, yhat, wm, gm, gam)


def _rw_out_body(y0_ref, y1_ref, bonus_ref, g_ref, lg_ref, lb_ref, o_ref):
    y = y0_ref[...] + y1_ref[...]
    mu = _head_sum(y) * (1.0 / HEAD)
    yc = y - mu
    var = _head_sum(yc * yc) * (1.0 / HEAD)
    yn = yc * lax.rsqrt(var + RW_LN_EPS) * lg_ref[...] + lb_ref[...]
    o_ref[...] = ((yn + bonus_ref[...]) * g_ref[...]).astype(o_ref.dtype)


def _rw_out(y, bonus, g, ln_g, ln_b):
    b, t, d = bonus.shape
    return pl.pallas_call(
        _rw_out_body,
        out_shape=jax.ShapeDtypeStruct((b, t, d), BF16),
        grid=(b, t // TM),
        in_specs=[_stream2(d, 0), _stream2(d, 1), _stream(d), _stream(d), _vec(d), _vec(d)],
        out_specs=_stream(d),
        compiler_params=_params(("parallel", "parallel")),
        name="rw_out",
    )(y, y, bonus, g, ln_g.reshape(1, d), ln_b.reshape(1, d))


def _rwkv_layer(h, w, *, n_ctx_rows):
    b, t, d = h.shape
    m = b * t
    xr, xw, xk, xv, xa, xg = [a.reshape(m, d) for a in _rw_mix(h, w["mix"], nct=n_ctx_rows // TM)]
    r = _matmul(xr, w["r_w"]).reshape(b, t, d)
    k = _matmul(xk, w["k_w"]).reshape(b, t, d)
    v = _matmul(xv, w["v_w"]).reshape(b, t, d)
    g = _lora(xg, w["g1"], w["g2"], jnp.zeros((1, d), F32), "sigmoid").reshape(b, t, d)
    wl = _lora(xw, w["w1"], w["w2"], w["w0"], "tanh").reshape(b, t, 2 * d)
    al = _lora(xa, w["a1"], w["a2"], w["a0"], "none").reshape(b, t, 2 * d)
    rhat, yhat, wm, gm, gam, bonus = _rw_chunks(r, k, v, wl, al, w["k_k"], w["k_a"], w["r_k"])
    y = _rw_state(rhat, yhat, wm, gm, gam, n_ctx_rows=n_ctx_rows)
    o = _rw_out(y, bonus, g, w["ln_g"], w["ln_b"])
    return _matmul(o.reshape(m, d), w["out_w"]).reshape(b, t, d)


def _pool_body(h_ref, hp_ref, hn_ref, w_ref, sc_ref, o_ref, *, nt, nct):
    i = pl.program_id(1)
    first, last = _seq_edges(i, nt, nct)
    h = h_ref[...]
    halo = POOL_HALO
    ext = jnp.concatenate([jnp.where(first, 0.0, hp_ref[...]), h, jnp.where(last, 0.0, hn_ref[...])], axis=0)
    seq_start = jnp.where(i < nct, 0, nct)
    seq_len = jnp.where(i < nct, nct, nt - nct) * TM
    gw = h.shape[1] // len(POOL_WINDOWS)
    pos = (i - seq_start) * TM + lax.broadcasted_iota(jnp.int32, (TM, gw), 0)
    tr = lax.broadcasted_iota(jnp.int32, (TM, TM + 2 * halo), 0)
    er = lax.broadcasted_iota(jnp.int32, (TM, TM + 2 * halo), 1)
    for gi, win in enumerate(POOL_WINDOWS):
        half = win // 2
        band = jnp.logical_and(er >= tr + halo - half, er < tr + halo + half)
        cols = slice(gi * gw, (gi + 1) * gw)
        wsum = _dot01_left(band, ext[:, cols])
        cnt = (jnp.minimum(pos + half, seq_len) - jnp.maximum(pos - half, 0)).astype(F32)
        pooled = wsum / cnt - h[:, cols]
        o_ref[:, cols] = _bdot(pooled, w_ref[gi]) * sc_ref[:, cols]


def _pool_layer(h, pl_w, scale, *, nct):
    b, t, d = h.shape
    nt = t // TM
    ng, gw, _ = pl_w.shape
    return pl.pallas_call(
        functools.partial(_pool_body, nt=nt, nct=nct),
        out_shape=jax.ShapeDtypeStruct((b, t, d), F32),
        grid=(b, nt),
        in_specs=[_stream(d), _halo(d, POOL_HALO, t, -1), _halo(d, POOL_HALO, t, +1),
                  pl.BlockSpec((ng, gw, gw), lambda b_, i: (0, 0, 0)), _vec(d)],
        out_specs=_stream(d),
        compiler_params=_params(("parallel", "parallel")),
        name="pool_mixer",
    )(h, h, h, pl_w, scale.reshape(1, d))


def _at_prep_body(qkv_ref, qg_ref, kg_ref, cos_ref, sin_ref, q_ref, k_ref, v_ref, *, scale):
    nq = q_ref.shape[1]
    nk = AT_KV_HEADS * HEAD
    cos = cos_ref[...]
    sin = sin_ref[...]
    lane = lax.broadcasted_iota(jnp.int32, cos.shape, 1)
    up = (lane & 31) < 16
    lo = lane < HEAD

    def norm_rope(x, g):
        xn = x * lax.rsqrt(_head_sum(x * x) * (1.0 / HEAD) + NORM_EPS) * g
        swapped = jnp.where(up, pltpu.roll(xn, LANES - 16, axis=1), pltpu.roll(xn, 16, axis=1))
        return xn * cos + swapped * sin

    for s in range(nq // LANES):
        sl = slice(s * LANES, (s + 1) * LANES)
        q_ref[:, sl] = (norm_rope(qkv_ref[:, sl], qg_ref[...]) * scale).astype(q_ref.dtype)
    for s in range(nk // LANES):
        kslab = norm_rope(qkv_ref[:, nq + s * LANES:nq + (s + 1) * LANES], kg_ref[...])
        vslab = qkv_ref[:, nq + nk + s * LANES:nq + nk + (s + 1) * LANES]
        krolled = pltpu.roll(kslab, HEAD, axis=1)
        k_ref[2 * s] = jnp.where(lo, kslab, krolled).astype(k_ref.dtype)
        k_ref[2 * s + 1] = jnp.where(lo, krolled, kslab).astype(k_ref.dtype)
        v_ref[2 * s] = jnp.where(lo, vslab, 1.0).astype(v_ref.dtype)
        v_ref[2 * s + 1] = jnp.where(lo, pltpu.roll(vslab, HEAD, axis=1), 1.0).astype(v_ref.dtype)


def _at_prep(qkv, q_g, k_g, cos, sin):
    b, t, _ = qkv.shape
    nq = AT_HEADS * HEAD
    width = qkv.shape[2]
    kvs = jax.ShapeDtypeStruct((b, AT_KV_HEADS, t, LANES), BF16)
    kv_spec = pl.BlockSpec((None, AT_KV_HEADS, TM, LANES), lambda b_, i: (b_, 0, i, 0))
    tab = pl.BlockSpec((TM, LANES), lambda b_, i: (i, 0))
    tile2 = lambda g: jnp.tile(g.reshape(1, HEAD), (1, LANES // HEAD))
    return pl.pallas_call(
        functools.partial(_at_prep_body, scale=HEAD ** -0.5 * math.log2(math.e)),
        out_shape=(jax.ShapeDtypeStruct((b, t, nq), BF16), kvs, kvs),
        grid=(b, t // TM),
        in_specs=[_stream(width), _vec(LANES), _vec(LANES), tab, tab],
        out_specs=(_stream(nq), kv_spec, kv_spec),
        compiler_params=_params(("parallel", "parallel")),
        name="at_prep",
    )(qkv, tile2(q_g), tile2(k_g), cos, sin)


def _flash_body(q_ref, k_ref, v_ref, o_ref, sa_ref, sb_ref, *, tk):
    tq = q_ref.shape[0]
    n = k_ref.shape[0] // tk
    lane = lax.broadcasted_iota(jnp.int32, (tq, LANES), 1)
    lo = lane < HEAD
    q = q_ref[...]
    zero = jnp.zeros_like(q)
    qs = jnp.concatenate([jnp.where(lo, q, zero), jnp.where(lo, zero, q)], axis=0)

    def scores(j, dst_ref):
        start = pl.multiple_of(j * tk, tk)
        dst_ref[...] = _bdot_nt(qs, k_ref[pl.ds(start, tk), :])

    def absorb(j, src_ref, carry):
        m, acc = carry
        start = pl.multiple_of(j * tk, tk)
        s = src_ref[...]
        m_new = jnp.maximum(m, jnp.max(s, axis=-1, keepdims=True))
        p = jnp.exp2(s - m_new)
        acc = jnp.exp2(m - m_new) * acc + _bdot(p, v_ref[pl.ds(start, tk), :])
        return m_new, acc

    def two_chunks(i, carry):
        j = 2 * i
        scores(j + 1, sb_ref)
        carry = absorb(j, sa_ref, carry)
        scores(j + 2, sa_ref)
        return absorb(j + 1, sb_ref, carry)

    carry = (jnp.full((2 * tq, 1), -jnp.inf, F32), jnp.zeros((2 * tq, LANES), F32))
    scores(0, sa_ref)
    carry = lax.fori_loop(0, (n - 1) // 2, two_chunks, carry)
    if n % 2 == 0:
        scores(n - 1, sb_ref)
        carry = absorb(n - 2, sa_ref, carry)
        carry = absorb(n - 1, sb_ref, carry)
    else:
        carry = absorb(n - 1, sa_ref, carry)
    acc = carry[1]
    o2 = acc / pltpu.roll(acc, HEAD, axis=1)
    o_ref[...] = jnp.where(lo, o2[0:tq], pltpu.roll(o2[tq:2 * tq], HEAD, axis=1)).astype(o_ref.dtype)


def _flash(q, k2, v2, *, n_ctx_rows):
    b, t, nq = q.shape
    tq = TM
    tl = t - n_ctx_rows
    off = n_ctx_rows // tq
    tk = _pick(t, (2816, 1408, 768, 512, 256, 128))
    npair = nq // LANES
    hp = AT_HEADS // AT_KV_HEADS // 2
    kv = pl.BlockSpec((None, None, t, LANES), lambda b_, p, i: (b_, p // hp, 0, 0))
    return pl.pallas_call(
        functools.partial(_flash_body, tk=tk),
        out_shape=jax.ShapeDtypeStruct((b, tl, nq), BF16),
        grid=(b, npair, tl // tq),
        in_specs=[pl.BlockSpec((None, tq, LANES), lambda b_, p, i: (b_, i + off, p)), kv, kv],
        out_specs=pl.BlockSpec((None, tq, LANES), lambda b_, p, i: (b_, i, p)),
        scratch_shapes=[pltpu.VMEM((2 * tq, tk), F32), pltpu.VMEM((2 * tq, tk), F32)],
        compiler_params=_params(("parallel", "parallel", "parallel")),
        name="flash_gqa",
    )(q, k2, v2)


def _rope_tables(n_ctx_rows, seq):
    quarter = HEAD // 4
    inv = ROPE_THETA ** (-jnp.arange(quarter, dtype=F32) / quarter)
    rows = jnp.repeat(jnp.arange(seq // GRID_W, dtype=jnp.int32), GRID_W).astype(F32)
    cols = (jnp.arange(seq, dtype=jnp.int32) % GRID_W).astype(F32)
    ar = rows[:, None] * inv
    ac = cols[:, None] * inv
    cos = jnp.concatenate([jnp.cos(ar), jnp.cos(ar), jnp.cos(ac), jnp.cos(ac)], axis=1)
    sin = jnp.concatenate([-jnp.sin(ar), jnp.sin(ar), -jnp.sin(ac), jnp.sin(ac)], axis=1)
    cos = jnp.concatenate([jnp.ones((n_ctx_rows, HEAD), F32), cos], axis=0)
    sin = jnp.concatenate([jnp.zeros((n_ctx_rows, HEAD), F32), sin], axis=0)
    return jnp.tile(cos, (1, LANES // HEAD)), jnp.tile(sin, (1, LANES // HEAD))


def _attn_layer(h, w, *, n_ctx_rows):
    b, t, d = h.shape
    qkv = _matmul(h.reshape(b * t, d), w["qkv_w"]).reshape(b, t, -1)
    cos, sin = _rope_tables(n_ctx_rows, t - n_ctx_rows)
    q, k2, v2 = _at_prep(qkv, w["q_g"], w["k_g"], cos, sin)
    o = _flash(q, k2, v2, n_ctx_rows=n_ctx_rows)
    tl = t - n_ctx_rows
    return _matmul(o.reshape(b * tl, -1), w["out_w"]).reshape(b, tl, d)


def _ffn(h2, w_in, w_out):
    return _matmul(_swiglu_in(h2, w_in), w_out)


MOE_ROWS = 512
MOE_TOK = 512


def _moe_rank_body(sel_ref, rank_ref, rankt_ref, selt_ref, tot_ref, carry_ref, carryt_ref):
    @pl.when(pl.program_id(0) == 0)
    def _():
        carry_ref[...] = jnp.zeros_like(carry_ref)
        carryt_ref[...] = jnp.zeros_like(carryt_ref)

    sel = sel_ref[...]
    tb = sel.shape[0]
    ri = lax.broadcasted_iota(jnp.int32, (tb, tb), 0)
    ci = lax.broadcasted_iota(jnp.int32, (tb, tb), 1)
    rank_ref[...] = _bdot(_as01(ri > ci), sel) + carry_ref[0:1, :]
    pick = lax.broadcasted_iota(jnp.int32, (8, LANES), 0) == lax.broadcasted_iota(jnp.int32, (8, LANES), 1)
    selt = _bdot_nt(_as01(pick), sel)
    selt_ref[...] = selt
    rankt_ref[...] = _bdot(selt, _as01(ri < ci)) + carryt_ref[:, 0:1]
    carry_ref[...] = carry_ref[...] + jnp.sum(sel, axis=0, keepdims=True)
    carryt_ref[...] = carryt_ref[...] + jnp.sum(selt, axis=1, keepdims=True)
    tot_ref[...] = carry_ref[...]


def _moe_rank(sel):
    m = sel.shape[0]
    tb = MOE_TOK
    return pl.pallas_call(
        _moe_rank_body,
        out_shape=(jax.ShapeDtypeStruct((m, LANES), F32), jax.ShapeDtypeStruct((8, m), F32),
                   jax.ShapeDtypeStruct((8, m), F32), jax.ShapeDtypeStruct((8, LANES), F32)),
        grid=(m // tb,),
        in_specs=[pl.BlockSpec((tb, LANES), lambda i: (i, 0))],
        out_specs=(pl.BlockSpec((tb, LANES), lambda i: (i, 0)), pl.BlockSpec((8, tb), lambda i: (0, i)),
                   pl.BlockSpec((8, tb), lambda i: (0, i)), pl.BlockSpec((8, LANES), lambda i: (0, 0))),
        scratch_shapes=[pltpu.VMEM((8, LANES), F32), pltpu.VMEM((8, LANES), F32)],
        compiler_params=_params(("arbitrary",)),
        name="moe_rank",
    )(sel)


def _moe_plan(sel, rank, rankt, selt, tot, m):
    tr, tb, ne = MOE_ROWS, MOE_TOK, N_EXPERTS
    nb = m // tb
    nt = 2 * m // tr + ne
    i32 = jnp.int32
    cnt = tot[0, :ne].astype(i32)
    tile_start = jnp.concatenate([jnp.zeros((1,), i32), jnp.cumsum((cnt + tr - 1) // tr)])
    off = tile_start[:ne] * tr
    n_tiles = tile_start[ne]
    count_le = lambda ends, x: jnp.sum(ends[None, :] <= x[:, None], axis=1).astype(i32)
    tiles = jnp.arange(nt, dtype=i32)
    tile_valid = tiles < n_tiles
    tile_exp = jnp.minimum(count_le(tile_start[1:], tiles), ne - 1)
    last_exp = tile_exp[jnp.maximum(n_tiles - 1, 0)]
    tile_exp = jnp.where(tile_valid, tile_exp, last_exp)
    offp = jnp.pad(off.astype(F32), (0, LANES - ne))
    posmat = jnp.where(sel > 0, rank + offp[None, :], -1.0)
    post = jnp.where(selt > 0, rankt + jnp.pad(off.astype(F32), (0, 8 - ne))[:, None], -1.0)
    blkcum = jnp.concatenate([rank[::tb, :ne].astype(i32).T, cnt[:, None]], axis=1)

    k_lo = tiles * tr - off[tile_exp]
    k_hi = jnp.minimum(cnt[tile_exp], k_lo + tr) - 1
    ends = blkcum[tile_exp][:, 1:]
    lo_blk = jnp.minimum(jnp.sum(ends <= k_lo[:, None], axis=1), nb - 1).astype(i32)
    hi_blk = jnp.minimum(jnp.sum(ends <= k_hi[:, None], axis=1), nb - 1).astype(i32)
    span = jnp.where(tile_valid, hi_blk - lo_blk + 1, 1)
    lo_blk = jnp.where(tile_valid, lo_blk, 0)
    g_end = jnp.cumsum(span)
    g_start = g_end - span
    ns = nt + ne * nb
    steps = jnp.arange(ns, dtype=i32)
    g_tile = jnp.minimum(count_le(g_end, steps), nt - 1)
    g_valid = (steps < g_end[-1]).astype(i32)
    g_blk = jnp.clip(lo_blk[g_tile] + steps - g_start[g_tile], 0, nb - 1).astype(i32)
    g_first = (steps == g_start[g_tile]).astype(i32)
    gather = (g_tile, g_blk, tile_exp[g_tile], g_first, g_valid)

    r_lo = off[:, None] + blkcum[:, :-1]
    r_hi = off[:, None] + blkcum[:, 1:] - 1
    t_lo = (r_lo // tr).T.reshape(-1)
    n_t = jnp.where(r_hi >= r_lo, r_hi // tr - r_lo // tr + 1, 0).T.reshape(-1)
    c_end = jnp.cumsum(n_t)
    c_start = c_end - n_t
    idx = jnp.minimum(count_le(c_end, steps), nb * ne - 1)
    c_valid = (steps < c_end[-1]).astype(i32)
    c_blk = idx // ne
    c_tile = jnp.clip(t_lo[idx] + steps - c_start[idx], 0, nt - 1).astype(i32)
    c_first = (steps == c_start[c_blk * ne]).astype(i32)
    combine = (c_blk, c_tile, idx % ne, c_first, c_valid)
    return dict(nt=nt, ns=ns, tile_exp=tile_exp, n_tiles=n_tiles.reshape(1), posmat=posmat, post=post,
                gather=gather, combine=combine)


def _moe_gather_body(g_tile, g_blk, g_exp, g_first, g_valid, h_ref, post_ref, xs_ref):
    s = pl.program_id(0)

    @pl.when(g_valid[s] == 1)
    def _():
        tr, tb = xs_ref.shape[0], h_ref.shape[0]
        rows = (lax.broadcasted_iota(jnp.int32, (tr, tb), 0) + g_tile[s] * tr).astype(F32)
        onehot = post_ref[pl.ds(g_exp[s], 1), :] == rows
        x = _bdot(_as01(onehot), h_ref[...]).astype(xs_ref.dtype)

        @pl.when(g_first[s] == 1)
        def _():
            xs_ref[...] = x

        @pl.when(g_first[s] == 0)
        def _():
            xs_ref[...] = xs_ref[...] + x


def _moe_gather(h2, post, plan):
    m, d = h2.shape
    tr, tb = MOE_ROWS, MOE_TOK
    grid_spec = pltpu.PrefetchScalarGridSpec(
        num_scalar_prefetch=5, grid=(plan["ns"],),
        in_specs=[pl.BlockSpec((tb, d), lambda s, gt, gb, ge, gf, gv: (gb[s], 0)),
                  pl.BlockSpec((8, tb), lambda s, gt, gb, ge, gf, gv: (0, gb[s]))],
        out_specs=pl.BlockSpec((tr, d), lambda s, gt, gb, ge, gf, gv: (gt[s], 0)))
    return pl.pallas_call(
        _moe_gather_body,
        out_shape=jax.ShapeDtypeStruct((plan["nt"] * tr, d), BF16),
        grid_spec=grid_spec,
        compiler_params=_params(("arbitrary",)),
        name="moe_gather",
    )(*plan["gather"], h2, post)


def _moe_expert_body(t_exp, n_tiles, xs_ref, wg_ref, wu_ref, wo_ref, y_ref, acc_ref):
    j = pl.program_id(0)
    f = pl.program_id(1)

    @pl.when(f == 0)
    def _():
        acc_ref[...] = jnp.zeros_like(acc_ref)

    @pl.when(j < n_tiles[0])
    def _():
        x = xs_ref[...]
        act = _silu(_bdot(x, wg_ref[...])) * _bdot(x, wu_ref[...])
        acc_ref[...] += _bdot(act, wo_ref[...])

    @pl.when(f == pl.num_programs(1) - 1)
    def _():
        y_ref[...] = acc_ref[...].astype(y_ref.dtype)


def _moe_experts(xs, w_in, w_out, layer, plan):
    rows, d = xs.shape
    tr = MOE_ROWS
    f = w_in.shape[3] // 2
    fk = _pick(f, (896, 512, 256, 128))
    nf = f // fk
    hold = lambda j, fi, nt: jnp.where(j < nt[0], fi, nf - 1)
    grid_spec = pltpu.PrefetchScalarGridSpec(
        num_scalar_prefetch=2, grid=(rows // tr, nf),
        in_specs=[pl.BlockSpec((tr, d), lambda j, fi, te, nt: (j, 0)),
                  pl.BlockSpec((None, None, d, fk), lambda j, fi, te, nt: (layer, te[j], 0, hold(j, fi, nt))),
                  pl.BlockSpec((None, None, d, fk), lambda j, fi, te, nt: (layer, te[j], 0, hold(j, fi, nt) + nf)),
                  pl.BlockSpec((None, None, fk, d), lambda j, fi, te, nt: (layer, te[j], hold(j, fi, nt), 0))],
        out_specs=pl.BlockSpec((tr, d), lambda j, fi, te, nt: (j, 0)),
        scratch_shapes=[pltpu.VMEM((tr, d), F32)])
    return pl.pallas_call(
        _moe_expert_body,
        out_shape=jax.ShapeDtypeStruct((rows, d), BF16),
        grid_spec=grid_spec,
        compiler_params=_params(("arbitrary", "arbitrary")),
        name="moe_experts",
    )(plan["tile_exp"], plan["n_tiles"], xs, w_in, w_in, w_out)


def _moe_combine_body(c_blk, c_tile, c_exp, c_first, c_valid, pos_ref, gates_ref, y_ref, o_ref):
    s = pl.program_id(0)

    @pl.when(c_valid[s] == 1)
    def _():
        tb, tr = o_ref.shape[0], y_ref.shape[0]
        lane = lax.broadcasted_iota(jnp.int32, pos_ref.shape, 1)
        mine = lane == c_exp[s]
        pos_e = jnp.sum(jnp.where(mine, pos_ref[...], 0.0), axis=-1, keepdims=True)
        gate = jnp.sum(jnp.where(mine, gates_ref[...], 0.0), axis=-1, keepdims=True)
        rows = (lax.broadcasted_iota(jnp.int32, (tb, tr), 1) + c_tile[s] * tr).astype(F32)
        y = gate * jnp.dot(_as01(pos_e == rows), y_ref[...], preferred_element_type=F32)

        @pl.when(c_first[s] == 1)
        def _():
            o_ref[...] = y

        @pl.when(c_first[s] == 0)
        def _():
            o_ref[...] = o_ref[...] + y


def _moe_combine(posmat, gates, y, plan):
    m = posmat.shape[0]
    d = y.shape[1]
    tr, tb = MOE_ROWS, MOE_TOK
    grid_spec = pltpu.PrefetchScalarGridSpec(
        num_scalar_prefetch=5, grid=(plan["ns"],),
        in_specs=[pl.BlockSpec((tb, LANES), lambda s, cb, ct, ce, cf, cv: (cb[s], 0)),
                  pl.BlockSpec((tb, LANES), lambda s, cb, ct, ce, cf, cv: (cb[s], 0)),
                  pl.BlockSpec((tr, d), lambda s, cb, ct, ce, cf, cv: (ct[s], 0))],
        out_specs=pl.BlockSpec((tb, d), lambda s, cb, ct, ce, cf, cv: (cb[s], 0)))
    return pl.pallas_call(
        _moe_combine_body,
        out_shape=jax.ShapeDtypeStruct((m, d), F32),
        grid_spec=grid_spec,
        compiler_params=_params(("arbitrary",)),
        name="moe_combine",
    )(*plan["combine"], posmat, gates, y)


def _moe(h2, gates, sel, w_in, w_out, layer):
    m = h2.shape[0]
    rank, rankt, selt, tot = _moe_rank(sel)
    plan = _moe_plan(sel, rank, rankt, selt, tot, m)
    xs = _moe_gather(h2, plan["post"], plan)
    y = _moe_experts(xs, w_in, w_out, layer, plan)
    return _moe_combine(plan["posmat"], gates, y, plan)


def kernel(x, c, ctx, c_ctx, mod_w, mod_b, norm1_g, norm2_g, final_g, mb_in_w, mb_conv_w, mb_conv_b, mb_dt_bias, mb_a_log, mb_d, mb_norm_g, mb_out_w, rw_mix, rw_rkv_w, rw_w0, rw_w1, rw_w2, rw_a0, rw_a1, rw_a2, rw_g1, rw_g2, rw_k_k, rw_k_a, rw_r_k, rw_ln_g, rw_ln_b, rw_out_w, pl_w, pl_scale, at_qkv_w, at_q_g, at_k_g, at_out_w, ff_in_w, ff_out_w, moe_router_w, moe_in_w, moe_out_w):
    b, seq, d = x.shape
    n_ctx = ctx.shape[1]
    depth = mod_w.shape[0]
    t = n_ctx + seq
    nct = n_ctx // TM
    assert depth == 4 and n_ctx % TM == 0 and seq % TM == 0 and b + 1 <= 8
    bf = lambda a: a.astype(BF16)

    svec = jnp.concatenate([c, c_ctx[None, :], jnp.zeros((8 - b - 1, d), F32)], axis=0)
    mods = _mods(svec, bf(mod_w), mod_b)
    mv = jnp.pad(mods[:, :b + 1].reshape(depth, b + 1, 6, d), ((0, 0), (0, 0), (0, 2), (0, 0)))

    xs = jnp.concatenate([ctx, x], axis=1)
    m = b * t
    resid = functools.partial(_resid_norm, nct=nct, x_off=0)

    di = MB_HEADS * HEAD
    xbc = di + 2 * MB_GROUPS * MB_STATE
    mamba_w = dict(in_w=bf(mb_in_w[0]),
                   in_w_dt=bf(jnp.pad(mb_in_w[0][:, di + xbc:], ((0, 0), (0, LANES - 2 * MB_HEADS)))),
                   conv_w=mb_conv_w[0], conv_b=mb_conv_b[0], dt_bias=mb_dt_bias[0], a_log=mb_a_log[0],
                   dvec=jnp.repeat(mb_d[0], HEAD).reshape(1, di), norm_g=mb_norm_g[0], out_w=bf(mb_out_w[0]))
    h = _norm_mod(xs, norm1_g[0], mv, 0, nct, BF16)
    y = _mamba_layer(h, mamba_w, n_ctx_rows=n_ctx)
    xs, h2 = resid(xs, y, mv, norm2_g[0], gate_layer=0, jg=2, mod_layer=0, js=3, jc=4, out_dtype=BF16)
    f = _ffn(h2.reshape(m, d), bf(ff_in_w[0]), bf(ff_out_w[0])).reshape(b, t, d)
    xs, h = resid(xs, f, mv, norm1_g[1], gate_layer=0, jg=5, mod_layer=1, js=0, jc=1, out_dtype=F32)

    blockdiag = lambda u: jnp.concatenate(
        [jnp.concatenate([u[0], jnp.zeros_like(u[0])], axis=1),
         jnp.concatenate([jnp.zeros_like(u[1]), u[1]], axis=1)], axis=0)
    rwkv_w = dict(mix=rw_mix[0], r_w=bf(rw_rkv_w[0, 0]), k_w=bf(rw_rkv_w[0, 1]), v_w=bf(rw_rkv_w[0, 2]),
                  g1=bf(rw_g1[0]), g2=bf(rw_g2[0]),
                  w1=bf(jnp.concatenate([rw_w1[0, 0], rw_w1[0, 1]], axis=1)), w2=bf(blockdiag(rw_w2[0])),
                  w0=rw_w0[0].reshape(1, 2 * d),
                  a1=bf(jnp.concatenate([rw_a1[0, 0], rw_a1[0, 1]], axis=1)), a2=bf(blockdiag(rw_a2[0])),
                  a0=rw_a0[0].reshape(1, 2 * d),
                  k_k=rw_k_k[0], k_a=rw_k_a[0], r_k=rw_r_k[0].reshape(d), ln_g=rw_ln_g[0], ln_b=rw_ln_b[0],
                  out_w=bf(rw_out_w[0]))
    y = _rwkv_layer(h, rwkv_w, n_ctx_rows=n_ctx)
    moe_in_b, moe_out_b = bf(moe_in_w), bf(moe_out_w)
    xs, h2, gates, sel = _resid_norm_router(xs, y, mv, norm2_g[1], moe_router_w[0], gate_layer=1, jg=2,
                                            mod_layer=1, js=3, jc=4, nct=nct, x_off=0)
    f = _moe(h2.reshape(m, d), gates.reshape(m, LANES), sel.reshape(m, LANES), moe_in_b, moe_out_b, 0)
    f = f.reshape(b, t, d)
    xs, h = resid(xs, f, mv, norm1_g[2], gate_layer=1, jg=5, mod_layer=2, js=0, jc=1, out_dtype=F32)

    y = _pool_layer(h, bf(pl_w[0]), pl_scale[0], nct=nct)
    xs, h2 = resid(xs, y, mv, norm2_g[2], gate_layer=2, jg=2, mod_layer=2, js=3, jc=4, out_dtype=BF16)
    f = _ffn(h2.reshape(m, d), bf(ff_in_w[1]), bf(ff_out_w[1])).reshape(b, t, d)
    xs, h = resid(xs, f, mv, norm1_g[3], gate_layer=2, jg=5, mod_layer=3, js=0, jc=1, out_dtype=BF16)

    attn_w = dict(qkv_w=bf(at_qkv_w[0]), q_g=at_q_g[0], k_g=at_k_g[0], out_w=bf(at_out_w[0]))
    y = _attn_layer(h, attn_w, n_ctx_rows=n_ctx)
    xl, h2, gates, sel = _resid_norm_router(xs, y, mv, norm2_g[3], moe_router_w[1], gate_layer=3, jg=2,
                                            mod_layer=3, js=3, jc=4, nct=0, x_off=nct)
    ml = b * seq
    f = _moe(h2.reshape(ml, d), gates.reshape(ml, LANES), sel.reshape(ml, LANES), moe_in_b, moe_out_b, 1)
    return _resid_final(xl, f.reshape(b, seq, d), mv, final_g, gate_layer=3, jg=5)
```

```python
import functools
import math

import jax
import jax.numpy as jnp
from jax import lax
from jax.experimental import pallas as pl
from jax.experimental.pallas import tpu as pltpu

F32 = jnp.float32
BF16 = jnp.bfloat16

NORM_EPS = 1e-6
TM = 256
LANES = 128
HEAD = 64
VMEM_LIMIT = 48 * 1024 * 1024

MB_HEADS = 32
MB_GROUPS = 4
MB_STATE = 128
MB_CHUNK = 128
MB_CONV = 5
RW_CHUNK = 64
RW_DECAY_SCALE = 0.606531
RW_LN_EPS = 64e-5
POOL_WINDOWS = (2, 4, 8, 16)
POOL_HALO = 16
AT_HEADS = 16
AT_KV_HEADS = 4
ROPE_THETA = 10000.0
GRID_W = 64
N_EXPERTS = 8


def _params(sem, vmem=VMEM_LIMIT):
    return pltpu.CompilerParams(dimension_semantics=sem, vmem_limit_bytes=vmem)


def _bdot(a, b):
    return jnp.dot(a.astype(BF16), b.astype(BF16), preferred_element_type=F32)


def _bdot_nt(a, b):
    return lax.dot_general(a.astype(BF16), b.astype(BF16), (((1,), (1,)), ((), ())),
                           preferred_element_type=F32)


def _bdot_tn(a, b):
    return lax.dot_general(a.astype(BF16), b.astype(BF16), (((0,), (0,)), ((), ())),
                           preferred_element_type=F32)


def _split3(x):
    p0 = x.astype(BF16)
    r1 = x - p0.astype(F32)
    p1 = r1.astype(BF16)
    p2 = (r1 - p1.astype(F32)).astype(BF16)
    return p0, p1, p2


def _as01(mask):
    return jnp.where(mask, 1.0, 0.0).astype(BF16)


def _dot01_left(sel, x):
    sel = _as01(sel)
    p0, p1, p2 = _split3(x)
    return (jnp.dot(sel, p0, preferred_element_type=F32)
            + jnp.dot(sel, p1, preferred_element_type=F32)
            + jnp.dot(sel, p2, preferred_element_type=F32))


def _dot01_right(x, sel):
    sel = _as01(sel)
    p0, p1, p2 = _split3(x)
    return (jnp.dot(p0, sel, preferred_element_type=F32)
            + jnp.dot(p1, sel, preferred_element_type=F32)
            + jnp.dot(p2, sel, preferred_element_type=F32))


def _head_sum(x):
    r = lax.broadcasted_iota(jnp.int32, (LANES, LANES), 0) // HEAD
    c = lax.broadcasted_iota(jnp.int32, (LANES, LANES), 1) // HEAD
    ones_bd = r == c
    slabs = [_dot01_right(x[:, s:s + LANES], ones_bd) for s in range(0, x.shape[1], LANES)]
    return slabs[0] if len(slabs) == 1 else jnp.concatenate(slabs, axis=1)


def _sigmoid(x):
    return 1.0 / (1.0 + jnp.exp(-x))


def _silu(x):
    return x * _sigmoid(x)


def _rms(x, g):
    ms = jnp.mean(x * x, axis=-1, keepdims=True)
    return x * lax.rsqrt(ms + NORM_EPS) * g


def _mods_body(s_ref, w_ref, b_ref, o_ref):
    o_ref[...] = _bdot(_silu(s_ref[...]), w_ref[...]) + b_ref[...]


def _mods(svec, mod_w, mod_b):
    depth, d, n6 = mod_w.shape
    tn = 1536
    return pl.pallas_call(
        _mods_body,
        out_shape=jax.ShapeDtypeStruct((depth, 8, n6), F32),
        grid=(depth, n6 // tn),
        in_specs=[pl.BlockSpec((8, d), lambda l, j: (0, 0)),
                  pl.BlockSpec((None, d, tn), lambda l, j: (l, 0, j)),
                  pl.BlockSpec((None, 1, tn), lambda l, j: (l, 0, j))],
        out_specs=pl.BlockSpec((None, 8, tn), lambda l, j: (l, 0, j)),
        compiler_params=_params(("parallel", "parallel")),
        name="mods",
    )(svec, mod_w, mod_b.reshape(depth, 1, n6))


def _stream(width, off=0, rows=TM):
    return pl.BlockSpec((None, rows, width), lambda b, i: (b, i + off, 0))


def _stream2(width, d, off=0):
    return pl.BlockSpec((None, None, TM, width), lambda b, i: (d, b, i + off, 0))


def _vec(width, rows=1):
    return pl.BlockSpec((rows, width), lambda b, i: (0, 0))


def _mv(layer, nct, nb, d):
    return pl.BlockSpec((None, None, 8, d), lambda b, i: (layer, jnp.where(i < nct, nb, b), 0, 0))


def _halo(width, nrows, t_total, side):
    per = TM // nrows
    last = t_total // nrows - 1
    if side < 0:
        return pl.BlockSpec((None, nrows, width), lambda b, i: (b, jnp.maximum(i * per - 1, 0), 0))
    return pl.BlockSpec((None, nrows, width), lambda b, i: (b, jnp.minimum((i + 1) * per, last), 0))


def _seq_edges(i, nt, nct):
    first = jnp.logical_or(i == 0, i == nct)
    last = jnp.logical_or(i == nct - 1, i == nt - 1)
    return first, last


def _shift_rows(x, prev8, next8, o):
    rows = x.shape[0]
    r8 = lax.broadcasted_iota(jnp.int32, (8, x.shape[1]), 0)
    if o < 0:
        k = -o
        s = pltpu.roll(x, k, axis=0)
        top = jnp.where(r8 < k, pltpu.roll(prev8, k, axis=0), s[0:8])
        return jnp.concatenate([top, s[8:]], axis=0)
    s = pltpu.roll(x, rows - o, axis=0)
    bot = jnp.where(r8 >= 8 - o, pltpu.roll(next8, 8 - o, axis=0), s[rows - 8:])
    return jnp.concatenate([s[:rows - 8], bot], axis=0)


def _norm_mod_body(x_ref, g_ref, mv_ref, h_ref, *, js, jc):
    h = _rms(x_ref[...], g_ref[...]) * (1.0 + mv_ref[jc:jc + 1, :]) + mv_ref[js:js + 1, :]
    h_ref[...] = h.astype(h_ref.dtype)


def _norm_mod(x, g, mv, layer, nct, out_dtype):
    b, t, d = x.shape
    nb = b
    return pl.pallas_call(
        functools.partial(_norm_mod_body, js=0, jc=1),
        out_shape=jax.ShapeDtypeStruct((b, t, d), out_dtype),
        grid=(b, t // TM),
        in_specs=[_stream(d), _vec(d), _mv(layer, nct, nb, d)],
        out_specs=_stream(d),
        compiler_params=_params(("parallel", "parallel")),
        name="norm_mod",
    )(x, g.reshape(1, d), mv)


def _resid_norm_body(x_ref, y_ref, mvg_ref, g_ref, mvm_ref, xo_ref, h_ref, *, jg, js, jc):
    xn = x_ref[...] + mvg_ref[jg:jg + 1, :] * y_ref[...].astype(F32)
    xo_ref[...] = xn
    h = _rms(xn, g_ref[...]) * (1.0 + mvm_ref[jc:jc + 1, :]) + mvm_ref[js:js + 1, :]
    h_ref[...] = h.astype(h_ref.dtype)


def _resid_norm(x, y, mv, g, *, gate_layer, jg, mod_layer, js, jc, nct, x_off, out_dtype):
    b, t, d = y.shape
    return pl.pallas_call(
        functools.partial(_resid_norm_body, jg=jg, js=js, jc=jc),
        out_shape=(jax.ShapeDtypeStruct((b, t, d), F32), jax.ShapeDtypeStruct((b, t, d), out_dtype)),
        grid=(b, t // TM),
        in_specs=[_stream(d, x_off), _stream(d), _mv(gate_layer, nct, b, d), _vec(d),
                  _mv(mod_layer, nct, b, d)],
        out_specs=(_stream(d), _stream(d)),
        compiler_params=_params(("parallel", "parallel")),
        name="resid_norm",
    )(x, y, mv, g.reshape(1, d), mv)


def _top2_gates(logits):
    lane = lax.broadcasted_iota(jnp.int32, logits.shape, 1)
    neg = jnp.float32(-jnp.inf)
    lg = jnp.where(lane < N_EXPERTS, logits, neg)
    v1 = jnp.max(lg, axis=-1, keepdims=True)
    i1 = jnp.min(jnp.where(lg == v1, lane, LANES), axis=-1, keepdims=True)
    lg2 = jnp.where(lane == i1, neg, lg)
    v2 = jnp.max(lg2, axis=-1, keepdims=True)
    i2 = jnp.min(jnp.where(lg2 == v2, lane, LANES), axis=-1, keepdims=True)
    e = jnp.exp(v2 - v1)
    w1 = 1.0 / (1.0 + e)
    w2 = e / (1.0 + e)
    gates = jnp.where(lane == i1, w1, 0.0) + jnp.where(lane == i2, w2, 0.0)
    sel = jnp.where(jnp.logical_or(lane == i1, lane == i2), 1.0, 0.0)
    return gates, sel


def _resid_norm_router_body(x_ref, y_ref, mvg_ref, g_ref, mvm_ref, rw_ref, xo_ref, h_ref, gates_ref, sel_ref,
                            *, jg, js, jc):
    xn = x_ref[...] + mvg_ref[jg:jg + 1, :] * y_ref[...].astype(F32)
    xo_ref[...] = xn
    h = _rms(xn, g_ref[...]) * (1.0 + mvm_ref[jc:jc + 1, :]) + mvm_ref[js:js + 1, :]
    h_ref[...] = h.astype(h_ref.dtype)
    h0, h1, h2 = _split3(h)
    w0, w1, w2 = _split3(rw_ref[...])
    dot = lambda a, c: jnp.dot(a, c, preferred_element_type=F32)
    logits = (dot(h0, w0) + (dot(h0, w1) + dot(h1, w0))
              + (dot(h0, w2) + dot(h1, w1) + dot(h2, w0)))
    gates_ref[...], sel_ref[...] = _top2_gates(logits)


def _resid_norm_router(x, y, mv, g, router_w, *, gate_layer, jg, mod_layer, js, jc, nct, x_off):
    b, t, d = y.shape
    rw = jnp.pad(router_w, ((0, 0), (0, LANES - router_w.shape[1])))
    lanes = jax.ShapeDtypeStruct((b, t, LANES), F32)
    return pl.pallas_call(
        functools.partial(_resid_norm_router_body, jg=jg, js=js, jc=jc),
        out_shape=(jax.ShapeDtypeStruct((b, t, d), F32), jax.ShapeDtypeStruct((b, t, d), BF16), lanes, lanes),
        grid=(b, t // TM),
        in_specs=[_stream(d, x_off), _stream(d), _mv(gate_layer, nct, b, d), _vec(d),
                  _mv(mod_layer, nct, b, d), _vec(LANES, d)],
        out_specs=(_stream(d), _stream(d), _stream(LANES), _stream(LANES)),
        compiler_params=_params(("parallel", "parallel")),
        name="resid_norm_router",
    )(x, y, mv, g.reshape(1, d), mv, rw)


def _resid_final_body(x_ref, y_ref, mvg_ref, g_ref, o_ref, *, jg):
    xn = x_ref[...] + mvg_ref[jg:jg + 1, :] * y_ref[...].astype(F32)
    o_ref[...] = _rms(xn, g_ref[...])


def _resid_final(x, y, mv, g, *, gate_layer, jg):
    b, t, d = y.shape
    return pl.pallas_call(
        functools.partial(_resid_final_body, jg=jg),
        out_shape=jax.ShapeDtypeStruct((b, t, d), F32),
        grid=(b, t // TM),
        in_specs=[_stream(d), _stream(d), _mv(gate_layer, 0, b, d), _vec(d)],
        out_specs=_stream(d),
        compiler_params=_params(("parallel", "parallel")),
        name="resid_final",
    )(x, y, mv, g.reshape(1, d))


def _mm_body(x_ref, w_ref, o_ref):
    o_ref[...] = _bdot(x_ref[...], w_ref[...]).astype(o_ref.dtype)


def _pick(n, prefs):
    for p in prefs:
        if n % p == 0:
            return p
    raise ValueError(f"no tile for {n}")


def _matmul(x, w, *, n_cols=None, out_dtype=F32, tm=None, tn=None):
    m, k = x.shape
    n = n_cols or w.shape[1]
    tm = tm or _pick(m, (1024, 768, 512, 256))
    tn = tn or _pick(n, (1024, 768, 512, 256, 128))
    return pl.pallas_call(
        _mm_body,
        out_shape=jax.ShapeDtypeStruct((m, n), out_dtype),
        grid=(n // tn, m // tm),
        in_specs=[pl.BlockSpec((tm, k), lambda j, i: (i, 0)),
                  pl.BlockSpec((k, tn), lambda j, i: (0, j))],
        out_specs=pl.BlockSpec((tm, tn), lambda j, i: (i, j)),
        compiler_params=_params(("parallel", "parallel")),
        name="matmul",
    )(x, w)


def _swiglu_body(x_ref, wg_ref, wu_ref, o_ref):
    x = x_ref[...]
    o_ref[...] = (_silu(_bdot(x, wg_ref[...])) * _bdot(x, wu_ref[...])).astype(o_ref.dtype)


def _swiglu_in(x, w_in, *, tm=None, tn=None):
    m, k = x.shape
    f = w_in.shape[1] // 2
    tm = tm or _pick(m, (768, 512, 256))
    tn = tn or _pick(f, (1408, 896, 512, 256, 128))
    nf = f // tn
    return pl.pallas_call(
        _swiglu_body,
        out_shape=jax.ShapeDtypeStruct((m, f), BF16),
        grid=(nf, m // tm),
        in_specs=[pl.BlockSpec((tm, k), lambda j, i: (i, 0)),
                  pl.BlockSpec((k, tn), lambda j, i: (0, j)),
                  pl.BlockSpec((k, tn), lambda j, i: (0, j + nf))],
        out_specs=pl.BlockSpec((tm, tn), lambda j, i: (i, j)),
        compiler_params=_params(("parallel", "parallel")),
        name="swiglu_in",
    )(x, w_in, w_in)


def _lora_body(x_ref, a_ref, b_ref, bias_ref, o_ref, *, act):
    t = _bdot(x_ref[...], a_ref[...])
    if act == "tanh":
        t = jnp.tanh(t)
    elif act == "sigmoid":
        t = _sigmoid(t)
    o_ref[...] = _bdot(t, b_ref[...]) + bias_ref[...]


def _lora(x, a, bm, bias, act):
    m, k = x.shape
    r = a.shape[1]
    n = bm.shape[1]
    tm = _pick(m, (512, 256))
    return pl.pallas_call(
        functools.partial(_lora_body, act=act),
        out_shape=jax.ShapeDtypeStruct((m, n), F32),
        grid=(m // tm,),
        in_specs=[pl.BlockSpec((tm, k), lambda i: (i, 0)),
                  pl.BlockSpec((k, r), lambda i: (0, 0)),
                  pl.BlockSpec((r, n), lambda i: (0, 0)),
                  pl.BlockSpec((1, n), lambda i: (0, 0))],
        out_specs=pl.BlockSpec((tm, n), lambda i: (i, 0)),
        compiler_params=_params(("parallel",)),
        name="lora",
    )(x, a, bm, bias)


def _mb_conv_body(x_ref, xp_ref, xn_ref, w_ref, b_ref, o_ref, *, nt, nct):
    i = pl.program_id(1)
    first, last = _seq_edges(i, nt, nct)
    x = x_ref[...]
    prev8 = jnp.where(first, 0.0, xp_ref[...])
    next8 = jnp.where(last, 0.0, xn_ref[...])
    pad = (MB_CONV - 1) // 2
    acc = x * w_ref[pad:pad + 1, :] + b_ref[...]
    for o in range(-pad, pad + 1):
        if o != 0:
            acc = acc + _shift_rows(x, prev8, next8, o) * w_ref[pad + o:pad + o + 1, :]
    o_ref[...] = _silu(acc)


def _mb_conv(zx, conv_w, conv_b, *, col0, nct):
    b, t, _ = zx.shape
    c = conv_w.shape[1]
    tc = 512
    cb = col0 // tc
    nt = t // TM
    per = TM // 8
    lastb = t // 8 - 1
    w8 = jnp.pad(conv_w, ((0, 8 - conv_w.shape[0]), (0, 0)))
    return pl.pallas_call(
        functools.partial(_mb_conv_body, nt=nt, nct=nct),
        out_shape=jax.ShapeDtypeStruct((b, t, c), F32),
        grid=(b, nt, c // tc),
        in_specs=[pl.BlockSpec((None, TM, tc), lambda b_, i, j: (b_, i, cb + j)),
                  pl.BlockSpec((None, 8, tc), lambda b_, i, j: (b_, jnp.maximum(i * per - 1, 0), cb + j)),
                  pl.BlockSpec((None, 8, tc),
                               lambda b_, i, j: (b_, jnp.minimum((i + 1) * per, lastb), cb + j)),
                  pl.BlockSpec((8, tc), lambda b_, i, j: (0, j)),
                  pl.BlockSpec((1, tc), lambda b_, i, j: (0, j))],
        out_specs=pl.BlockSpec((None, TM, tc), lambda b_, i, j: (b_, i, j)),
        compiler_params=_params(("parallel", "parallel", "parallel")),
        name="mb_conv",
    )(zx, zx, zx, w8, conv_b.reshape(1, c))


def _softplus(x):
    return jnp.maximum(x, 0.0) + jnp.log(1.0 + jnp.exp(-jnp.abs(x)))


def _lane_bcast(col, width):
    return jnp.broadcast_to(col, (col.shape[0], width))


def _ssd_body(x_ref, b_ref, c_ref, dtc_ref, dtr_ref, bc_ref, br_ref, ac_ref, ar_ref, y_ref, st_ref):
    d = pl.program_id(1)
    q = MB_CHUNK
    hpg = MB_HEADS // MB_GROUPS

    @pl.when(pl.program_id(2) == 0)
    def _():
        st_ref[...] = jnp.zeros_like(st_ref)

    sgn = 1 - 2 * d
    dt_c = _softplus(dtc_ref[...] + bc_ref[...])
    dt_r = _softplus(dtr_ref[...] + br_ref[...])
    dta_c = dt_c * (-jnp.exp(ac_ref[...]))
    dta_r = dt_r * (-jnp.exp(ar_ref[...]))
    ii = lax.broadcasted_iota(jnp.int32, (q, q), 0)
    jj = lax.broadcasted_iota(jnp.int32, (q, q), 1)
    mask = (ii - jj) * sgn >= 0
    cum_c = _dot01_left(mask, dta_c)
    cum_r = _dot01_right(dta_r, (jj - ii) * sgn >= 0)
    tot_c = jnp.sum(dta_c, axis=0, keepdims=True)
    ecum_c = jnp.exp(cum_c)
    f_c = jnp.exp(tot_c - cum_c) * dt_c
    etot_c = jnp.exp(tot_c)
    lane = lax.broadcasted_iota(jnp.int32, (q, LANES), 1)
    lo = lane < HEAD
    lane1 = lax.broadcasted_iota(jnp.int32, (1, LANES), 1)

    for g in range(MB_GROUPS):
        bg = b_ref[:, g * MB_STATE:(g + 1) * MB_STATE].astype(BF16)
        cg = c_ref[:, g * MB_STATE:(g + 1) * MB_STATE].astype(BF16)
        gmat = _bdot_nt(cg, bg)
        for pq in range(hpg // 2):
            p = g * (hpg // 2) + pq
            h0 = 2 * p
            xp = x_ref[:, p * LANES:(p + 1) * LANES]
            xpb = xp.astype(BF16)
            ys = []
            for h in (h0, h0 + 1):
                seg = _lane_bcast(cum_c[:, h:h + 1], q) - cum_r[h:h + 1, :]
                wmat = gmat * (jnp.exp(jnp.where(mask, seg, -jnp.inf)) * dt_r[h:h + 1, :])
                ys.append(_bdot(wmat, xpb))
            y_intra = jnp.where(lo, ys[0], ys[1])
            pair = lambda a: jnp.where(lo, _lane_bcast(a[:, h0:h0 + 1], LANES),
                                       _lane_bcast(a[:, h0 + 1:h0 + 2], LANES))
            st = st_ref[p]
            y_ref[:, p * LANES:(p + 1) * LANES] = y_intra + _bdot(cg, st) * pair(ecum_c)
            upd = _bdot_tn(bg, xp * pair(f_c))
            et = jnp.where(lane1 < HEAD, _lane_bcast(etot_c[:, h0:h0 + 1], LANES),
                           _lane_bcast(etot_c[:, h0 + 1:h0 + 2], LANES))
            st_ref[p] = st * et + upd


def _scan_chunk(d, c, n_ctx, n_all):
    rev = jnp.where(c < n_ctx, n_ctx - 1 - c, n_all - 1 - (c - n_ctx))
    return jnp.where(d == 0, c, rev)


def _ssd(xa, dtc, dtr, dt_bias, a_log, *, n_ctx_rows):
    b, t, _ = xa.shape
    q = MB_CHUNK
    nh = MB_HEADS
    di = nh * HEAD
    gn = MB_GROUPS * MB_STATE
    nc = t // q
    ncc = n_ctx_rows // q
    tc = functools.partial(_scan_chunk, n_ctx=ncc, n_all=nc)
    bias_c = dt_bias.reshape(2, 1, nh)
    bias_r = dt_bias.reshape(2, nh, 1)
    a_c = a_log.reshape(2, 1, nh)
    a_r = a_log.reshape(2, nh, 1)
    small_c = pl.BlockSpec((None, 1, nh), lambda b_, d, c: (d, 0, 0))
    small_r = pl.BlockSpec((None, nh, 1), lambda b_, d, c: (d, 0, 0))
    return pl.pallas_call(
        _ssd_body,
        out_shape=jax.ShapeDtypeStruct((2, b, t, di), F32),
        grid=(b, 2, nc),
        in_specs=[pl.BlockSpec((None, q, di), lambda b_, d, c: (b_, tc(d, c), 0)),
                  pl.BlockSpec((None, q, gn), lambda b_, d, c: (b_, tc(d, c), di // gn)),
                  pl.BlockSpec((None, q, gn), lambda b_, d, c: (b_, tc(d, c), di // gn + 1)),
                  pl.BlockSpec((None, None, q, nh), lambda b_, d, c: (d, b_, tc(d, c), 0)),
                  pl.BlockSpec((None, None, nh, q), lambda b_, d, c: (d, b_, 0, tc(d, c))),
                  small_c, small_r, small_c, small_r],
        out_specs=pl.BlockSpec((None, None, q, di), lambda b_, d, c: (d, b_, tc(d, c), 0)),
        scratch_shapes=[pltpu.VMEM((nh // 2, MB_STATE, LANES), F32)],
        compiler_params=_params(("parallel", "parallel", "arbitrary")),
        name="ssd_scan",
    )(xa, xa, xa, dtc, dtr, bias_c, bias_r, a_c, a_r)


def _mb_gate_body(y0_ref, y1_ref, xs_ref, z_ref, dv_ref, g_ref, o_ref):
    y = y0_ref[...] + y1_ref[...] + xs_ref[...] * dv_ref[...]
    o_ref[...] = _rms(y * _silu(z_ref[...]), g_ref[...]).astype(o_ref.dtype)


def _mb_gate(y, xa, zx, dvec, norm_g):
    _, b, t, di = y.shape
    return pl.pallas_call(
        _mb_gate_body,
        out_shape=jax.ShapeDtypeStruct((b, t, di), BF16),
        grid=(b, t // TM),
        in_specs=[_stream2(di, 0), _stream2(di, 1), _stream(di), _stream(di), _vec(di), _vec(di)],
        out_specs=_stream(di),
        compiler_params=_params(("parallel", "parallel")),
        name="mb_gate",
    )(y, y, xa, zx, dvec, norm_g.reshape(1, di))


def _mamba_layer(h, w, *, n_ctx_rows):
    b, t, d = h.shape
    m = b * t
    di = MB_HEADS * HEAD
    xbc = di + 2 * MB_GROUPS * MB_STATE
    h2 = h.reshape(m, d)
    zx = _matmul(h2, w["in_w"], n_cols=di + xbc).reshape(b, t, di + xbc)
    dt_raw = _matmul(h2, w["in_w_dt"], tn=LANES).reshape(b, t, LANES)[..., :2 * MB_HEADS]
    dtc = jnp.moveaxis(dt_raw.reshape(b, t, 2, MB_HEADS), 2, 0)
    dtr = jnp.swapaxes(dtc, 2, 3)
    xa = _mb_conv(zx, w["conv_w"], w["conv_b"], col0=di, nct=n_ctx_rows // TM)
    y = _ssd(xa, dtc, dtr, w["dt_bias"], w["a_log"], n_ctx_rows=n_ctx_rows)
    gated = _mb_gate(y, xa, zx, w["dvec"], w["norm_g"])
    return _matmul(gated.reshape(m, di), w["out_w"]).reshape(b, t, d)


def _rw_mix_body(h_ref, hp_ref, hn_ref, mix_ref, *o_refs, nt, nct):
    i = pl.program_id(1)
    first, last = _seq_edges(i, nt, nct)
    h = h_ref[...]
    prow = jnp.where(first, 0.0, hp_ref[7:8, :])
    nrow = jnp.where(last, 0.0, hn_ref[0:1, :])
    row = lax.broadcasted_iota(jnp.int32, h.shape, 0)
    prev = jnp.where(row == 0, prow, pltpu.roll(h, 1, axis=0))
    nxt = jnp.where(row == TM - 1, nrow, pltpu.roll(h, TM - 1, axis=0))
    xx = 0.5 * (prev + nxt) - h
    for j, o_ref in enumerate(o_refs):
        o_ref[...] = (h + xx * mix_ref[j:j + 1, :]).astype(o_ref.dtype)


def _rw_mix(h, mix, *, nct):
    b, t, d = h.shape
    nt = t // TM
    mix8 = jnp.pad(mix, ((0, 8 - mix.shape[0]), (0, 0)))
    return pl.pallas_call(
        functools.partial(_rw_mix_body, nt=nt, nct=nct),
        out_shape=tuple(jax.ShapeDtypeStruct((b, t, d), BF16) for _ in range(6)),
        grid=(b, nt),
        in_specs=[_stream(d), _halo(d, 8, t, -1), _halo(d, 8, t, +1), _vec(d, 8)],
        out_specs=tuple(_stream(d) for _ in range(6)),
        compiler_params=_params(("parallel", "parallel")),
        name="rw_mix",
    )(h, h, h, mix8)


def _bmm(a, b):
    return jnp.einsum('pij,pjk->pik', a.astype(BF16), b.astype(BF16), preferred_element_type=F32)


def _bmm_nt(a, b):
    return jnp.einsum('pik,pjk->pij', a.astype(BF16), b.astype(BF16), preferred_element_type=F32)


def _lane_pairs(x):
    return jnp.stack([x[:, p * LANES:(p + 1) * LANES] for p in range(x.shape[1] // LANES)], axis=0)


def _rw_chunk_body(r_ref, k_ref, v_ref, wl_ref, al_ref, kkk_ref, ka_ref, rk_ref,
                   rhat_ref, yhat_ref, w_ref, g_ref, gam_ref, bonus_ref):
    c = RW_CHUNK
    c2 = 2 * c
    d = r_ref.shape[1]
    r = r_ref[...]
    k = k_ref[...]
    v = v_ref[...]
    kkr = k * kkk_ref[...]
    kk = kkr / jnp.maximum(jnp.sqrt(_head_sum(kkr * kkr)), 1e-12)
    lws, kds, bbs = [], [], []
    for dr in range(2):
        lws.append(-RW_DECAY_SCALE * _sigmoid(wl_ref[:, dr * d:(dr + 1) * d]))
        a = _sigmoid(al_ref[:, dr * d:(dr + 1) * d])
        kds.append(k * (1.0 + (a - 1.0) * ka_ref[...]))
        bbs.append(kk * a)
    bonus_ref[...] = _head_sum(r * rk_ref[...] * (0.5 * (kds[0] + kds[1]))) * v

    ti = lax.broadcasted_iota(jnp.int32, (c, c), 0)
    si = lax.broadcasted_iota(jnp.int32, (c, c), 1)
    ri = lax.broadcasted_iota(jnp.int32, (c2, c2), 0)
    ci = lax.broadcasted_iota(jnp.int32, (c2, c2), 1)
    same = (ri >= c) == (ci >= c)
    dtm = (ri & (c - 1)) - (ci & (c - 1))
    eye = jnp.where(ri == ci, 1.0, 0.0)
    lo = lax.broadcasted_iota(jnp.int32, (c, LANES), 1) < HEAD

    def stack(x):
        return jnp.concatenate([jnp.where(lo, x, 0.0), jnp.where(lo, 0.0, x)], axis=1)

    def unstack(x2):
        return x2[:, 0:c] + x2[:, c:c2]

    v_p = _lane_pairs(v)
    v2 = stack(v_p)
    npair = v_p.shape[0]
    for dr in range(2):
        sgn = 1 - 2 * dr
        lw = lws[dr]
        cum = _dot01_left((ti - si) * sgn >= 0, lw)
        eg = jnp.exp(cum)
        einv = jnp.exp(-cum)
        at_p = _lane_pairs(-kk * jnp.exp(cum - lw))
        rt_p = _lane_pairs(r * eg)
        kt_p = _lane_pairs(kds[dr] * einv)
        bt_p = _lane_pairs(bbs[dr] * einv)
        gam_ref[dr] = eg[c - 1:c, :] if dr == 0 else eg[0:1, :]
        m_strict = jnp.logical_and(same, dtm * sgn > 0)
        m_incl = jnp.logical_and(same, dtm * sgn >= 0)

        at2 = stack(at_p)
        lhs = jnp.concatenate([at2, stack(rt_p)], axis=1)
        rhs = jnp.concatenate([bt_p, bt_p, kt_p, kt_p], axis=1)
        nn = _bmm_nt(lhs, rhs)
        n_ab = jnp.where(m_strict, nn[:, 0:c2, 0:c2], 0.0)
        n_ak = jnp.where(m_strict, nn[:, 0:c2, c2:2 * c2], 0.0)
        n_qb = jnp.where(m_incl, nn[:, c2:2 * c2, 0:c2], 0.0)
        n_qk = jnp.where(m_incl, nn[:, c2:2 * c2, c2:2 * c2], 0.0)
        tinv = eye + n_ab
        pw = n_ab
        for _ in range(int(math.log2(c)) - 1):
            pw = _bmm(pw, pw)
            tinv = tinv + _bmm(pw, tinv)
        z = _bmm(tinv, jnp.concatenate([at2, _bmm(n_ak, v2)], axis=2))
        x = _bmm(n_qb, z)
        rhat = rt_p + unstack(x[:, :, 0:LANES])
        yhat = unstack(x[:, :, LANES:] + _bmm(n_qk, v2))
        ah = unstack(z[:, :, 0:LANES])
        uh = unstack(z[:, :, LANES:])
        for p in range(npair):
            sl = slice(p * LANES, (p + 1) * LANES)
            rhat_ref[dr, :, sl] = rhat[p].astype(rhat_ref.dtype)
            yhat_ref[dr, :, sl] = yhat[p]
            w_ref[dr, p] = jnp.where(same, _bdot_tn(ah[p], bt_p[p]), 0.0).astype(w_ref.dtype)
            g = _bdot_tn(jnp.concatenate([v_p[p], uh[p]], axis=0), jnp.concatenate([kt_p[p], bt_p[p]], axis=0))
            g_ref[dr, p] = jnp.where(same, g, 0.0)


def _rw_chunks(r, k, v, wl, al, k_k, k_a, r_k):
    b, t, d = r.shape
    c = RW_CHUNK
    nc = t // c
    npair = d // LANES
    one = pl.BlockSpec((None, c, d), lambda b_, ch: (b_, ch, 0))
    wide = pl.BlockSpec((None, c, 2 * d), lambda b_, ch: (b_, ch, 0))
    two = pl.BlockSpec((2, None, c, d), lambda b_, ch: (0, b_, ch, 0))
    mats = pl.BlockSpec((2, None, None, npair, LANES, LANES), lambda b_, ch: (0, b_, ch, 0, 0, 0))
    gam = pl.BlockSpec((2, None, None, 1, d), lambda b_, ch: (0, b_, ch, 0, 0))
    vec = pl.BlockSpec((1, d), lambda b_, ch: (0, 0))
    return pl.pallas_call(
        _rw_chunk_body,
        out_shape=(jax.ShapeDtypeStruct((2, b, t, d), BF16), jax.ShapeDtypeStruct((2, b, t, d), F32),
                   jax.ShapeDtypeStruct((2, b, nc, npair, LANES, LANES), BF16),
                   jax.ShapeDtypeStruct((2, b, nc, npair, LANES, LANES), F32),
                   jax.ShapeDtypeStruct((2, b, nc, 1, d), F32),
                   jax.ShapeDtypeStruct((b, t, d), F32)),
        grid=(b, nc),
        in_specs=[one, one, one, wide, wide, vec, vec, vec],
        out_specs=(two, two, mats, mats, gam, one),
        compiler_params=_params(("parallel", "parallel")),
        name="rw_chunks",
    )(r, k, v, wl, al, k_k.reshape(1, d), k_a.reshape(1, d), r_k.reshape(1, d))


def _rw_state_body(rhat_ref, yhat_ref, w_ref, g_ref, gam_ref, y_ref, s_ref):
    @pl.when(pl.program_id(2) == 0)
    def _():
        s_ref[...] = jnp.zeros_like(s_ref)

    dr = pl.program_id(1)
    c = RW_CHUNK
    for k in range(2):
        sub = k + dr * (1 - 2 * k)
        rows = pl.ds(pl.multiple_of(sub * c, c), c)
        s = s_ref[...]
        sb = s.astype(BF16)
        y = _bmm_nt(_lane_pairs(rhat_ref[rows, :]), sb)
        y_ref[rows, :] = jnp.concatenate([y[p] for p in range(s.shape[0])], axis=1) + yhat_ref[rows, :]
        gam = _lane_pairs(gam_ref[sub])
        s_ref[...] = (s + _bmm(sb, w_ref[sub]) + g_ref[sub]) * gam


def _rw_state(rhat, yhat, wm, gm, gam, *, n_ctx_rows):
    _, b, t, d = rhat.shape
    c = RW_CHUNK
    nc = t // c
    npair = d // LANES
    tc = functools.partial(_scan_chunk, n_ctx=n_ctx_rows // (2 * c), n_all=nc // 2)
    two = pl.BlockSpec((None, None, 2 * c, d), lambda b_, dr, ch: (dr, b_, tc(dr, ch), 0))
    mats = pl.BlockSpec((None, None, 2, npair, LANES, LANES), lambda b_, dr, ch: (dr, b_, tc(dr, ch), 0, 0, 0))
    gsp = pl.BlockSpec((None, None, 2, 1, d), lambda b_, dr, ch: (dr, b_, tc(dr, ch), 0, 0))
    return pl.pallas_call(
        _rw_state_body,
        out_shape=jax.ShapeDtypeStruct((2, b, t, d), F32),
        grid=(b, 2, nc // 2),
        in_specs=[two, two, mats, mats, gsp],
        out_specs=two,
        scratch_shapes=[pltpu.VMEM((npair, LANES, LANES), F32)],
        compiler_params=_params(("parallel", "parallel", "arbitrary")),
        name="rw_state",
    )(rhat, yhat, wm, gm, gam)


def _rw_out_body(y0_ref, y1_ref, bonus_ref, g_ref, lg_ref, lb_ref, o_ref):
    y = y0_ref[...] + y1_ref[...]
    mu = _head_sum(y) * (1.0 / HEAD)
    yc = y - mu
    var = _head_sum(yc * yc) * (1.0 / HEAD)
    yn = yc * lax.rsqrt(var + RW_LN_EPS) * lg_ref[...] + lb_ref[...]
    o_ref[...] = ((yn + bonus_ref[...]) * g_ref[...]).astype(o_ref.dtype)


def _rw_out(y, bonus, g, ln_g, ln_b):
    b, t, d = bonus.shape
    return pl.pallas_call(
        _rw_out_body,
        out_shape=jax.ShapeDtypeStruct((b, t, d), BF16),
        grid=(b, t // TM),
        in_specs=[_stream2(d, 0), _stream2(d, 1), _stream(d), _stream(d), _vec(d), _vec(d)],
        out_specs=_stream(d),
        compiler_params=_params(("parallel", "parallel")),
        name="rw_out",
    )(y, y, bonus, g, ln_g.reshape(1, d), ln_b.reshape(1, d))


def _rwkv_layer(h, w, *, n_ctx_rows):
    b, t, d = h.shape
    m = b * t
    xr, xw, xk, xv, xa, xg = [a.reshape(m, d) for a in _rw_mix(h, w["mix"], nct=n_ctx_rows // TM)]
    r = _matmul(xr, w["r_w"]).reshape(b, t, d)
    k = _matmul(xk, w["k_w"]).reshape(b, t, d)
    v = _matmul(xv, w["v_w"]).reshape(b, t, d)
    g = _lora(xg, w["g1"], w["g2"], jnp.zeros((1, d), F32), "sigmoid").reshape(b, t, d)
    wl = _lora(xw, w["w1"], w["w2"], w["w0"], "tanh").reshape(b, t, 2 * d)
    al = _lora(xa, w["a1"], w["a2"], w["a0"], "none").reshape(b, t, 2 * d)
    rhat, yhat, wm, gm, gam, bonus = _rw_chunks(r, k, v, wl, al, w["k_k"], w["k_a"], w["r_k"])
    y = _rw_state(rhat, yhat, wm, gm, gam, n_ctx_rows=n_ctx_rows)
    o = _rw_out(y, bonus, g, w["ln_g"], w["ln_b"])
    return _matmul(o.reshape(m, d), w["out_w"]).reshape(b, t, d)


def _pool_body(h_ref, hp_ref, hn_ref, w_ref, sc_ref, o_ref, *, nt, nct):
    i = pl.program_id(1)
    first, last = _seq_edges(i, nt, nct)
    h = h_ref[...]
    halo = POOL_HALO
    ext = jnp.concatenate([jnp.where(first, 0.0, hp_ref[...]), h, jnp.where(last, 0.0, hn_ref[...])], axis=0)
    seq_start = jnp.where(i < nct, 0, nct)
    seq_len = jnp.where(i < nct, nct, nt - nct) * TM
    gw = h.shape[1] // len(POOL_WINDOWS)
    pos = (i - seq_start) * TM + lax.broadcasted_iota(jnp.int32, (TM, gw), 0)
    tr = lax.broadcasted_iota(jnp.int32, (TM, TM + 2 * halo), 0)
    er = lax.broadcasted_iota(jnp.int32, (TM, TM + 2 * halo), 1)
    for gi, win in enumerate(POOL_WINDOWS):
        half = win // 2
        band = jnp.logical_and(er >= tr + halo - half, er < tr + halo + half)
        cols = slice(gi * gw, (gi + 1) * gw)
        wsum = _dot01_left(band, ext[:, cols])
        cnt = (jnp.minimum(pos + half, seq_len) - jnp.maximum(pos - half, 0)).astype(F32)
        pooled = wsum / cnt - h[:, cols]
        o_ref[:, cols] = _bdot(pooled, w_ref[gi]) * sc_ref[:, cols]


def _pool_layer(h, pl_w, scale, *, nct):
    b, t, d = h.shape
    nt = t // TM
    ng, gw, _ = pl_w.shape
    return pl.pallas_call(
        functools.partial(_pool_body, nt=nt, nct=nct),
        out_shape=jax.ShapeDtypeStruct((b, t, d), F32),
        grid=(b, nt),
        in_specs=[_stream(d), _halo(d, POOL_HALO, t, -1), _halo(d, POOL_HALO, t, +1),
                  pl.BlockSpec((ng, gw, gw), lambda b_, i: (0, 0, 0)), _vec(d)],
        out_specs=_stream(d),
        compiler_params=_params(("parallel", "parallel")),
        name="pool_mixer",
    )(h, h, h, pl_w, scale.reshape(1, d))


def _at_prep_body(qkv_ref, qg_ref, kg_ref, cos_ref, sin_ref, q_ref, k_ref, v_ref, *, scale):
    nq = q_ref.shape[1]
    nk = AT_KV_HEADS * HEAD
    cos = cos_ref[...]
    sin = sin_ref[...]
    lane = lax.broadcasted_iota(jnp.int32, cos.shape, 1)
    up = (lane & 31) < 16
    lo = lane < HEAD

    def norm_rope(x, g):
        xn = x * lax.rsqrt(_head_sum(x * x) * (1.0 / HEAD) + NORM_EPS) * g
        swapped = jnp.where(up, pltpu.roll(xn, LANES - 16, axis=1), pltpu.roll(xn, 16, axis=1))
        return xn * cos + swapped * sin

    for s in range(nq // LANES):
        sl = slice(s * LANES, (s + 1) * LANES)
        q_ref[:, sl] = (norm_rope(qkv_ref[:, sl], qg_ref[...]) * scale).astype(q_ref.dtype)
    for s in range(nk // LANES):
        kslab = norm_rope(qkv_ref[:, nq + s * LANES:nq + (s + 1) * LANES], kg_ref[...])
        vslab = qkv_ref[:, nq + nk + s * LANES:nq + nk + (s + 1) * LANES]
        krolled = pltpu.roll(kslab, HEAD, axis=1)
        k_ref[2 * s] = jnp.where(lo, kslab, krolled).astype(k_ref.dtype)
        k_ref[2 * s + 1] = jnp.where(lo, krolled, kslab).astype(k_ref.dtype)
        v_ref[2 * s] = jnp.where(lo, vslab, 1.0).astype(v_ref.dtype)
        v_ref[2 * s + 1] = jnp.where(lo, pltpu.roll(vslab, HEAD, axis=1), 1.0).astype(v_ref.dtype)


def _at_prep(qkv, q_g, k_g, cos, sin):
    b, t, _ = qkv.shape
    nq = AT_HEADS * HEAD
    width = qkv.shape[2]
    kvs = jax.ShapeDtypeStruct((b, AT_KV_HEADS, t, LANES), BF16)
    kv_spec = pl.BlockSpec((None, AT_KV_HEADS, TM, LANES), lambda b_, i: (b_, 0, i, 0))
    tab = pl.BlockSpec((TM, LANES), lambda b_, i: (i, 0))
    tile2 = lambda g: jnp.tile(g.reshape(1, HEAD), (1, LANES // HEAD))
    return pl.pallas_call(
        functools.partial(_at_prep_body, scale=HEAD ** -0.5 * math.log2(math.e)),
        out_shape=(jax.ShapeDtypeStruct((b, t, nq), BF16), kvs, kvs),
        grid=(b, t // TM),
        in_specs=[_stream(width), _vec(LANES), _vec(LANES), tab, tab],
        out_specs=(_stream(nq), kv_spec, kv_spec),
        compiler_params=_params(("parallel", "parallel")),
        name="at_prep",
    )(qkv, tile2(q_g), tile2(k_g), cos, sin)


def _flash_body(q_ref, k_ref, v_ref, o_ref, sa_ref, sb_ref, *, tk):
    tq = q_ref.shape[0]
    n = k_ref.shape[0] // tk
    lane = lax.broadcasted_iota(jnp.int32, (tq, LANES), 1)
    lo = lane < HEAD
    q = q_ref[...]
    zero = jnp.zeros_like(q)
    qs = jnp.concatenate([jnp.where(lo, q, zero), jnp.where(lo, zero, q)], axis=0)

    def scores(j, dst_ref):
        start = pl.multiple_of(j * tk, tk)
        dst_ref[...] = _bdot_nt(qs, k_ref[pl.ds(start, tk), :])

    def absorb(j, src_ref, carry):
        m, acc = carry
        start = pl.multiple_of(j * tk, tk)
        s = src_ref[...]
        m_new = jnp.maximum(m, jnp.max(s, axis=-1, keepdims=True))
        p = jnp.exp2(s - m_new)
        acc = jnp.exp2(m - m_new) * acc + _bdot(p, v_ref[pl.ds(start, tk), :])
        return m_new, acc

    def two_chunks(i, carry):
        j = 2 * i
        scores(j + 1, sb_ref)
        carry = absorb(j, sa_ref, carry)
        scores(j + 2, sa_ref)
        return absorb(j + 1, sb_ref, carry)

    carry = (jnp.full((2 * tq, 1), -jnp.inf, F32), jnp.zeros((2 * tq, LANES), F32))
    scores(0, sa_ref)
    carry = lax.fori_loop(0, (n - 1) // 2, two_chunks, carry)
    if n % 2 == 0:
        scores(n - 1, sb_ref)
        carry = absorb(n - 2, sa_ref, carry)
        carry = absorb(n - 1, sb_ref, carry)
    else:
        carry = absorb(n - 1, sa_ref, carry)
    acc = carry[1]
    o2 = acc / pltpu.roll(acc, HEAD, axis=1)
    o_ref[...] = jnp.where(lo, o2[0:tq], pltpu.roll(o2[tq:2 * tq], HEAD, axis=1)).astype(o_ref.dtype)


def _flash(q, k2, v2, *, n_ctx_rows):
    b, t, nq = q.shape
    tq = TM
    tl = t - n_ctx_rows
    off = n_ctx_rows // tq
    tk = _pick(t, (2816, 1408, 768, 512, 256, 128))
    npair = nq // LANES
    hp = AT_HEADS // AT_KV_HEADS // 2
    kv = pl.BlockSpec((None, None, t, LANES), lambda b_, p, i: (b_, p // hp, 0, 0))
    return pl.pallas_call(
        functools.partial(_flash_body, tk=tk),
        out_shape=jax.ShapeDtypeStruct((b, tl, nq), BF16),
        grid=(b, npair, tl // tq),
        in_specs=[pl.BlockSpec((None, tq, LANES), lambda b_, p, i: (b_, i + off, p)), kv, kv],
        out_specs=pl.BlockSpec((None, tq, LANES), lambda b_, p, i: (b_, i, p)),
        scratch_shapes=[pltpu.VMEM((2 * tq, tk), F32), pltpu.VMEM((2 * tq, tk), F32)],
        compiler_params=_params(("parallel", "parallel", "parallel")),
        name="flash_gqa",
    )(q, k2, v2)


def _rope_tables(n_ctx_rows, seq):
    quarter = HEAD // 4
    inv = ROPE_THETA ** (-jnp.arange(quarter, dtype=F32) / quarter)
    rows = jnp.repeat(jnp.arange(seq // GRID_W, dtype=jnp.int32), GRID_W).astype(F32)
    cols = (jnp.arange(seq, dtype=jnp.int32) % GRID_W).astype(F32)
    ar = rows[:, None] * inv
    ac = cols[:, None] * inv
    cos = jnp.concatenate([jnp.cos(ar), jnp.cos(ar), jnp.cos(ac), jnp.cos(ac)], axis=1)
    sin = jnp.concatenate([-jnp.sin(ar), jnp.sin(ar), -jnp.sin(ac), jnp.sin(ac)], axis=1)
    cos = jnp.concatenate([jnp.ones((n_ctx_rows, HEAD), F32), cos], axis=0)
    sin = jnp.concatenate([jnp.zeros((n_ctx_rows, HEAD), F32), sin], axis=0)
    return jnp.tile(cos, (1, LANES // HEAD)), jnp.tile(sin, (1, LANES // HEAD))


def _attn_layer(h, w, *, n_ctx_rows):
    b, t, d = h.shape
    qkv = _matmul(h.reshape(b * t, d), w["qkv_w"]).reshape(b, t, -1)
    cos, sin = _rope_tables(n_ctx_rows, t - n_ctx_rows)
    q, k2, v2 = _at_prep(qkv, w["q_g"], w["k_g"], cos, sin)
    o = _flash(q, k2, v2, n_ctx_rows=n_ctx_rows)
    tl = t - n_ctx_rows
    return _matmul(o.reshape(b * tl, -1), w["out_w"]).reshape(b, tl, d)


def _ffn(h2, w_in, w_out):
    return _matmul(_swiglu_in(h2, w_in), w_out)


MOE_ROWS = 512
MOE_TOK = 512


def _moe_rank_body(sel_ref, rank_ref, rankt_ref, selt_ref, tot_ref, carry_ref, carryt_ref):
    @pl.when(pl.program_id(0) == 0)
    def _():
        carry_ref[...] = jnp.zeros_like(carry_ref)
        carryt_ref[...] = jnp.zeros_like(carryt_ref)

    sel = sel_ref[...]
    tb = sel.shape[0]
    ri = lax.broadcasted_iota(jnp.int32, (tb, tb), 0)
    ci = lax.broadcasted_iota(jnp.int32, (tb, tb), 1)
    rank_ref[...] = _bdot(_as01(ri > ci), sel) + carry_ref[0:1, :]
    pick = lax.broadcasted_iota(jnp.int32, (8, LANES), 0) == lax.broadcasted_iota(jnp.int32, (8, LANES), 1)
    selt = _bdot_nt(_as01(pick), sel)
    selt_ref[...] = selt
    rankt_ref[...] = _bdot(selt, _as01(ri < ci)) + carryt_ref[:, 0:1]
    carry_ref[...] = carry_ref[...] + jnp.sum(sel, axis=0, keepdims=True)
    carryt_ref[...] = carryt_ref[...] + jnp.sum(selt, axis=1, keepdims=True)
    tot_ref[...] = carry_ref[...]


def _moe_rank(sel):
    m = sel.shape[0]
    tb = MOE_TOK
    return pl.pallas_call(
        _moe_rank_body,
        out_shape=(jax.ShapeDtypeStruct((m, LANES), F32), jax.ShapeDtypeStruct((8, m), F32),
                   jax.ShapeDtypeStruct((8, m), F32), jax.ShapeDtypeStruct((8, LANES), F32)),
        grid=(m // tb,),
        in_specs=[pl.BlockSpec((tb, LANES), lambda i: (i, 0))],
        out_specs=(pl.BlockSpec((tb, LANES), lambda i: (i, 0)), pl.BlockSpec((8, tb), lambda i: (0, i)),
                   pl.BlockSpec((8, tb), lambda i: (0, i)), pl.BlockSpec((8, LANES), lambda i: (0, 0))),
        scratch_shapes=[pltpu.VMEM((8, LANES), F32), pltpu.VMEM((8, LANES), F32)],
        compiler_params=_params(("arbitrary",)),
        name="moe_rank",
    )(sel)


def _moe_plan(sel, rank, rankt, selt, tot, m):
    tr, tb, ne = MOE_ROWS, MOE_TOK, N_EXPERTS
    nb = m // tb
    nt = 2 * m // tr + ne
    i32 = jnp.int32
    cnt = tot[0, :ne].astype(i32)
    tile_start = jnp.concatenate([jnp.zeros((1,), i32), jnp.cumsum((cnt + tr - 1) // tr)])
    off = tile_start[:ne] * tr
    n_tiles = tile_start[ne]
    count_le = lambda ends, x: jnp.sum(ends[None, :] <= x[:, None], axis=1).astype(i32)
    tiles = jnp.arange(nt, dtype=i32)
    tile_valid = tiles < n_tiles
    tile_exp = jnp.minimum(count_le(tile_start[1:], tiles), ne - 1)
    last_exp = tile_exp[jnp.maximum(n_tiles - 1, 0)]
    tile_exp = jnp.where(tile_valid, tile_exp, last_exp)
    offp = jnp.pad(off.astype(F32), (0, LANES - ne))
    posmat = jnp.where(sel > 0, rank + offp[None, :], -1.0)
    post = jnp.where(selt > 0, rankt + jnp.pad(off.astype(F32), (0, 8 - ne))[:, None], -1.0)
    blkcum = jnp.concatenate([rank[::tb, :ne].astype(i32).T, cnt[:, None]], axis=1)

    k_lo = tiles * tr - off[tile_exp]
    k_hi = jnp.minimum(cnt[tile_exp], k_lo + tr) - 1
    ends = blkcum[tile_exp][:, 1:]
    lo_blk = jnp.minimum(jnp.sum(ends <= k_lo[:, None], axis=1), nb - 1).astype(i32)
    hi_blk = jnp.minimum(jnp.sum(ends <= k_hi[:, None], axis=1), nb - 1).astype(i32)
    span = jnp.where(tile_valid, hi_blk - lo_blk + 1, 1)
    lo_blk = jnp.where(tile_valid, lo_blk, 0)
    g_end = jnp.cumsum(span)
    g_start = g_end - span
    ns = nt + ne * nb
    steps = jnp.arange(ns, dtype=i32)
    g_tile = jnp.minimum(count_le(g_end, steps), nt - 1)
    g_valid = (steps < g_end[-1]).astype(i32)
    g_blk = jnp.clip(lo_blk[g_tile] + steps - g_start[g_tile], 0, nb - 1).astype(i32)
    g_first = (steps == g_start[g_tile]).astype(i32)
    gather = (g_tile, g_blk, tile_exp[g_tile], g_first, g_valid)

    r_lo = off[:, None] + blkcum[:, :-1]
    r_hi = off[:, None] + blkcum[:, 1:] - 1
    t_lo = (r_lo // tr).T.reshape(-1)
    n_t = jnp.where(r_hi >= r_lo, r_hi // tr - r_lo // tr + 1, 0).T.reshape(-1)
    c_end = jnp.cumsum(n_t)
    c_start = c_end - n_t
    idx = jnp.minimum(count_le(c_end, steps), nb * ne - 1)
    c_valid = (steps < c_end[-1]).astype(i32)
    c_blk = idx // ne
    c_tile = jnp.clip(t_lo[idx] + steps - c_start[idx], 0, nt - 1).astype(i32)
    c_first = (steps == c_start[c_blk * ne]).astype(i32)
    combine = (c_blk, c_tile, idx % ne, c_first, c_valid)
    return dict(nt=nt, ns=ns, tile_exp=tile_exp, n_tiles=n_tiles.reshape(1), posmat=posmat, post=post,
                gather=gather, combine=combine)


def _moe_gather_body(g_tile, g_blk, g_exp, g_first, g_valid, h_ref, post_ref, xs_ref):
    s = pl.program_id(0)

    @pl.when(g_valid[s] == 1)
    def _():
        tr, tb = xs_ref.shape[0], h_ref.shape[0]
        rows = (lax.broadcasted_iota(jnp.int32, (tr, tb), 0) + g_tile[s] * tr).astype(F32)
        onehot = post_ref[pl.ds(g_exp[s], 1), :] == rows
        x = _bdot(_as01(onehot), h_ref[...]).astype(xs_ref.dtype)

        @pl.when(g_first[s] == 1)
        def _():
            xs_ref[...] = x

        @pl.when(g_first[s] == 0)
        def _():
            xs_ref[...] = xs_ref[...] + x


def _moe_gather(h2, post, plan):
    m, d = h2.shape
    tr, tb = MOE_ROWS, MOE_TOK
    grid_spec = pltpu.PrefetchScalarGridSpec(
        num_scalar_prefetch=5, grid=(plan["ns"],),
        in_specs=[pl.BlockSpec((tb, d), lambda s, gt, gb, ge, gf, gv: (gb[s], 0)),
                  pl.BlockSpec((8, tb), lambda s, gt, gb, ge, gf, gv: (0, gb[s]))],
        out_specs=pl.BlockSpec((tr, d), lambda s, gt, gb, ge, gf, gv: (gt[s], 0)))
    return pl.pallas_call(
        _moe_gather_body,
        out_shape=jax.ShapeDtypeStruct((plan["nt"] * tr, d), BF16),
        grid_spec=grid_spec,
        compiler_params=_params(("arbitrary",)),
        name="moe_gather",
    )(*plan["gather"], h2, post)


def _moe_expert_body(t_exp, n_tiles, xs_ref, wg_ref, wu_ref, wo_ref, y_ref, acc_ref):
    j = pl.program_id(0)
    f = pl.program_id(1)

    @pl.when(f == 0)
    def _():
        acc_ref[...] = jnp.zeros_like(acc_ref)

    @pl.when(j < n_tiles[0])
    def _():
        x = xs_ref[...]
        act = _silu(_bdot(x, wg_ref[...])) * _bdot(x, wu_ref[...])
        acc_ref[...] += _bdot(act, wo_ref[...])

    @pl.when(f == pl.num_programs(1) - 1)
    def _():
        y_ref[...] = acc_ref[...].astype(y_ref.dtype)


def _moe_experts(xs, w_in, w_out, layer, plan):
    rows, d = xs.shape
    tr = MOE_ROWS
    f = w_in.shape[3] // 2
    fk = _pick(f, (896, 512, 256, 128))
    nf = f // fk
    hold = lambda j, fi, nt: jnp.where(j < nt[0], fi, nf - 1)
    grid_spec = pltpu.PrefetchScalarGridSpec(
        num_scalar_prefetch=2, grid=(rows // tr, nf),
        in_specs=[pl.BlockSpec((tr, d), lambda j, fi, te, nt: (j, 0)),
                  pl.BlockSpec((None, None, d, fk), lambda j, fi, te, nt: (layer, te[j], 0, hold(j, fi, nt))),
                  pl.BlockSpec((None, None, d, fk), lambda j, fi, te, nt: (layer, te[j], 0, hold(j, fi, nt) + nf)),
                  pl.BlockSpec((None, None, fk, d), lambda j, fi, te, nt: (layer, te[j], hold(j, fi, nt), 0))],
        out_specs=pl.BlockSpec((tr, d), lambda j, fi, te, nt: (j, 0)),
        scratch_shapes=[pltpu.VMEM((tr, d), F32)])
    return pl.pallas_call(
        _moe_expert_body,
        out_shape=jax.ShapeDtypeStruct((rows, d), BF16),
        grid_spec=grid_spec,
        compiler_params=_params(("arbitrary", "arbitrary")),
        name="moe_experts",
    )(plan["tile_exp"], plan["n_tiles"], xs, w_in, w_in, w_out)


def _moe_combine_body(c_blk, c_tile, c_exp, c_first, c_valid, pos_ref, gates_ref, y_ref, o_ref):
    s = pl.program_id(0)

    @pl.when(c_valid[s] == 1)
    def _():
        tb, tr = o_ref.shape[0], y_ref.shape[0]
        lane = lax.broadcasted_iota(jnp.int32, pos_ref.shape, 1)
        mine = lane == c_exp[s]
        pos_e = jnp.sum(jnp.where(mine, pos_ref[...], 0.0), axis=-1, keepdims=True)
        gate = jnp.sum(jnp.where(mine, gates_ref[...], 0.0), axis=-1, keepdims=True)
        rows = (lax.broadcasted_iota(jnp.int32, (tb, tr), 1) + c_tile[s] * tr).astype(F32)
        y = gate * jnp.dot(_as01(pos_e == rows), y_ref[...], preferred_element_type=F32)

        @pl.when(c_first[s] == 1)
        def _():
            o_ref[...] = y

        @pl.when(c_first[s] == 0)
        def _():
            o_ref[...] = o_ref[...] + y


def _moe_combine(posmat, gates, y, plan):
    m = posmat.shape[0]
    d = y.shape[1]
    tr, tb = MOE_ROWS, MOE_TOK
    grid_spec = pltpu.PrefetchScalarGridSpec(
        num_scalar_prefetch=5, grid=(plan["ns"],),
        in_specs=[pl.BlockSpec((tb, LANES), lambda s, cb, ct, ce, cf, cv: (cb[s], 0)),
                  pl.BlockSpec((tb, LANES), lambda s, cb, ct, ce, cf, cv: (cb[s], 0)),
                  pl.BlockSpec((tr, d), lambda s, cb, ct, ce, cf, cv: (ct[s], 0))],
        out_specs=pl.BlockSpec((tb, d), lambda s, cb, ct, ce, cf, cv: (cb[s], 0)))
    return pl.pallas_call(
        _moe_combine_body,
        out_shape=jax.ShapeDtypeStruct((m, d), F32),
        grid_spec=grid_spec,
        compiler_params=_params(("arbitrary",)),
        name="moe_combine",
    )(*plan["combine"], posmat, gates, y)


def _moe(h2, gates, sel, w_in, w_out, layer):
    m = h2.shape[0]
    rank, rankt, selt, tot = _moe_rank(sel)
    plan = _moe_plan(sel, rank, rankt, selt, tot, m)
    xs = _moe_gather(h2, plan["post"], plan)
    y = _moe_experts(xs, w_in, w_out, layer, plan)
    return _moe_combine(plan["posmat"], gates, y, plan)


def kernel(x, c, ctx, c_ctx, mod_w, mod_b, norm1_g, norm2_g, final_g, mb_in_w, mb_conv_w, mb_conv_b, mb_dt_bias, mb_a_log, mb_d, mb_norm_g, mb_out_w, rw_mix, rw_rkv_w, rw_w0, rw_w1, rw_w2, rw_a0, rw_a1, rw_a2, rw_g1, rw_g2, rw_k_k, rw_k_a, rw_r_k, rw_ln_g, rw_ln_b, rw_out_w, pl_w, pl_scale, at_qkv_w, at_q_g, at_k_g, at_out_w, ff_in_w, ff_out_w, moe_router_w, moe_in_w, moe_out_w):
    b, seq, d = x.shape
    n_ctx = ctx.shape[1]
    depth = mod_w.shape[0]
    t = n_ctx + seq
    nct = n_ctx // TM
    assert depth == 4 and n_ctx % TM == 0 and seq % TM == 0 and b + 1 <= 8
    bf = lambda a: a.astype(BF16)

    svec = jnp.concatenate([c, c_ctx[None, :], jnp.zeros((8 - b - 1, d), F32)], axis=0)
    mods = _mods(svec, bf(mod_w), mod_b)
    mv = jnp.pad(mods[:, :b + 1].reshape(depth, b + 1, 6, d), ((0, 0), (0, 0), (0, 2), (0, 0)))

    xs = jnp.concatenate([ctx, x], axis=1)
    m = b * t
    resid = functools.partial(_resid_norm, nct=nct, x_off=0)

    di = MB_HEADS * HEAD
    xbc = di + 2 * MB_GROUPS * MB_STATE
    mamba_w = dict(in_w=bf(mb_in_w[0]),
                   in_w_dt=bf(jnp.pad(mb_in_w[0][:, di + xbc:], ((0, 0), (0, LANES - 2 * MB_HEADS)))),
                   conv_w=mb_conv_w[0], conv_b=mb_conv_b[0], dt_bias=mb_dt_bias[0], a_log=mb_a_log[0],
                   dvec=jnp.repeat(mb_d[0], HEAD).reshape(1, di), norm_g=mb_norm_g[0], out_w=bf(mb_out_w[0]))
    h = _norm_mod(xs, norm1_g[0], mv, 0, nct, BF16)
    y = _mamba_layer(h, mamba_w, n_ctx_rows=n_ctx)
    xs, h2 = resid(xs, y, mv, norm2_g[0], gate_layer=0, jg=2, mod_layer=0, js=3, jc=4, out_dtype=BF16)
    f = _ffn(h2.reshape(m, d), bf(ff_in_w[0]), bf(ff_out_w[0])).reshape(b, t, d)
    xs, h = resid(xs, f, mv, norm1_g[1], gate_layer=0, jg=5, mod_layer=1, js=0, jc=1, out_dtype=F32)

    blockdiag = lambda u: jnp.concatenate(
        [jnp.concatenate([u[0], jnp.zeros_like(u[0])], axis=1),
         jnp.concatenate([jnp.zeros_like(u[1]), u[1]], axis=1)], axis=0)
    rwkv_w = dict(mix=rw_mix[0], r_w=bf(rw_rkv_w[0, 0]), k_w=bf(rw_rkv_w[0, 1]), v_w=bf(rw_rkv_w[0, 2]),
                  g1=bf(rw_g1[0]), g2=bf(rw_g2[0]),
                  w1=bf(jnp.concatenate([rw_w1[0, 0], rw_w1[0, 1]], axis=1)), w2=bf(blockdiag(rw_w2[0])),
                  w0=rw_w0[0].reshape(1, 2 * d),
                  a1=bf(jnp.concatenate([rw_a1[0, 0], rw_a1[0, 1]], axis=1)), a2=bf(blockdiag(rw_a2[0])),
                  a0=rw_a0[0].reshape(1, 2 * d),
                  k_k=rw_k_k[0], k_a=rw_k_a[0], r_k=rw_r_k[0].reshape(d), ln_g=rw_ln_g[0], ln_b=rw_ln_b[0],
                  out_w=bf(rw_out_w[0]))
    y = _rwkv_layer(h, rwkv_w, n_ctx_rows=n_ctx)
    moe_in_b, moe_out_b = bf(moe_in_w), bf(moe_out_w)
    xs, h2, gates, sel = _resid_norm_router(xs, y, mv, norm2_g[1], moe_router_w[0], gate_layer=1, jg=2,
                                            mod_layer=1, js=3, jc=4, nct=nct, x_off=0)
    f = _moe(h2.reshape(m, d), gates.reshape(m, LANES), sel.reshape(m, LANES), moe_in_b, moe_out_b, 0)
    f = f.reshape(b, t, d)
    xs, h = resid(xs, f, mv, norm1_g[2], gate_layer=1, jg=5, mod_layer=2, js=0, jc=1, out_dtype=F32)

    y = _pool_layer(h, bf(pl_w[0]), pl_scale[0], nct=nct)
    xs, h2 = resid(xs, y, mv, norm2_g[2], gate_layer=2, jg=2, mod_layer=2, js=3, jc=4, out_dtype=BF16)
    f = _ffn(h2.reshape(m, d), bf(ff_in_w[1]), bf(ff_out_w[1])).reshape(b, t, d)
    xs, h = resid(xs, f, mv, norm1_g[3], gate_layer=2, jg=5, mod_layer=3, js=0, jc=1, out_dtype=BF16)

    attn_w = dict(qkv_w=bf(at_qkv_w[0]), q_g=at_q_g[0], k_g=at_k_g[0], out_w=bf(at_out_w[0]))
    y = _attn_layer(h, attn_w, n_ctx_rows=n_ctx)
    xl, h2, gates, sel = _resid_norm_router(xs, y, mv, norm2_g[3], moe_router_w[1], gate_layer=3, jg=2,
                                            mod_layer=3, js=3, jc=4, nct=0, x_off=nct)
    ml = b * seq
    f = _moe(h2.reshape(ml, d), gates.reshape(ml, LANES), sel.reshape(ml, LANES), moe_in_b, moe_out_b, 1)
    return _resid_final(xl, f.reshape(b, seq, d), mv, final_g, gate_layer=3, jg=5)
```
